```python
import jax, jax.numpy as jnp
from jax import lax
import numpy as np


D_MODEL = 1024
BATCH = 4
SEQ = 4096
DEPTH = 4
DEC_BATCH = 32
DEC_SEQ = 1
PAST_LEN = 8192
PAGE_SIZE = 128

N_EVEN = (DEPTH + 1) // 2
N_ODD = DEPTH // 2
D_CONV = D_MODEL // 2
CONV_W = 3
N_HEADS_B = 8
HEAD_DIM_B = (D_MODEL // 2) // N_HEADS_B
D_ATT = N_HEADS_B * HEAD_DIM_B
DILATED_PAIRS = ((128, 1), (512, 4), (2048, 16))
WINDOW_MAX = max(w for w, _ in DILATED_PAIRS)
ATT_BLOCK = 128
NEG_INF = -1e30
POOL_WINDOWS = (2, 4, 8, 16)
POOL_GROUP = D_MODEL // len(POOL_WINDOWS)
POOL_STATE = max(POOL_WINDOWS) - 1
PEER_HEADS = 8
PEER_QDIM = 256
PEER_HALF = PEER_QDIM // 2
N_KEYS = 128
N_EXPERTS = N_KEYS * N_KEYS
PEER_TOPK = 16
PEER_CHUNK = 256
NORM_EPS = 1e-6

kernel_name = 'hybrid_conv_dilated_pool_peer_step'


def rms_norm(x, g):
    x32 = x.astype(jnp.float32)
    r = lax.rsqrt(jnp.mean(x32 * x32, axis=-1, keepdims=True) + NORM_EPS)
    return (x32 * r).astype(x.dtype) * g


def split_projection(h, w_in):
    p = h @ w_in
    cuts = [D_CONV, 2 * D_CONV, 3 * D_CONV, 3 * D_CONV + D_ATT, 3 * D_CONV + 2 * D_ATT]
    return jnp.split(p, cuts, axis=-1)


def gated_short_conv(gate_b, u_hist, conv_w):
    T = gate_b.shape[1]
    y = conv_w[0] * u_hist[:, 0:T]
    for i in range(1, CONV_W):
        y = y + conv_w[i] * u_hist[:, i:i + T]
    return gate_b * y


def to_heads(t):
    return t.reshape(t.shape[0], t.shape[1], N_HEADS_B, HEAD_DIM_B)


def dilated_branch_prompt(q, k, v, window, dil):
    b, S, H, hd = q.shape
    n_back = window // dil
    L = S // dil
    nb = -(-L // ATT_BLOCK)
    pad_end = nb * ATT_BLOCK - L

    def residues(t):
        return t.reshape(b, L, dil, H, hd).transpose(0, 2, 1, 3, 4).reshape(b * dil, L, H, hd)

    qb = jnp.pad(residues(q), ((0, 0), (0, pad_end), (0, 0), (0, 0))).reshape(b * dil, nb, ATT_BLOCK, H, hd)

    def band(t):
        tb = jnp.pad(residues(t), ((0, 0), (ATT_BLOCK, pad_end), (0, 0), (0, 0)))
        tb = tb.reshape(b * dil, nb + 1, ATT_BLOCK, H, hd)
        return jnp.concatenate([tb[:, :-1], tb[:, 1:]], axis=2)

    kb, vb = band(k), band(v)
    s = jnp.einsum('bnqhd,bnkhd->bnhqk', qb, kb)
    qi = jnp.arange(ATT_BLOCK)[:, None]
    ki = jnp.arange(2 * ATT_BLOCK)[None, :]
    dist = qi + ATT_BLOCK - ki
    key_pos = (jnp.arange(nb) * ATT_BLOCK - ATT_BLOCK)[:, None, None] + ki[None]
    mask = ((dist >= 0) & (dist <= n_back))[None] & (key_pos >= 0)
    s = jnp.where(mask[None, :, None], s, NEG_INF)
    m = jnp.max(s, axis=-1, keepdims=True)
    p = jnp.exp(s - m)
    den = jnp.sum(p, axis=-1)
    o = jnp.einsum('bnhqk,bnkhd->bnqhd', p, vb) / den.transpose(0, 1, 3, 2)[..., None]
    lse = (m[..., 0] + jnp.log(den)).transpose(0, 1, 3, 2)
    o = o.reshape(b * dil, nb * ATT_BLOCK, H, hd)[:, :L]
    lse = lse.reshape(b * dil, nb * ATT_BLOCK, H)[:, :L]
    o = o.reshape(b, dil, L, H, hd).transpose(0, 2, 1, 3, 4).reshape(b, S, H, hd)
    lse = lse.reshape(b, dil, L, H).transpose(0, 2, 1, 3).reshape(b, S, H)
    return o, lse


def dilated_branch_sample(q, k_all, v_all, window, dil, wb):
    T = q.shape[1]
    n_back = window // dil
    idx = wb + jnp.arange(T)[:, None] - jnp.arange(n_back + 1)[None, :] * dil
    valid = idx >= 0
    idx = jnp.maximum(idx, 0)
    kg = k_all[:, idx]
    vg = v_all[:, idx]
    s = jnp.einsum('bthd,btjhd->bhtj', q, kg)
    s = jnp.where(valid[None, None], s, NEG_INF)
    m = jnp.max(s, axis=-1, keepdims=True)
    p = jnp.exp(s - m)
    den = jnp.sum(p, axis=-1)
    o = jnp.einsum('bhtj,btjhd->bthd', p, vg) / den.transpose(0, 2, 1)[..., None]
    lse = (m[..., 0] + jnp.log(den)).transpose(0, 2, 1)
    return o, lse


def combine_branches(branches):
    o = jnp.stack([br[0] for br in branches])
    lse = jnp.stack([br[1] for br in branches])
    wts = jax.nn.softmax(lse, axis=0)
    return jnp.sum(wts[..., None] * o, axis=0)


def even_mixer_prompt(h, w_in, conv_w, w_out):
    b, S, _ = h.shape
    gate_b, gate_c, xv, q, k, v = split_projection(h, w_in)
    u_hist = jnp.pad(gate_c * xv, ((0, 0), (CONV_W - 1, 0), (0, 0)))
    y_a = gated_short_conv(gate_b, u_hist, conv_w)
    q, k, v = to_heads(q), to_heads(k), to_heads(v)
    qf = q.astype(jnp.float32) * HEAD_DIM_B ** -0.5
    kf, vf = k.astype(jnp.float32), v.astype(jnp.float32)
    y_b = combine_branches([dilated_branch_prompt(qf, kf, vf, w, d) for w, d in DILATED_PAIRS])
    y_b = y_b.reshape(b, S, D_ATT).astype(h.dtype)
    out = jnp.concatenate([y_a, y_b], axis=-1) @ w_out
    wb = min(WINDOW_MAX, S)
    return out, u_hist[:, -(CONV_W - 1):], k[:, S - wb:], v[:, S - wb:]


def even_mixer_sample(h, conv_state, k_state, v_state, w_in, conv_w, w_out):
    b, T, _ = h.shape
    gate_b, gate_c, xv, q, k, v = split_projection(h, w_in)
    u_hist = jnp.concatenate([conv_state.astype(h.dtype), gate_c * xv], axis=1)
    y_a = gated_short_conv(gate_b, u_hist, conv_w)
    q, k, v = to_heads(q), to_heads(k), to_heads(v)
    wb = k_state.shape[1]
    k_all = jnp.concatenate([k_state.astype(h.dtype), k], axis=1)
    v_all = jnp.concatenate([v_state.astype(h.dtype), v], axis=1)
    qf = q.astype(jnp.float32) * HEAD_DIM_B ** -0.5
    kf, vf = k_all.astype(jnp.float32), v_all.astype(jnp.float32)
    y_b = combine_branches([dilated_branch_sample(qf, kf, vf, w, d, wb) for w, d in DILATED_PAIRS])
    y_b = y_b.reshape(b, T, D_ATT).astype(h.dtype)
    out = jnp.concatenate([y_a, y_b], axis=-1) @ w_out
    return out, u_hist[:, -(CONV_W - 1):], k_all[:, -wb:], v_all[:, -wb:]


def pool_mix(u, pool_w, pool_b, pool_scale, n_out):
    b, L, _ = u.shape
    u32 = u.astype(jnp.float32)
    csum = jnp.concatenate([jnp.zeros_like(u32[:, :1]), jnp.cumsum(u32, axis=1)], axis=1)
    rows = jnp.arange(1, L + 1, dtype=jnp.float32)
    groups = []
    for g, w in enumerate(POOL_WINDOWS):
        c = csum[:, :, g * POOL_GROUP:(g + 1) * POOL_GROUP]
        lagged = jnp.pad(c[:, :L + 1 - w], ((0, 0), (w - 1, 0), (0, 0)))
        mean = (c[:, 1:] - lagged) / jnp.minimum(rows, w)[None, :, None]
        groups.append(mean - u32[:, :, g * POOL_GROUP:(g + 1) * POOL_GROUP])
    y = jnp.stack(groups, axis=2)[:, L - n_out:].astype(u.dtype)
    y = jnp.einsum('btgc,gcd->btgd', y, pool_w) + pool_b
    return y.reshape(b, n_out, D_MODEL) * pool_scale


def peer_ffn(h, wq, subkeys, u_tab, v_tab):
    b, T, D = h.shape
    n = b * T
    chunk = min(PEER_CHUNK, n)
    n_chunks = -(-n // chunk)
    hc = jnp.pad(h.reshape(n, D), ((0, n_chunks * chunk - n), (0, 0))).reshape(n_chunks, chunk, D)
    sk = subkeys.astype(jnp.float32)

    def retrieve(hb):
        q = (hb @ wq).astype(jnp.float32).reshape(chunk, PEER_HEADS, 2, PEER_HALF)
        s = jnp.einsum('nhcd,hckd->nhck', q, sk)
        sv, si = lax.top_k(s, PEER_TOPK)
        cand = (sv[:, :, 0, :, None] + sv[:, :, 1, None, :]).reshape(chunk, PEER_HEADS, PEER_TOPK * PEER_TOPK)
        cs, ci = lax.top_k(cand, PEER_TOPK)
        i1 = jnp.take_along_axis(si[:, :, 0], ci // PEER_TOPK, axis=-1)
        i2 = jnp.take_along_axis(si[:, :, 1], ci % PEER_TOPK, axis=-1)
        expert = i1 * N_KEYS + i2
        g = jax.nn.softmax(cs, axis=-1)
        act = jax.nn.gelu(jnp.einsum('nd,nhkd->nhk', hb, u_tab[expert]).astype(jnp.float32), approximate=False)
        return jnp.einsum('nhk,nhkd->nd', (g * act).astype(hb.dtype), v_tab[expert])

    out = lax.map(retrieve, hc)
    return out.reshape(n_chunks * chunk, D)[:n].reshape(b, T, D)


def setup_inputs(seed: int = 0) -> dict:
    key = jax.random.key(seed)
    ks = jax.random.split(key, 20)
    f32 = jnp.float32
    wb = min(WINDOW_MAX, PAST_LEN)

    def nrm(k, shape, scale):
        return jax.random.normal(k, shape, f32) * scale

    return {
        'x_prompt': nrm(ks[0], (BATCH, SEQ, D_MODEL), 1.0),
        'x_sample': nrm(ks[1], (DEC_BATCH, DEC_SEQ, D_MODEL), 1.0),
        'state_conv': nrm(ks[2], (N_EVEN, DEC_BATCH, CONV_W - 1, D_CONV), 1.0),
        'state_win_k': nrm(ks[3], (N_EVEN, DEC_BATCH, wb, N_HEADS_B, HEAD_DIM_B), 1.0),
        'state_win_v': nrm(ks[4], (N_EVEN, DEC_BATCH, wb, N_HEADS_B, HEAD_DIM_B), 1.0),
        'state_pool': nrm(ks[5], (N_ODD, DEC_BATCH, POOL_STATE, D_MODEL), 1.0),
        'norm_mix': 1.0 + nrm(ks[6], (DEPTH, D_MODEL), 0.1),
        'w_in_ab': nrm(ks[7], (N_EVEN, D_MODEL, 3 * D_CONV + 3 * D_ATT), D_MODEL ** -0.5),
        'conv_w': nrm(ks[8], (N_EVEN, CONV_W, D_CONV), 0.5),
        'w_out_ab': nrm(ks[9], (N_EVEN, D_CONV + D_ATT, D_MODEL), (D_CONV + D_ATT) ** -0.5),
        'pool_w': nrm(ks[10], (N_ODD, len(POOL_WINDOWS), POOL_GROUP, POOL_GROUP), POOL_GROUP ** -0.5),
        'pool_b': nrm(ks[11], (N_ODD, len(POOL_WINDOWS), POOL_GROUP), 0.01),
        'pool_scale': 1.0 + nrm(ks[12], (N_ODD, D_MODEL), 0.1),
        'norm_ffn': 1.0 + nrm(ks[13], (DEPTH, D_MODEL), 0.1),
        'peer_wq': nrm(ks[14], (DEPTH, D_MODEL, PEER_HEADS * PEER_QDIM), D_MODEL ** -0.5),
        'peer_subkeys': nrm(ks[15], (DEPTH, PEER_HEADS, 2, N_KEYS, PEER_HALF), PEER_HALF ** -0.5),
        'peer_u': nrm(ks[16], (DEPTH, N_EXPERTS, D_MODEL), D_MODEL ** -0.5),
        'peer_v': nrm(ks[17], (DEPTH, N_EXPERTS, D_MODEL), 0.2),
        'norm_final': 1.0 + nrm(ks[18], (D_MODEL,), 0.1),
    }


def reference(x_prompt, x_sample, state_conv, state_win_k, state_win_v, state_pool,
              norm_mix, w_in_ab, conv_w, w_out_ab, pool_w, pool_b, pool_scale,
              norm_ffn, peer_wq, peer_subkeys, peer_u, peer_v, norm_final):
    xp, xs = x_prompt, x_sample
    conv_p, conv_s, wk_p, wk_s, wv_p, wv_s, pool_p, pool_s = [], [], [], [], [], [], [], []
    for l in range(DEPTH):
        hp = rms_norm(xp, norm_mix[l])
        hs = rms_norm(xs, norm_mix[l])
        if l % 2 == 0:
            e = l // 2
            op, cp, kp, vp = even_mixer_prompt(hp, w_in_ab[e], conv_w[e], w_out_ab[e])
            osm, cs, ksm, vsm = even_mixer_sample(hs, state_conv[e], state_win_k[e], state_win_v[e],
                                                  w_in_ab[e], conv_w[e], w_out_ab[e])
            conv_p.append(cp); conv_s.append(cs)
            wk_p.append(kp); wk_s.append(ksm)
            wv_p.append(vp); wv_s.append(vsm)
        else:
            o = l // 2
            op = pool_mix(hp, pool_w[o], pool_b[o], pool_scale[o], hp.shape[1])
            pool_p.append(hp[:, -POOL_STATE:])
            hcat = jnp.concatenate([state_pool[o].astype(hs.dtype), hs], axis=1)
            osm = pool_mix(hcat, pool_w[o], pool_b[o], pool_scale[o], hs.shape[1])
            pool_s.append(hcat[:, -POOL_STATE:])
        xp = xp + op
        xs = xs + osm
        xp = xp + peer_ffn(rms_norm(xp, norm_ffn[l]), peer_wq[l], peer_subkeys[l], peer_u[l], peer_v[l])
        xs = xs + peer_ffn(rms_norm(xs, norm_ffn[l]), peer_wq[l], peer_subkeys[l], peer_u[l], peer_v[l])
    y_prompt = rms_norm(xp, norm_final)
    y_sample = rms_norm(xs, norm_final)
    return (y_prompt, y_sample, jnp.stack(conv_p), jnp.stack(conv_s), jnp.stack(wk_p), jnp.stack(wk_s),
            jnp.stack(wv_p), jnp.stack(wv_s), jnp.stack(pool_p), jnp.stack(pool_s))
```

```python
import functools
import math

import jax
import jax.numpy as jnp
from jax import lax
from jax.experimental import pallas as pl
from jax.experimental.pallas import tpu as pltpu

F32 = jnp.float32
BF16 = jnp.bfloat16

NORM_EPS = 1e-6
NEG_INF = -1e30
DILATED_PAIRS = ((128, 1), (512, 4), (2048, 16))
POOL_WINDOWS = (2, 4, 8, 16)
PEER_TOPK = 16
N_KEYS = 128

LANES = 128
SUBLANES = 8
VMEM_LIMIT = 56 * 1024 * 1024

ATT_BLOCK = 128
ROW_BLOCK = 512
SEL_BLOCK = 128
DENSE_TOKENS = 256
DENSE_EXPERTS = SUBLANES * N_KEYS


def _cparams(sem):
    return pltpu.CompilerParams(dimension_semantics=sem, vmem_limit_bytes=VMEM_LIMIT)


def _rms(x, g):
    r = lax.rsqrt(jnp.mean(x * x, axis=-1, keepdims=True) + NORM_EPS)
    return (x * r) * g


def _bdot(a, b):
    return jnp.dot(a.astype(BF16), b.astype(BF16), preferred_element_type=F32)


def _bdot_nt(a, b):
    return lax.dot_general(a.astype(BF16), b.astype(BF16), (((1,), (1,)), ((), ())),
                           preferred_element_type=F32)


def _even_in_body(x, g, w, cw, u1_fn, d_conv, d_att, q_scale):
    h = _rms(x, g)
    p = _bdot(h, w)
    gate_b = p[:, 0:d_conv]
    gate_c = p[:, d_conv:2 * d_conv]
    xv = p[:, 2 * d_conv:3 * d_conv]
    o = 3 * d_conv
    q = p[:, o:o + d_att] * q_scale
    k = p[:, o + d_att:o + 2 * d_att]
    v = p[:, o + 2 * d_att:o + 3 * d_att]
    u = gate_c * xv
    u1, u2 = u1_fn(u)
    y = cw[0:1] * u2 + cw[1:2] * u1 + cw[2:3] * u
    return gate_b * y, q, k, v, u


def _even_in_seq_kernel(x_ref, g_ref, w_ref, cw_ref, ya_ref, q_ref, k_ref, v_ref, ul_ref, carry_ref,
                        *, d_conv, d_att, q_scale):
    s = pl.program_id(1)

    @pl.when(s == 0)
    def _():
        carry_ref[...] = jnp.zeros_like(carry_ref)

    prev = carry_ref[...]
    tm = x_ref.shape[0]

    def shifted(u):
        rows = lax.broadcasted_iota(jnp.int32, u.shape, 0)
        u1 = jnp.where(rows == 0, prev[7:8], pltpu.roll(u, 1, 0))
        u2 = pltpu.roll(u, 2, 0)
        u2 = jnp.where(rows == 0, prev[6:7], jnp.where(rows == 1, prev[7:8], u2))
        return u1, u2

    ya, q, k, v, u = _even_in_body(x_ref[...], g_ref[...], w_ref[...], cw_ref[...], shifted,
                                   d_conv, d_att, q_scale)
    ya_ref[...] = ya
    q_ref[...] = q
    k_ref[...] = k
    v_ref[...] = v
    last = u[tm - SUBLANES:tm]
    carry_ref[...] = last
    ul_ref[...] = last


def _even_in_rows_kernel(x_ref, g_ref, w_ref, cw_ref, u1_ref, u2_ref, ya_ref, q_ref, k_ref, v_ref, u_ref,
                         *, d_conv, d_att, q_scale):
    ya, q, k, v, u = _even_in_body(x_ref[...], g_ref[...], w_ref[...], cw_ref[...],
                                   lambda _: (u1_ref[...], u2_ref[...]), d_conv, d_att, q_scale)
    ya_ref[...] = ya
    q_ref[...] = q
    k_ref[...] = k
    v_ref[...] = v
    u_ref[...] = u


def _even_in_prompt(x, g, w_bf, cw, d_conv, d_att, q_scale):
    b, s, d = x.shape
    tm = min(ROW_BLOCK, s)
    assert s % tm == 0 and tm % SUBLANES == 0
    ncol = w_bf.shape[1]
    row = lambda c: pl.BlockSpec((None, tm, c), lambda i, j: (i, j, 0))
    full = lambda shp: pl.BlockSpec(shp, lambda i, j: (0,) * len(shp))
    outs = pl.pallas_call(
        functools.partial(_even_in_seq_kernel, d_conv=d_conv, d_att=d_att, q_scale=q_scale),
        grid=(b, s // tm),
        in_specs=[row(d), full((1, d)), full((d, ncol)), full((cw.shape[0], d_conv))],
        out_specs=[row(d_conv), row(d_att), row(d_att), row(d_att),
                   pl.BlockSpec((None, SUBLANES, d_conv), lambda i, j: (i, 0, 0))],
        out_shape=[jax.ShapeDtypeStruct((b, s, d_conv), F32)] + [jax.ShapeDtypeStruct((b, s, d_att), F32)] * 3
                  + [jax.ShapeDtypeStruct((b, SUBLANES, d_conv), F32)],
        scratch_shapes=[pltpu.VMEM((SUBLANES, d_conv), F32)],
        compiler_params=_cparams(("arbitrary", "arbitrary")),
        name="even_in_prompt",
    )(x, g, w_bf, cw)
    return outs


def _even_in_sample(x, g, w_bf, cw, u1, u2, d_conv, d_att, q_scale):
    n, d = x.shape
    ncol = w_bf.shape[1]
    full = lambda shp: pl.BlockSpec(shp, lambda i: (0,) * len(shp))
    return pl.pallas_call(
        functools.partial(_even_in_rows_kernel, d_conv=d_conv, d_att=d_att, q_scale=q_scale),
        grid=(1,),
        in_specs=[full((n, d)), full((1, d)), full((d, ncol)), full((cw.shape[0], d_conv)),
                  full((n, d_conv)), full((n, d_conv))],
        out_specs=[full((n, d_conv)), full((n, d_att)), full((n, d_att)), full((n, d_att)), full((n, d_conv))],
        out_shape=[jax.ShapeDtypeStruct((n, d_conv), F32)] + [jax.ShapeDtypeStruct((n, d_att), F32)] * 3
                  + [jax.ShapeDtypeStruct((n, d_conv), F32)],
        compiler_params=_cparams(("arbitrary",)),
        name="even_in_sample",
    )(x, g, w_bf, cw, u1, u2)


def _attn_branch_kernel(q_ref, kp_ref, kc_ref, vp_ref, vc_ref, o_ref, l_ref, *, n_heads, hd, n_back):
    n = pl.program_id(2)
    blk = q_ref.shape[0]
    q = q_ref[...]
    k = jnp.concatenate([kp_ref[...], kc_ref[...]], axis=0)
    v = jnp.concatenate([vp_ref[...], vc_ref[...]], axis=0)
    qi = lax.broadcasted_iota(jnp.int32, (blk, 2 * blk), 0)
    ki = lax.broadcasted_iota(jnp.int32, (blk, 2 * blk), 1)
    dist = qi + blk - ki
    has_prev = jnp.where(n > 0, 0, blk)
    mask = (dist >= 0) & (dist <= n_back) & (ki >= has_prev)
    o_parts, l_parts = [], []
    for h in range(n_heads):
        sl = slice(h * hd, (h + 1) * hd)
        s = _bdot_nt(q[:, sl], k[:, sl])
        s = jnp.where(mask, s, NEG_INF)
        m = jnp.max(s, axis=-1, keepdims=True)
        p = jnp.exp(s - m)
        den = jnp.sum(p, axis=-1, keepdims=True)
        o_parts.append(_bdot(p, v[:, sl]) / den)
        l_parts.append(jnp.broadcast_to(m + jnp.log(den), (blk, hd)))
    o_ref[...] = jnp.concatenate(o_parts, axis=1)
    l_ref[...] = jnp.concatenate(l_parts, axis=1)


def _attn_branch(q, k, v, window, dil, n_heads):
    b, s, da = q.shape
    hd = da // n_heads
    n_back = window // dil
    assert s % (dil * ATT_BLOCK) == 0 and n_back <= ATT_BLOCK
    L = s // dil
    nb = L // ATT_BLOCK
    view = lambda t: t.reshape(b, L, dil * da)
    cur = pl.BlockSpec((None, ATT_BLOCK, da), lambda i, r, n: (i, n, r))
    prev = pl.BlockSpec((None, ATT_BLOCK, da), lambda i, r, n: (i, jnp.maximum(n - 1, 0), r))
    o, l = pl.pallas_call(
        functools.partial(_attn_branch_kernel, n_heads=n_heads, hd=hd, n_back=n_back),
        grid=(b, dil, nb),
        in_specs=[cur, prev, cur, prev, cur],
        out_specs=[cur, cur],
        out_shape=[jax.ShapeDtypeStruct((b, L, dil * da), F32)] * 2,
        compiler_params=_cparams(("arbitrary", "arbitrary", "arbitrary")),
        name=f"attn_branch_d{dil}",
    )(view(q), view(k), view(k), view(v), view(v))
    return o.reshape(b, s, da), l.reshape(b, s, da)


def _attn_sample_kernel(q_ref, kn_ref, vn_ref, *refs, n_heads, hd):
    n_br = (len(refs) - 1) // 2
    k_refs, v_refs, y_ref = refs[:n_br], refs[n_br:2 * n_br], refs[-1]
    q = q_ref[...]
    kn = kn_ref[...]
    vn = vn_ref[...]
    o_rows, l_rows = [], []
    for g in range(n_br):
        kq = k_refs[g][...] * q
        vv = v_refs[g][...]
        sq = kn * q
        o_parts, l_parts = [], []
        for h in range(n_heads):
            sl = slice(h * hd, (h + 1) * hd)
            s = jnp.sum(kq[:, sl], axis=-1, keepdims=True)
            s0 = jnp.sum(sq[:, sl], axis=-1, keepdims=True)
            m = jnp.maximum(jnp.max(s, axis=0, keepdims=True), s0)
            p = jnp.exp(s - m)
            p0 = jnp.exp(s0 - m)
            den = jnp.sum(p, axis=0, keepdims=True) + p0
            o = (jnp.sum(p * vv[:, sl], axis=0, keepdims=True) + p0 * vn[:, sl]) / den
            o_parts.append(o)
            l_parts.append(jnp.broadcast_to(m + jnp.log(den), (1, hd)))
        o_rows.append(jnp.concatenate(o_parts, axis=1))
        l_rows.append(jnp.concatenate(l_parts, axis=1))
    y_ref[...] = _merge_branches(o_rows, l_rows)


def _merge_branches(o_list, l_list):
    m = functools.reduce(jnp.maximum, l_list)
    e = [jnp.exp(l - m) for l in l_list]
    num = functools.reduce(lambda a, b: a + b, [ei * oi for ei, oi in zip(e, o_list)])
    return num / functools.reduce(lambda a, b: a + b, e)


def _attn_sample(q, k_new, v_new, k_state, v_state, n_heads):
    b, da = q.shape
    wb = k_state.shape[1]
    hd = da // n_heads
    row = pl.BlockSpec((None, 1, da), lambda i: (i, 0, 0))
    ins, specs = [], []
    for st in (k_state, v_state):
        for window, dil in DILATED_PAIRS:
            n_back = window // dil
            assert n_back * dil <= wb and wb % dil == 0 and (wb // dil) % n_back == 0
            L = wb // dil
            ins.append(st.reshape(b, L, dil * da))
            specs.append(pl.BlockSpec((None, n_back, da), lambda i, L=L, nbk=n_back: (i, L // nbk - 1, 0)))
    y = pl.pallas_call(
        functools.partial(_attn_sample_kernel, n_heads=n_heads, hd=hd),
        grid=(b,),
        in_specs=[row, row, row] + specs,
        out_specs=row,
        out_shape=jax.ShapeDtypeStruct((b, 1, da), F32),
        compiler_params=_cparams(("arbitrary",)),
        name="attn_sample",
    )(q[:, None], k_new[:, None], v_new[:, None], *ins)
    return y[:, 0]


def _out_proj_kernel(x_ref, ya_ref, *refs):
    wa_ref, wb_ref, xo_ref = refs[-3:]
    br = refs[:-3]
    if len(br) == 1:
        yb = br[0][...]
    else:
        n_br = len(br) // 2
        yb = _merge_branches([r[...] for r in br[:n_br]], [r[...] for r in br[n_br:]])
    xo_ref[...] = x_ref[...] + _bdot(ya_ref[...], wa_ref[...]) + _bdot(yb, wb_ref[...])


def _out_proj(x, ya, branch_arrays, wa_bf, wb_bf):
    n, d = x.shape
    tm = min(ROW_BLOCK, n)
    assert n % tm == 0
    row = lambda c: pl.BlockSpec((tm, c), lambda i: (i, 0))
    full = lambda shp: pl.BlockSpec(shp, lambda i: (0,) * len(shp))
    return pl.pallas_call(
        _out_proj_kernel,
        grid=(n // tm,),
        in_specs=[row(d), row(ya.shape[1])] + [row(a.shape[1]) for a in branch_arrays]
                 + [full(wa_bf.shape), full(wb_bf.shape)],
        out_specs=row(d),
        out_shape=jax.ShapeDtypeStruct((n, d), F32),
        compiler_params=_cparams(("arbitrary",)),
        name="out_proj",
    )(x, ya, *branch_arrays, wa_bf, wb_bf)


def _pool_groups(dmat, w_ref, b_ref, sc):
    n_g = w_ref.shape[0]
    gw = w_ref.shape[1]
    ys = [_bdot(dmat[g], w_ref[g]) + b_ref[g:g + 1] for g in range(n_g)]
    return jnp.concatenate(ys, axis=1) * sc


def _pool_prompt_kernel(x_ref, g_ref, w_ref, b_ref, sc_ref, xo_ref, hl_ref, carry_ref):
    s = pl.program_id(1)
    hist = carry_ref.shape[0]

    @pl.when(s == 0)
    def _():
        carry_ref[...] = jnp.zeros_like(carry_ref)

    x = x_ref[...]
    tm = x.shape[0]
    h = _rms(x, g_ref[...])
    ext = jnp.concatenate([carry_ref[...], h], axis=0)
    pos = s * tm + lax.broadcasted_iota(jnp.int32, (tm, 1), 0) + 1
    gw = w_ref.shape[1]
    acc = ext
    width = 1
    diffs = []
    for g, w in enumerate(POOL_WINDOWS):
        while width < w:
            acc = acc + pltpu.roll(acc, width, 0)
            width *= 2
        cols = slice(g * gw, (g + 1) * gw)
        win = acc[hist:, cols]
        div = jnp.minimum(pos, w).astype(F32)
        diffs.append(win / div - h[:, cols])
    xo_ref[...] = x + _pool_groups(diffs, w_ref, b_ref, sc_ref[...])
    last = ext[tm:tm + hist]
    carry_ref[...] = last
    hl_ref[...] = last


def _pool_prompt(x, g, w_bf, bias, scale):
    b, s, d = x.shape
    tm = min(ROW_BLOCK, s)
    hist = 16
    assert s % tm == 0 and tm >= hist and max(POOL_WINDOWS) <= hist
    assert all(w == 2 ** (i + 1) for i, w in enumerate(POOL_WINDOWS))
    row = pl.BlockSpec((None, tm, d), lambda i, j: (i, j, 0))
    full = lambda shp: pl.BlockSpec(shp, lambda i, j: (0,) * len(shp))
    return pl.pallas_call(
        _pool_prompt_kernel,
        grid=(b, s // tm),
        in_specs=[row, full((1, d)), full(w_bf.shape), full(bias.shape), full((1, d))],
        out_specs=[row, pl.BlockSpec((None, hist, d), lambda i, j: (i, 0, 0))],
        out_shape=[jax.ShapeDtypeStruct((b, s, d), F32), jax.ShapeDtypeStruct((b, hist, d), F32)],
        scratch_shapes=[pltpu.VMEM((hist, d), F32)],
        compiler_params=_cparams(("arbitrary", "arbitrary")),
        name="pool_prompt",
    )(x, g, w_bf, bias, scale)


def _pool_sample_kernel(x_ref, st_ref, g_ref, w_ref, b_ref, sc_ref, xo_ref, h_ref):
    x = x_ref[...]
    h = _rms(x, g_ref[...])
    n_st = st_ref.shape[0]
    gw = w_ref.shape[1]
    diffs = []
    for g, w in enumerate(POOL_WINDOWS):
        cols = slice(g * gw, (g + 1) * gw)
        tot = h[:, cols]
        for j in range(1, w):
            tot = tot + st_ref[n_st - j][:, cols]
        diffs.append(tot / float(w) - h[:, cols])
    xo_ref[...] = x + _pool_groups(diffs, w_ref, b_ref, sc_ref[...])
    h_ref[...] = h


def _pool_sample(x, state_t, g, w_bf, bias, scale):
    n, d = x.shape
    assert state_t.shape[0] + 1 >= max(POOL_WINDOWS)
    full = lambda shp: pl.BlockSpec(shp, lambda i: (0,) * len(shp))
    return pl.pallas_call(
        _pool_sample_kernel,
        grid=(1,),
        in_specs=[full(x.shape), full(state_t.shape), full((1, d)), full(w_bf.shape), full(bias.shape),
                  full((1, d))],
        out_specs=[full(x.shape), full(x.shape)],
        out_shape=[jax.ShapeDtypeStruct((n, d), F32)] * 2,
        compiler_params=_cparams(("arbitrary",)),
        name="pool_sample",
    )(x, state_t, g, w_bf, bias, scale)


def _top16_ranked(s):
    n, t = s.shape
    rows = lax.broadcasted_iota(jnp.int32, (n, t), 0).astype(F32)
    rows16 = lax.broadcasted_iota(jnp.int32, (PEER_TOPK, t), 0)

    def step(it, carry):
        s, rank, sv = carry
        m = jnp.max(s, axis=0, keepdims=True)
        first = jnp.min(jnp.where(s == m, rows, float(n)), axis=0, keepdims=True)
        sel = rows == first
        itf = it.astype(F32)
        return (jnp.where(sel, -jnp.inf, s), jnp.where(sel, itf, rank), jnp.where(rows16 == it, m, sv))

    init = (s, jnp.full((n, t), float(PEER_TOPK), F32), jnp.zeros((PEER_TOPK, t), F32))
    _, rank, sv = lax.fori_loop(0, PEER_TOPK, step, init)
    return rank, sv


def _pair_counts(sv1, sv2):
    t = sv1.shape[1]
    half = SUBLANES
    tiles = [sv1[0:1] + sv2[0:half], sv1[0:1] + sv2[half:2 * half]]
    tiles += [sv1[r:r + 1] + sv2[0:half] for r in range(1, half)]
    tiles += [sv1[half:2 * half] + sv2[0:1]]
    cand0 = jnp.concatenate(tiles, axis=0)
    nrow = cand0.shape[0]
    i = lax.broadcasted_iota(jnp.int32, (nrow, t), 0)
    j = i - 2 * half
    mid = (lax.shift_right_arithmetic(j, 3) + 1) * PEER_TOPK + lax.bitwise_and(j, half - 1)
    idx = jnp.where(i < 2 * half, i,
                    jnp.where(i < nrow - half, mid, (i - (nrow - half) + half) * PEER_TOPK)).astype(F32)

    def step(_, carry):
        cand, chosen = carry
        m = jnp.max(cand, axis=0, keepdims=True)
        first = jnp.min(jnp.where(cand == m, idx, 1e9), axis=0, keepdims=True)
        sel = idx == first
        return jnp.where(sel, -jnp.inf, cand), jnp.where(sel, 1.0, chosen)

    _, chosen = lax.fori_loop(0, PEER_TOPK, step, (cand0, jnp.zeros((nrow, t), F32)))
    den = jnp.sum(jnp.where(chosen > 0, jnp.exp(cand0 - cand0[0:1]), 0.0), axis=0, keepdims=True)
    lam = [jnp.sum(chosen[0:2 * half], axis=0, keepdims=True)]
    lam += [jnp.sum(chosen[2 * half + half * (r - 1):2 * half + half * r], axis=0, keepdims=True)
            for r in range(1, half)]
    lam += [chosen[nrow - half + r:nrow - half + r + 1] for r in range(half)]
    return lam, den


def _peer_select_kernel(x_ref, g_ref, wqt_ref, sk_ref, hb_ref, lam1_ref, rank2_ref, p1_ref, p2_ref, s_ref,
                        *, n_heads):
    h = _rms(x_ref[...], g_ref[...])
    hb = h.astype(BF16)
    hb_ref[...] = hb
    nk = sk_ref.shape[1]
    qt = lax.dot_general(wqt_ref[...], hb, (((1,), (1,)), ((), ())), preferred_element_type=F32)
    for c in range(2 * n_heads):
        s_ref[c] = jnp.dot(sk_ref[c], qt[c * nk:(c + 1) * nk], preferred_element_type=F32,
                           precision=lax.Precision.HIGHEST)
    for hh in range(n_heads):
        s1 = s_ref[2 * hh]
        s2 = s_ref[2 * hh + 1]
        rank1, sv1 = _top16_ranked(s1)
        rank2, sv2 = _top16_ranked(s2)
        lam, den = _pair_counts(sv1, sv2)
        lam1 = jnp.zeros_like(rank1)
        for r in range(PEER_TOPK):
            lam1 = jnp.where(rank1 == float(r), lam[r], lam1)
        lam1_ref[hh] = lam1
        rank2_ref[hh] = rank2
        p1_ref[hh] = jnp.exp(s1 - sv1[0:1]) / den
        p2_ref[hh] = jnp.exp(s2 - sv2[0:1])


def _peer_select(x, g, wqt_bf, sk):
    n, d = x.shape
    n_heads = sk.shape[0] // 2
    nk = sk.shape[1]
    t = SEL_BLOCK
    assert n % t == 0 and nk == N_KEYS
    full = lambda shp: pl.BlockSpec(shp, lambda i: (0,) * len(shp))
    sel = pl.BlockSpec((n_heads, nk, t), lambda i: (0, 0, i))
    sel_shape = jax.ShapeDtypeStruct((n_heads, nk, n), F32)
    return pl.pallas_call(
        functools.partial(_peer_select_kernel, n_heads=n_heads),
        grid=(n // t,),
        in_specs=[pl.BlockSpec((t, d), lambda i: (i, 0)), full((1, d)), full(wqt_bf.shape), full(sk.shape)],
        out_specs=[pl.BlockSpec((t, d), lambda i: (i, 0)), sel, sel, sel, sel],
        out_shape=[jax.ShapeDtypeStruct((n, d), BF16), sel_shape, sel_shape, sel_shape, sel_shape],
        scratch_shapes=[pltpu.VMEM((2 * n_heads, nk, t), F32)],
        compiler_params=_cparams(("arbitrary",)),
        name="peer_select",
    )(x, g, wqt_bf, sk)


def _gelu(a):
    return 0.5 * a * (1.0 + lax.erf(a * (1.0 / math.sqrt(2.0))))


def _peer_dense_kernel(x_ref, hb_ref, u_ref, vt_ref, lam1_ref, rank2_ref, p1_ref, p2_ref, xo_ref, acc_ref, z_ref,
                       *, n_heads):
    e = pl.program_id(1)
    ne = pl.num_programs(1)
    te = u_ref.shape[0]
    t = hb_ref.shape[0]
    nk = rank2_ref.shape[1]

    @pl.when(e == 0)
    def _():
        acc_ref[...] = jnp.zeros_like(acc_ref)

    a = lax.dot_general(u_ref[...], hb_ref[...], (((1,), (1,)), ((), ())), preferred_element_type=F32)
    grp = pl.ds(pl.multiple_of(e * SUBLANES, SUBLANES), SUBLANES)
    for tc in range(t // LANES):
        lanes = slice(tc * LANES, (tc + 1) * LANES)
        l1g = [lam1_ref[hh, grp, lanes] for hh in range(n_heads)]
        p1g = [p1_ref[hh, grp, lanes] for hh in range(n_heads)]
        for ii in range(SUBLANES):
            gate = jnp.zeros((nk, LANES), F32)
            for hh in range(n_heads):
                gate = gate + jnp.where(l1g[hh][ii:ii + 1] > rank2_ref[hh, :, lanes],
                                        p1g[hh][ii:ii + 1] * p2_ref[hh, :, lanes], 0.0)
            z_ref[ii * nk:(ii + 1) * nk, lanes] = (gate * _gelu(a[ii * nk:(ii + 1) * nk, lanes])).astype(BF16)
    acc_ref[...] += jnp.dot(vt_ref[...], z_ref[...], preferred_element_type=F32)

    @pl.when(e == ne - 1)
    def _():
        xo_ref[...] = x_ref[...] + acc_ref[...].T


def _peer_dense(x, hb, u_bf, vt_bf, lam1, rank2, p1, p2):
    n, d = x.shape
    n_exp = u_bf.shape[0]
    n_heads, nk, _ = lam1.shape
    t = min(DENSE_TOKENS, n)
    te = DENSE_EXPERTS
    assert n % t == 0 and n_exp % te == 0 and te == SUBLANES * nk and n_exp == nk * nk and t % LANES == 0
    tok = pl.BlockSpec((t, d), lambda i, e: (i, 0))
    sel = pl.BlockSpec((n_heads, nk, t), lambda i, e: (0, 0, i))
    return pl.pallas_call(
        functools.partial(_peer_dense_kernel, n_heads=n_heads),
        grid=(n // t, n_exp // te),
        in_specs=[tok, tok, pl.BlockSpec((te, d), lambda i, e: (e, 0)), pl.BlockSpec((d, te), lambda i, e: (0, e)),
                  sel, sel, sel, sel],
        out_specs=tok,
        out_shape=jax.ShapeDtypeStruct((n, d), F32),
        scratch_shapes=[pltpu.VMEM((d, t), F32), pltpu.VMEM((te, t), BF16)],
        compiler_params=_cparams(("arbitrary", "arbitrary")),
        name="peer_dense",
    )(x, hb, u_bf, vt_bf, lam1, rank2, p1, p2)


def _peer(x, g, wqt_bf, sk, u_bf, vt_bf):
    n = x.shape[0]
    pad = (-n) % max(SEL_BLOCK, LANES)
    xp = jnp.pad(x, ((0, pad), (0, 0))) if pad else x
    hb, lam1, rank2, p1, p2 = _peer_select(xp, g, wqt_bf, sk)
    out = _peer_dense(xp, hb, u_bf, vt_bf, lam1, rank2, p1, p2)
    return out[:n] if pad else out


def _final_norm_kernel(x_ref, g_ref, o_ref):
    o_ref[...] = _rms(x_ref[...], g_ref[...])


def _final_norm(x, g):
    n, d = x.shape
    tm = min(ROW_BLOCK, n)
    assert n % tm == 0
    return pl.pallas_call(
        _final_norm_kernel,
        grid=(n // tm,),
        in_specs=[pl.BlockSpec((tm, d), lambda i: (i, 0)), pl.BlockSpec((1, d), lambda i: (0, 0))],
        out_specs=pl.BlockSpec((tm, d), lambda i: (i, 0)),
        out_shape=jax.ShapeDtypeStruct((n, d), F32),
        compiler_params=_cparams(("arbitrary",)),
        name="final_norm",
    )(x, g)


def kernel(x_prompt, x_sample, state_conv, state_win_k, state_win_v, state_pool, norm_mix, w_in_ab, conv_w,
           w_out_ab, pool_w, pool_b, pool_scale, norm_ffn, peer_wq, peer_subkeys, peer_u, peer_v, norm_final):
    b, s, d = x_prompt.shape
    bs, ts, _ = x_sample.shape
    assert ts == 1
    depth = norm_mix.shape[0]
    d_conv = conv_w.shape[2]
    n_heads_b, hd = state_win_k.shape[3], state_win_k.shape[4]
    d_att = n_heads_b * hd
    wb_s = state_win_k.shape[2]
    wb_p = min(max(w for w, _ in DILATED_PAIRS), s)
    pool_state = state_pool.shape[2]
    q_scale = float(hd) ** -0.5
    n_peer_heads = peer_subkeys.shape[1]

    xp = x_prompt.reshape(b * s, d)
    xs = x_sample.reshape(bs, d)
    conv_p, conv_s, wk_p, wk_s, wv_p, wv_s, pool_p, pool_s = [], [], [], [], [], [], [], []
    for l in range(depth):
        g_mix = norm_mix[l][None]
        if l % 2 == 0:
            e = l // 2
            w_in = w_in_ab[e].astype(BF16)
            wa = w_out_ab[e, :d_conv].astype(BF16)
            wbm = w_out_ab[e, d_conv:].astype(BF16)
            ya, q, k, v, ul = _even_in_prompt(xp.reshape(b, s, d), g_mix, w_in, conv_w[e], d_conv, d_att, q_scale)
            branches = [_attn_branch(q, k, v, w, dl, n_heads_b) for w, dl in DILATED_PAIRS]
            o_list = [br[0].reshape(b * s, d_att) for br in branches]
            l_list = [br[1].reshape(b * s, d_att) for br in branches]
            xp = _out_proj(xp, ya.reshape(b * s, d_conv), o_list + l_list, wa, wbm)
            conv_p.append(ul[:, SUBLANES - 2:])
            wk_p.append(k[:, s - wb_p:].reshape(b, wb_p, n_heads_b, hd))
            wv_p.append(v[:, s - wb_p:].reshape(b, wb_p, n_heads_b, hd))
            cst = state_conv[e]
            ya_s, q_s, k_s, v_s, u_s = _even_in_sample(xs, g_mix, w_in, conv_w[e], cst[:, 1], cst[:, 0],
                                                       d_conv, d_att, q_scale)
            kst = state_win_k[e].reshape(bs, wb_s, d_att)
            vst = state_win_v[e].reshape(bs, wb_s, d_att)
            yb_s = _attn_sample(q_s, k_s, v_s, kst, vst, n_heads_b)
            xs = _out_proj(xs, ya_s, [yb_s], wa, wbm)
            conv_s.append(jnp.stack([cst[:, 1], u_s], axis=1))
            wk_s.append(jnp.concatenate([kst[:, 1:], k_s[:, None]], axis=1).reshape(bs, wb_s, n_heads_b, hd))
            wv_s.append(jnp.concatenate([vst[:, 1:], v_s[:, None]], axis=1).reshape(bs, wb_s, n_heads_b, hd))
        else:
            o = l // 2
            pw = pool_w[o].astype(BF16)
            xp3, hl = _pool_prompt(xp.reshape(b, s, d), g_mix, pw, pool_b[o], pool_scale[o][None])
            xp = xp3.reshape(b * s, d)
            pool_p.append(hl[:, hl.shape[1] - pool_state:])
            st = state_pool[o]
            xs, h_s = _pool_sample(xs, jnp.swapaxes(st, 0, 1), g_mix, pw, pool_b[o], pool_scale[o][None])
            pool_s.append(jnp.concatenate([st[:, 1:], h_s[:, None]], axis=1))
        g_ffn = norm_ffn[l][None]
        wqt = peer_wq[l].T.astype(BF16)
        sk = peer_subkeys[l].reshape(2 * n_peer_heads, N_KEYS, -1)
        u_bf = peer_u[l].astype(BF16)
        vt_bf = peer_v[l].T.astype(BF16)
        xp = _peer(xp, g_ffn, wqt, sk, u_bf, vt_bf)
        xs = _peer(xs, g_ffn, wqt, sk, u_bf, vt_bf)
    gf = norm_final[None]
    y_prompt = _final_norm(xp, gf).reshape(b, s, d)
    y_sample = _final_norm(xs, gf).reshape(bs, ts, d)
    return (y_prompt, y_sample, jnp.stack(conv_p), jnp.stack(conv_s), jnp.stack(wk_p), jnp.stack(wk_s),
            jnp.stack(wv_p), jnp.stack(wv_s), jnp.stack(pool_p), jnp.stack(pool_s))
```

```python
import functools
import math

import jax
import jax.numpy as jnp
from jax import lax
from jax.experimental import pallas as pl
from jax.experimental.pallas import tpu as pltpu

F32 = jnp.float32
BF16 = jnp.bfloat16

NORM_EPS = 1e-6
NEG_INF = -1e30
DILATED_PAIRS = ((128, 1), (512, 4), (2048, 16))
POOL_WINDOWS = (2, 4, 8, 16)
PEER_TOPK = 16
N_KEYS = 128

LANES = 128
SUBLANES = 8
VMEM_LIMIT = 56 * 1024 * 1024

ATT_BLOCK = 128
ROW_BLOCK = 512
SEL_BLOCK = 128
DENSE_TOKENS = 256
MXU_CHUNKS = 4
DENSE_EXPERTS = SUBLANES * N_KEYS


def _cparams(sem, flags=None):
    return pltpu.CompilerParams(dimension_semantics=sem, vmem_limit_bytes=VMEM_LIMIT, flags=flags)


def _rms(x, g):
    r = lax.rsqrt(jnp.mean(x * x, axis=-1, keepdims=True) + NORM_EPS)
    return (x * r) * g


def _bdot(a, b):
    return jnp.dot(a.astype(BF16), b.astype(BF16), preferred_element_type=F32)


def _bdot_nt(a, b):
    return lax.dot_general(a.astype(BF16), b.astype(BF16), (((1,), (1,)), ((), ())),
                           preferred_element_type=F32)


def _even_in_body(x, g, w, cw, u1_fn, d_conv, d_att, q_scale):
    h = _rms(x, g)
    p = _bdot(h, w)
    gate_b = p[:, 0:d_conv]
    gate_c = p[:, d_conv:2 * d_conv]
    xv = p[:, 2 * d_conv:3 * d_conv]
    o = 3 * d_conv
    q = p[:, o:o + d_att] * q_scale
    k = p[:, o + d_att:o + 2 * d_att]
    v = p[:, o + 2 * d_att:o + 3 * d_att]
    u = gate_c * xv
    u1, u2 = u1_fn(u)
    y = cw[0:1] * u2 + cw[1:2] * u1 + cw[2:3] * u
    return gate_b * y, q, k, v, u


def _even_in_seq_kernel(x_ref, g_ref, w_ref, cw_ref, ya_ref, q_ref, k_ref, v_ref, ul_ref, carry_ref,
                        *, d_conv, d_att, q_scale):
    s = pl.program_id(1)

    @pl.when(s == 0)
    def _():
        carry_ref[...] = jnp.zeros_like(carry_ref)

    prev = carry_ref[...]
    tm = x_ref.shape[0]

    def shifted(u):
        rows = lax.broadcasted_iota(jnp.int32, u.shape, 0)
        u1 = jnp.where(rows == 0, prev[7:8], pltpu.roll(u, 1, 0))
        u2 = pltpu.roll(u, 2, 0)
        u2 = jnp.where(rows == 0, prev[6:7], jnp.where(rows == 1, prev[7:8], u2))
        return u1, u2

    ya, q, k, v, u = _even_in_body(x_ref[...], g_ref[...], w_ref[...], cw_ref[...], shifted,
                                   d_conv, d_att, q_scale)
    ya_ref[...] = ya
    q_ref[...] = q
    k_ref[...] = k
    v_ref[...] = v
    last = u[tm - SUBLANES:tm]
    carry_ref[...] = last
    ul_ref[...] = last


def _even_in_rows_kernel(x_ref, g_ref, w_ref, cw_ref, u1_ref, u2_ref, ya_ref, q_ref, k_ref, v_ref, u_ref,
                         *, d_conv, d_att, q_scale):
    ya, q, k, v, u = _even_in_body(x_ref[...], g_ref[...], w_ref[...], cw_ref[...],
                                   lambda _: (u1_ref[...], u2_ref[...]), d_conv, d_att, q_scale)
    ya_ref[...] = ya
    q_ref[...] = q
    k_ref[...] = k
    v_ref[...] = v
    u_ref[...] = u


def _even_in_prompt(x, g, w_bf, cw, d_conv, d_att, q_scale):
    b, s, d = x.shape
    tm = min(ROW_BLOCK, s)
    assert s % tm == 0 and tm % SUBLANES == 0
    ncol = w_bf.shape[1]
    row = lambda c: pl.BlockSpec((None, tm, c), lambda i, j: (i, j, 0))
    full = lambda shp: pl.BlockSpec(shp, lambda i, j: (0,) * len(shp))
    outs = pl.pallas_call(
        functools.partial(_even_in_seq_kernel, d_conv=d_conv, d_att=d_att, q_scale=q_scale),
        grid=(b, s // tm),
        in_specs=[row(d), full((1, d)), full((d, ncol)), full((cw.shape[0], d_conv))],
        out_specs=[row(d_conv), row(d_att), row(d_att), row(d_att),
                   pl.BlockSpec((None, SUBLANES, d_conv), lambda i, j: (i, 0, 0))],
        out_shape=[jax.ShapeDtypeStruct((b, s, d_conv), F32)] + [jax.ShapeDtypeStruct((b, s, d_att), F32)] * 3
                  + [jax.ShapeDtypeStruct((b, SUBLANES, d_conv), F32)],
        scratch_shapes=[pltpu.VMEM((SUBLANES, d_conv), F32)],
        compiler_params=_cparams(("arbitrary", "arbitrary")),
        name="even_in_prompt",
    )(x, g, w_bf, cw)
    return outs


def _even_in_sample(x, g, w_bf, cw, u1, u2, d_conv, d_att, q_scale):
    n, d = x.shape
    ncol = w_bf.shape[1]
    full = lambda shp: pl.BlockSpec(shp, lambda i: (0,) * len(shp))
    return pl.pallas_call(
        functools.partial(_even_in_rows_kernel, d_conv=d_conv, d_att=d_att, q_scale=q_scale),
        grid=(1,),
        in_specs=[full((n, d)), full((1, d)), full((d, ncol)), full((cw.shape[0], d_conv)),
                  full((n, d_conv)), full((n, d_conv))],
        out_specs=[full((n, d_conv)), full((n, d_att)), full((n, d_att)), full((n, d_att)), full((n, d_conv))],
        out_shape=[jax.ShapeDtypeStruct((n, d_conv), F32)] + [jax.ShapeDtypeStruct((n, d_att), F32)] * 3
                  + [jax.ShapeDtypeStruct((n, d_conv), F32)],
        compiler_params=_cparams(("arbitrary",)),
        name="even_in_sample",
    )(x, g, w_bf, cw, u1, u2)


def _attn_branch_kernel(q_ref, kp_ref, kc_ref, vp_ref, vc_ref, o_ref, l_ref, *, n_heads, hd, n_back):
    n = pl.program_id(2)
    blk = q_ref.shape[0]
    q = q_ref[...]
    k = jnp.concatenate([kp_ref[...], kc_ref[...]], axis=0)
    v = jnp.concatenate([vp_ref[...], vc_ref[...]], axis=0)
    qi = lax.broadcasted_iota(jnp.int32, (blk, 2 * blk), 0)
    ki = lax.broadcasted_iota(jnp.int32, (blk, 2 * blk), 1)
    dist = qi + blk - ki
    has_prev = jnp.where(n > 0, 0, blk)
    mask = (dist >= 0) & (dist <= n_back) & (ki >= has_prev)
    o_parts, l_parts = [], []
    for h in range(n_heads):
        sl = slice(h * hd, (h + 1) * hd)
        s = _bdot_nt(q[:, sl], k[:, sl])
        s = jnp.where(mask, s, NEG_INF)
        m = jnp.max(s, axis=-1, keepdims=True)
        p = jnp.exp(s - m)
        den = jnp.sum(p, axis=-1, keepdims=True)
        o_parts.append(_bdot(p, v[:, sl]) / den)
        l_parts.append(jnp.broadcast_to(m + jnp.log(den), (blk, hd)))
    o_ref[...] = jnp.concatenate(o_parts, axis=1)
    l_ref[...] = jnp.concatenate(l_parts, axis=1)


def _attn_branch(q, k, v, window, dil, n_heads):
    b, s, da = q.shape
    hd = da // n_heads
    n_back = window // dil
    assert s % (dil * ATT_BLOCK) == 0 and n_back <= ATT_BLOCK
    L = s // dil
    nb = L // ATT_BLOCK
    view = lambda t: t.reshape(b, L, dil * da)
    cur = pl.BlockSpec((None, ATT_BLOCK, da), lambda i, r, n: (i, n, r))
    prev = pl.BlockSpec((None, ATT_BLOCK, da), lambda i, r, n: (i, jnp.maximum(n - 1, 0), r))
    o, l = pl.pallas_call(
        functools.partial(_attn_branch_kernel, n_heads=n_heads, hd=hd, n_back=n_back),
        grid=(b, dil, nb),
        in_specs=[cur, prev, cur, prev, cur],
        out_specs=[cur, cur],
        out_shape=[jax.ShapeDtypeStruct((b, L, dil * da), F32)] * 2,
        compiler_params=_cparams(("arbitrary", "arbitrary", "arbitrary")),
        name=f"attn_branch_d{dil}",
    )(view(q), view(k), view(k), view(v), view(v))
    return o.reshape(b, s, da), l.reshape(b, s, da)


def _attn_sample_kernel(q_ref, kn_ref, vn_ref, *refs, n_heads, hd):
    n_br = (len(refs) - 1) // 2
    k_refs, v_refs, y_ref = refs[:n_br], refs[n_br:2 * n_br], refs[-1]
    q = q_ref[...]
    kn = kn_ref[...]
    vn = vn_ref[...]
    o_rows, l_rows = [], []
    for g in range(n_br):
        kq = k_refs[g][...] * q
        vv = v_refs[g][...]
        sq = kn * q
        o_parts, l_parts = [], []
        for h in range(n_heads):
            sl = slice(h * hd, (h + 1) * hd)
            s = jnp.sum(kq[:, sl], axis=-1, keepdims=True)
            s0 = jnp.sum(sq[:, sl], axis=-1, keepdims=True)
            m = jnp.maximum(jnp.max(s, axis=0, keepdims=True), s0)
            p = jnp.exp(s - m)
            p0 = jnp.exp(s0 - m)
            den = jnp.sum(p, axis=0, keepdims=True) + p0
            o = (jnp.sum(p * vv[:, sl], axis=0, keepdims=True) + p0 * vn[:, sl]) / den
            o_parts.append(o)
            l_parts.append(jnp.broadcast_to(m + jnp.log(den), (1, hd)))
        o_rows.append(jnp.concatenate(o_parts, axis=1))
        l_rows.append(jnp.concatenate(l_parts, axis=1))
    y_ref[...] = _merge_branches(o_rows, l_rows)


def _merge_branches(o_list, l_list):
    m = functools.reduce(jnp.maximum, l_list)
    e = [jnp.exp(l - m) for l in l_list]
    num = functools.reduce(lambda a, b: a + b, [ei * oi for ei, oi in zip(e, o_list)])
    return num / functools.reduce(lambda a, b: a + b, e)


def _attn_sample(q, k_new, v_new, k_state, v_state, n_heads):
    b, da = q.shape
    wb = k_state.shape[1]
    hd = da // n_heads
    row = pl.BlockSpec((None, 1, da), lambda i: (i, 0, 0))
    ins, specs = [], []
    for st in (k_state, v_state):
        for window, dil in DILATED_PAIRS:
            n_back = window // dil
            assert n_back * dil <= wb and wb % dil == 0 and (wb // dil) % n_back == 0
            L = wb // dil
            ins.append(st.reshape(b, L, dil * da))
            specs.append(pl.BlockSpec((None, n_back, da), lambda i, L=L, nbk=n_back: (i, L // nbk - 1, 0)))
    y = pl.pallas_call(
        functools.partial(_attn_sample_kernel, n_heads=n_heads, hd=hd),
        grid=(b,),
        in_specs=[row, row, row] + specs,
        out_specs=row,
        out_shape=jax.ShapeDtypeStruct((b, 1, da), F32),
        compiler_params=_cparams(("arbitrary",)),
        name="attn_sample",
    )(q[:, None], k_new[:, None], v_new[:, None], *ins)
    return y[:, 0]


def _out_proj_kernel(x_ref, ya_ref, *refs):
    wa_ref, wb_ref, xo_ref = refs[-3:]
    br = refs[:-3]
    if len(br) == 1:
        yb = br[0][...]
    else:
        n_br = len(br) // 2
        yb = _merge_branches([r[...] for r in br[:n_br]], [r[...] for r in br[n_br:]])
    xo_ref[...] = x_ref[...] + _bdot(ya_ref[...], wa_ref[...]) + _bdot(yb, wb_ref[...])


def _out_proj(x, ya, branch_arrays, wa_bf, wb_bf):
    n, d = x.shape
    tm = min(ROW_BLOCK, n)
    assert n % tm == 0
    row = lambda c: pl.BlockSpec((tm, c), lambda i: (i, 0))
    full = lambda shp: pl.BlockSpec(shp, lambda i: (0,) * len(shp))
    return pl.pallas_call(
        _out_proj_kernel,
        grid=(n // tm,),
        in_specs=[row(d), row(ya.shape[1])] + [row(a.shape[1]) for a in branch_arrays]
                 + [full(wa_bf.shape), full(wb_bf.shape)],
        out_specs=row(d),
        out_shape=jax.ShapeDtypeStruct((n, d), F32),
        compiler_params=_cparams(("arbitrary",)),
        name="out_proj",
    )(x, ya, *branch_arrays, wa_bf, wb_bf)


def _pool_groups(dmat, w_ref, b_ref, sc):
    n_g = w_ref.shape[0]
    gw = w_ref.shape[1]
    ys = [_bdot(dmat[g], w_ref[g]) + b_ref[g:g + 1] for g in range(n_g)]
    return jnp.concatenate(ys, axis=1) * sc


def _pool_prompt_kernel(x_ref, g_ref, w_ref, b_ref, sc_ref, xo_ref, hl_ref, carry_ref):
    s = pl.program_id(1)
    hist = carry_ref.shape[0]

    @pl.when(s == 0)
    def _():
        carry_ref[...] = jnp.zeros_like(carry_ref)

    x = x_ref[...]
    tm = x.shape[0]
    h = _rms(x, g_ref[...])
    ext = jnp.concatenate([carry_ref[...], h], axis=0)
    pos = s * tm + lax.broadcasted_iota(jnp.int32, (tm, 1), 0) + 1
    gw = w_ref.shape[1]
    acc = ext
    width = 1
    diffs = []
    for g, w in enumerate(POOL_WINDOWS):
        while width < w:
            acc = acc + pltpu.roll(acc, width, 0)
            width *= 2
        cols = slice(g * gw, (g + 1) * gw)
        win = acc[hist:, cols]
        div = jnp.minimum(pos, w).astype(F32)
        diffs.append(win / div - h[:, cols])
    xo_ref[...] = x + _pool_groups(diffs, w_ref, b_ref, sc_ref[...])
    last = ext[tm:tm + hist]
    carry_ref[...] = last
    hl_ref[...] = last


def _pool_prompt(x, g, w_bf, bias, scale):
    b, s, d = x.shape
    tm = min(ROW_BLOCK, s)
    hist = 16
    assert s % tm == 0 and tm >= hist and max(POOL_WINDOWS) <= hist
    assert all(w == 2 ** (i + 1) for i, w in enumerate(POOL_WINDOWS))
    row = pl.BlockSpec((None, tm, d), lambda i, j: (i, j, 0))
    full = lambda shp: pl.BlockSpec(shp, lambda i, j: (0,) * len(shp))
    return pl.pallas_call(
        _pool_prompt_kernel,
        grid=(b, s // tm),
        in_specs=[row, full((1, d)), full(w_bf.shape), full(bias.shape), full((1, d))],
        out_specs=[row, pl.BlockSpec((None, hist, d), lambda i, j: (i, 0, 0))],
        out_shape=[jax.ShapeDtypeStruct((b, s, d), F32), jax.ShapeDtypeStruct((b, hist, d), F32)],
        scratch_shapes=[pltpu.VMEM((hist, d), F32)],
        compiler_params=_cparams(("arbitrary", "arbitrary")),
        name="pool_prompt",
    )(x, g, w_bf, bias, scale)


def _pool_sample_kernel(x_ref, st_ref, g_ref, w_ref, b_ref, sc_ref, xo_ref, h_ref):
    x = x_ref[...]
    h = _rms(x, g_ref[...])
    n_st = st_ref.shape[0]
    gw = w_ref.shape[1]
    diffs = []
    for g, w in enumerate(POOL_WINDOWS):
        cols = slice(g * gw, (g + 1) * gw)
        tot = h[:, cols]
        for j in range(1, w):
            tot = tot + st_ref[n_st - j][:, cols]
        diffs.append(tot / float(w) - h[:, cols])
    xo_ref[...] = x + _pool_groups(diffs, w_ref, b_ref, sc_ref[...])
    h_ref[...] = h


def _pool_sample(x, state_t, g, w_bf, bias, scale):
    n, d = x.shape
    assert state_t.shape[0] + 1 >= max(POOL_WINDOWS)
    full = lambda shp: pl.BlockSpec(shp, lambda i: (0,) * len(shp))
    return pl.pallas_call(
        _pool_sample_kernel,
        grid=(1,),
        in_specs=[full(x.shape), full(state_t.shape), full((1, d)), full(w_bf.shape), full(bias.shape),
                  full((1, d))],
        out_specs=[full(x.shape), full(x.shape)],
        out_shape=[jax.ShapeDtypeStruct((n, d), F32)] * 2,
        compiler_params=_cparams(("arbitrary",)),
        name="pool_sample",
    )(x, state_t, g, w_bf, bias, scale)


def _top16_ranked(s):
    n, t = s.shape
    rows = lax.broadcasted_iota(jnp.int32, (n, t), 0).astype(F32)
    rows16 = lax.broadcasted_iota(jnp.int32, (PEER_TOPK, t), 0)

    def step(it, carry):
        s, rank, sv = carry
        m = jnp.max(s, axis=0, keepdims=True)
        first = jnp.min(jnp.where(s == m, rows, float(n)), axis=0, keepdims=True)
        sel = rows == first
        itf = it.astype(F32)
        return (jnp.where(sel, -jnp.inf, s), jnp.where(sel, itf, rank), jnp.where(rows16 == it, m, sv))

    init = (s, jnp.full((n, t), float(PEER_TOPK), F32), jnp.zeros((PEER_TOPK, t), F32))
    _, rank, sv = lax.fori_loop(0, PEER_TOPK, step, init)
    return rank, sv


def _top16_values(sa, sb):
    t = sa.shape[1]
    rows16 = lax.broadcasted_iota(jnp.int32, (PEER_TOPK, t), 0)

    def step(it, carry):
        a, b, sva, svb = carry
        ma = jnp.max(a, axis=0, keepdims=True)
        mb = jnp.max(b, axis=0, keepdims=True)
        return (jnp.where(a == ma, -jnp.inf, a), jnp.where(b == mb, -jnp.inf, b),
                jnp.where(rows16 == it, ma, sva), jnp.where(rows16 == it, mb, svb))

    zero = jnp.zeros((PEER_TOPK, t), F32)
    a, b, sva, svb = lax.fori_loop(0, PEER_TOPK, step, (sa, sb, zero, zero))
    cover = lambda x: jnp.sum(jnp.where(x == -jnp.inf, 1.0, 0.0), axis=0, keepdims=True)
    return sva, svb, cover(a), cover(b)


def _pair_cells(sv1, sv2):
    t = sv1.shape[1]
    half = SUBLANES
    tiles = [sv1[0:1] + sv2[0:half], sv1[0:1] + sv2[half:2 * half]]
    tiles += [sv1[r:r + 1] + sv2[0:half] for r in range(1, half)]
    tiles += [sv1[half:2 * half] + sv2[0:1]]
    cand0 = jnp.concatenate(tiles, axis=0)
    nrow = cand0.shape[0]
    i = lax.broadcasted_iota(jnp.int32, (nrow, t), 0)
    j = i - 2 * half
    mid = (lax.shift_right_arithmetic(j, 3) + 1) * PEER_TOPK + lax.bitwise_and(j, half - 1)
    idx = jnp.where(i < 2 * half, i,
                    jnp.where(i < nrow - half, mid, (i - (nrow - half) + half) * PEER_TOPK)).astype(F32)
    return cand0, idx


def _pairs_exact(cand0, idx):
    def step(_, carry):
        cand, chosen = carry
        m = jnp.max(cand, axis=0, keepdims=True)
        first = jnp.min(jnp.where(cand == m, idx, 1e9), axis=0, keepdims=True)
        sel = idx == first
        return jnp.where(sel, -jnp.inf, cand), jnp.where(sel, 1.0, chosen)

    _, chosen = lax.fori_loop(0, PEER_TOPK, step, (cand0, jnp.zeros_like(cand0)))
    return chosen


def _pairs_fast(cand0):
    def step(_, cand):
        return jnp.where(cand == jnp.max(cand, axis=0, keepdims=True), -jnp.inf, cand)

    cand = lax.fori_loop(0, PEER_TOPK, step, cand0)
    return jnp.where(cand == -jnp.inf, 1.0, 0.0)


def _lam_den(chosen, cand0):
    half = SUBLANES
    nrow = cand0.shape[0]
    den = jnp.sum(jnp.where(chosen > 0, jnp.exp(cand0 - cand0[0:1]), 0.0), axis=0, keepdims=True)
    lam = [jnp.sum(chosen[0:2 * half], axis=0, keepdims=True)]
    lam += [jnp.sum(chosen[2 * half + half * (r - 1):2 * half + half * r], axis=0, keepdims=True)
            for r in range(1, half)]
    lam += [chosen[nrow - half + r:nrow - half + r + 1] for r in range(half)]
    return lam, den


def _peer_select_kernel(x_ref, g_ref, wqt_ref, sk_ref, hb_ref, lam1_ref, rank2_ref, p1_ref, p2_ref, s_ref,
                        *, n_heads):
    h = _rms(x_ref[...], g_ref[...])
    hb = h.astype(BF16)
    hb_ref[...] = hb
    nk = sk_ref.shape[1]
    qt = lax.dot_general(wqt_ref[...], hb, (((1,), (1,)), ((), ())), preferred_element_type=F32)
    for c in range(2 * n_heads):
        s_ref[c] = jnp.dot(sk_ref[c], qt[c * nk:(c + 1) * nk], preferred_element_type=F32,
                           precision=lax.Precision.HIGHEST)

    def emit(hh, lam1, rank2, s1, s2, top1, top2, den):
        lam1_ref[hh] = lam1
        rank2_ref[hh] = rank2.astype(rank2_ref.dtype)
        p1_ref[hh] = jnp.exp(s1 - top1) / den
        p2_ref[hh] = jnp.exp(s2 - top2).astype(p2_ref.dtype)

    for hh in range(n_heads):
        s1 = s_ref[2 * hh]
        s2 = s_ref[2 * hh + 1]
        sv1, sv2, cov1, cov2 = _top16_values(s1, s2)
        cand0, _ = _pair_cells(sv1, sv2)
        chosen = _pairs_fast(cand0)
        lam, den = _lam_den(chosen, cand0)
        lam1 = jnp.broadcast_to(lam[0], s1.shape)
        rank2 = jnp.zeros_like(s2)
        for r in range(PEER_TOPK):
            lam1 = jnp.where(sv1[r:r + 1] > s1, lam[r + 1] if r + 1 < PEER_TOPK else 0.0, lam1)
            rank2 = jnp.where(sv2[r:r + 1] > s2, float(r + 1), rank2)
        emit(hh, lam1, rank2, s1, s2, sv1[0:1], sv2[0:1], den)
        cov3 = jnp.sum(chosen, axis=0, keepdims=True)
        k = float(PEER_TOPK)
        tied = jnp.max(jnp.abs(cov1 - k) + jnp.abs(cov2 - k) + jnp.abs(cov3 - k)) > 0.0

        @pl.when(tied)
        def _(hh=hh):
            s1 = s_ref[2 * hh]
            s2 = s_ref[2 * hh + 1]
            rank1, sv1 = _top16_ranked(s1)
            rank2, sv2 = _top16_ranked(s2)
            cand0, idx = _pair_cells(sv1, sv2)
            lam, den = _lam_den(_pairs_exact(cand0, idx), cand0)
            lam1 = jnp.zeros_like(rank1)
            for r in range(PEER_TOPK):
                lam1 = jnp.where(rank1 == float(r), lam[r], lam1)
            emit(hh, lam1, rank2, s1, s2, sv1[0:1], sv2[0:1], den)


def _peer_select(x, g, wqt_bf, sk):
    n, d = x.shape
    n_heads = sk.shape[0] // 2
    nk = sk.shape[1]
    t = SEL_BLOCK
    assert n % t == 0 and nk == N_KEYS
    full = lambda shp: pl.BlockSpec(shp, lambda i: (0,) * len(shp))
    sel = pl.BlockSpec((n_heads, nk, t), lambda i: (0, 0, i))
    sel_f32 = jax.ShapeDtypeStruct((n_heads, nk, n), F32)
    sel_bf16 = jax.ShapeDtypeStruct((n_heads, nk, n), BF16)
    return pl.pallas_call(
        functools.partial(_peer_select_kernel, n_heads=n_heads),
        grid=(n // t,),
        in_specs=[pl.BlockSpec((t, d), lambda i: (i, 0)), full((1, d)), full(wqt_bf.shape), full(sk.shape)],
        out_specs=[pl.BlockSpec((t, d), lambda i: (i, 0)), sel, sel, sel, sel],
        out_shape=[jax.ShapeDtypeStruct((n, d), BF16), sel_f32, sel_bf16, sel_f32, sel_bf16],
        scratch_shapes=[pltpu.VMEM((2 * n_heads, nk, t), F32)],
        compiler_params=_cparams(("arbitrary",)),
        name="peer_select",
    )(x, g, wqt_bf, sk)


def _gelu(a):
    return 0.5 * a * (1.0 + lax.erf(a * (1.0 / math.sqrt(2.0))))


def _peer_dense_kernel(x_ref, hb_ref, u_ref, vt_ref, lam1_ref, rank2_ref, p1_ref, p2_ref, xo_ref,
                       acc_ref, a0_ref, a1_ref, z0_ref, z1_ref, *, n_heads, n_tiles):
    s = pl.program_id(1)
    t = hb_ref.shape[0]
    nk = rank2_ref.shape[1]

    @pl.when(s == 0)
    def _():
        acc_ref[...] = jnp.zeros_like(acc_ref)
        a1_ref[...] = jnp.zeros_like(a1_ref)
        z0_ref[...] = jnp.zeros_like(z0_ref)

    eb = jnp.clip(s - 1, 0, n_tiles - 1)
    grp = pl.ds(pl.multiple_of(eb * SUBLANES, SUBLANES), SUBLANES)

    te = u_ref.shape[0]
    d = vt_ref.shape[0]
    n_tc = t // LANES

    def act_chunk(a_new, j):
        rows = slice(j * (te // MXU_CHUNKS), (j + 1) * (te // MXU_CHUNKS))
        a_new[rows, :] = lax.dot_general(u_ref[rows, :], hb_ref[...], (((1,), (1,)), ((), ())),
                                         preferred_element_type=F32)

    def val_chunk(z_old, j):
        rows = slice(j * (d // MXU_CHUNKS), (j + 1) * (d // MXU_CHUNKS))
        acc_ref[rows, :] += jnp.dot(vt_ref[rows, :], z_old[...], preferred_element_type=F32)

    def gate_chunk(a_cur, z_new, c):
        ii, tc = divmod(c, n_tc)
        lanes = slice(tc * LANES, (tc + 1) * LANES)
        rows = slice(ii * nk, (ii + 1) * nk)
        gate = jnp.zeros((nk, LANES), BF16)
        for hh in range(n_heads):
            l1 = jnp.broadcast_to(lam1_ref[hh, grp, lanes][ii:ii + 1], (nk, LANES)).astype(BF16)
            pb = jnp.broadcast_to(p1_ref[hh, grp, lanes][ii:ii + 1], (nk, LANES)).astype(BF16)
            gate = gate + jnp.where(l1 > rank2_ref[hh, :, lanes], pb * p2_ref[hh, :, lanes],
                                    jnp.zeros((), BF16))
        z_new[rows, lanes] = gate * _gelu(a_cur[rows, lanes]).astype(BF16)

    def step(a_new, a_cur, z_new, z_old):
        n_gate = SUBLANES * n_tc
        c = 0
        for k in range(2 * MXU_CHUNKS):
            if k % 2 == 0:
                act_chunk(a_new, k // 2)
            else:
                val_chunk(z_old, k // 2)
            while c < (k + 1) * n_gate // (2 * MXU_CHUNKS):
                gate_chunk(a_cur, z_new, c)
                c += 1

    @pl.when(s % 2 == 0)
    def _():
        step(a0_ref, a1_ref, z1_ref, z0_ref)

    @pl.when(s % 2 == 1)
    def _():
        step(a1_ref, a0_ref, z0_ref, z1_ref)

    @pl.when(s == n_tiles + 1)
    def _():
        xo_ref[...] = x_ref[...] + acc_ref[...].T


def _peer_dense(x, hb, u_bf, vt_bf, lam1, rank2, p1, p2):
    n, d = x.shape
    n_exp = u_bf.shape[0]
    n_heads, nk, _ = lam1.shape
    t = min(DENSE_TOKENS, n)
    te = DENSE_EXPERTS
    assert n % t == 0 and n_exp % te == 0 and te == SUBLANES * nk and n_exp == nk * nk and t % LANES == 0
    ne = n_exp // te
    tok = pl.BlockSpec((t, d), lambda i, s: (i, 0))
    sel = pl.BlockSpec((n_heads, nk, t), lambda i, s: (0, 0, i))
    return pl.pallas_call(
        functools.partial(_peer_dense_kernel, n_heads=n_heads, n_tiles=ne),
        grid=(n // t, ne + 2),
        in_specs=[tok, tok,
                  pl.BlockSpec((te, d), lambda i, s: (jnp.minimum(s, ne - 1), 0)),
                  pl.BlockSpec((d, te), lambda i, s: (0, jnp.clip(s - 2, 0, ne - 1))),
                  sel, sel, sel, sel],
        out_specs=tok,
        out_shape=jax.ShapeDtypeStruct((n, d), F32),
        scratch_shapes=[pltpu.VMEM((d, t), F32), pltpu.VMEM((te, t), F32), pltpu.VMEM((te, t), F32),
                        pltpu.VMEM((te, t), BF16), pltpu.VMEM((te, t), BF16)],
        compiler_params=_cparams(("arbitrary", "arbitrary")),
        name="peer_dense",
    )(x, hb, u_bf, vt_bf, lam1, rank2, p1, p2)


def _peer(x, g, wqt_bf, sk, u_bf, vt_bf):
    n = x.shape[0]
    pad = (-n) % max(SEL_BLOCK, LANES)
    xp = jnp.pad(x, ((0, pad), (0, 0))) if pad else x
    hb, lam1, rank2, p1, p2 = _peer_select(xp, g, wqt_bf, sk)
    out = _peer_dense(xp, hb, u_bf, vt_bf, lam1, rank2, p1, p2)
    return out[:n] if pad else out


def _final_norm_kernel(x_ref, g_ref, o_ref):
    o_ref[...] = _rms(x_ref[...], g_ref[...])


def _final_norm(x, g):
    n, d = x.shape
    tm = min(ROW_BLOCK, n)
    assert n % tm == 0
    return pl.pallas_call(
        _final_norm_kernel,
        grid=(n // tm,),
        in_specs=[pl.BlockSpec((tm, d), lambda i: (i, 0)), pl.BlockSpec((1, d), lambda i: (0, 0))],
        out_specs=pl.BlockSpec((tm, d), lambda i: (i, 0)),
        out_shape=jax.ShapeDtypeStruct((n, d), F32),
        compiler_params=_cparams(("arbitrary",)),
        name="final_norm",
    )(x, g)


def kernel(x_prompt, x_sample, state_conv, state_win_k, state_win_v, state_pool, norm_mix, w_in_ab, conv_w,
           w_out_ab, pool_w, pool_b, pool_scale, norm_ffn, peer_wq, peer_subkeys, peer_u, peer_v, norm_final):
    b, s, d = x_prompt.shape
    bs, ts, _ = x_sample.shape
    assert ts == 1
    depth = norm_mix.shape[0]
    d_conv = conv_w.shape[2]
    n_heads_b, hd = state_win_k.shape[3], state_win_k.shape[4]
    d_att = n_heads_b * hd
    wb_s = state_win_k.shape[2]
    wb_p = min(max(w for w, _ in DILATED_PAIRS), s)
    pool_state = state_pool.shape[2]
    q_scale = float(hd) ** -0.5
    n_peer_heads = peer_subkeys.shape[1]

    xp = x_prompt.reshape(b * s, d)
    xs = x_sample.reshape(bs, d)
    conv_p, conv_s, wk_p, wk_s, wv_p, wv_s, pool_p, pool_s = [], [], [], [], [], [], [], []
    for l in range(depth):
        g_mix = norm_mix[l][None]
        if l % 2 == 0:
            e = l // 2
            w_in = w_in_ab[e].astype(BF16)
            wa = w_out_ab[e, :d_conv].astype(BF16)
            wbm = w_out_ab[e, d_conv:].astype(BF16)
            ya, q, k, v, ul = _even_in_prompt(xp.reshape(b, s, d), g_mix, w_in, conv_w[e], d_conv, d_att, q_scale)
            branches = [_attn_branch(q, k, v, w, dl, n_heads_b) for w, dl in DILATED_PAIRS]
            o_list = [br[0].reshape(b * s, d_att) for br in branches]
            l_list = [br[1].reshape(b * s, d_att) for br in branches]
            xp = _out_proj(xp, ya.reshape(b * s, d_conv), o_list + l_list, wa, wbm)
            conv_p.append(ul[:, SUBLANES - 2:])
            wk_p.append(k[:, s - wb_p:].reshape(b, wb_p, n_heads_b, hd))
            wv_p.append(v[:, s - wb_p:].reshape(b, wb_p, n_heads_b, hd))
            cst = state_conv[e]
            ya_s, q_s, k_s, v_s, u_s = _even_in_sample(xs, g_mix, w_in, conv_w[e], cst[:, 1], cst[:, 0],
                                                       d_conv, d_att, q_scale)
            kst = state_win_k[e].reshape(bs, wb_s, d_att)
            vst = state_win_v[e].reshape(bs, wb_s, d_att)
            yb_s = _attn_sample(q_s, k_s, v_s, kst, vst, n_heads_b)
            xs = _out_proj(xs, ya_s, [yb_s], wa, wbm)
            conv_s.append(jnp.stack([cst[:, 1], u_s], axis=1))
            wk_s.append(jnp.concatenate([kst[:, 1:], k_s[:, None]], axis=1).reshape(bs, wb_s, n_heads_b, hd))
            wv_s.append(jnp.concatenate([vst[:, 1:], v_s[:, None]], axis=1).reshape(bs, wb_s, n_heads_b, hd))
        else:
            o = l // 2
            pw = pool_w[o].astype(BF16)
            xp3, hl = _pool_prompt(xp.reshape(b, s, d), g_mix, pw, pool_b[o], pool_scale[o][None])
            xp = xp3.reshape(b * s, d)
            pool_p.append(hl[:, hl.shape[1] - pool_state:])
            st = state_pool[o]
            xs, h_s = _pool_sample(xs, jnp.swapaxes(st, 0, 1), g_mix, pw, pool_b[o], pool_scale[o][None])
            pool_s.append(jnp.concatenate([st[:, 1:], h_s[:, None]], axis=1))
        g_ffn = norm_ffn[l][None]
        wqt = peer_wq[l].T.astype(BF16)
        sk = peer_subkeys[l].reshape(2 * n_peer_heads, N_KEYS, -1)
        u_bf = peer_u[l].astype(BF16)
        vt_bf = peer_v[l].T.astype(BF16)
        xp = _peer(xp, g_ffn, wqt, sk, u_bf, vt_bf)
        xs = _peer(xs, g_ffn, wqt, sk, u_bf, vt_bf)
    gf = norm_final[None]
    y_prompt = _final_norm(xp, gf).reshape(b, s, d)
    y_sample = _final_norm(xs, gf).reshape(bs, ts, d)
    return (y_prompt, y_sample, jnp.stack(conv_p), jnp.stack(conv_s), jnp.stack(wk_p), jnp.stack(wk_s),
            jnp.stack(wv_p), jnp.stack(wv_s), jnp.stack(pool_p), jnp.stack(pool_s))
```

```python
import functools
import math

import jax
import jax.numpy as jnp
from jax import lax
from jax.experimental import pallas as pl
from jax.experimental.pallas import tpu as pltpu

F32 = jnp.float32
BF16 = jnp.bfloat16

NORM_EPS = 1e-6
NEG_INF = -1e30
DILATED_PAIRS = ((128, 1), (512, 4), (2048, 16))
POOL_WINDOWS = (2, 4, 8, 16)
PEER_TOPK = 16
N_KEYS = 128

LANES = 128
SUBLANES = 8
VMEM_LIMIT = 56 * 1024 * 1024

ATT_BLOCK = 128
ROW_BLOCK = 512
SEL_BLOCK = 128
DENSE_TOKENS = 256
MXU_CHUNKS = 4
DENSE_EXPERTS = SUBLANES * N_KEYS


def _cparams(sem, flags=None):
    return pltpu.CompilerParams(dimension_semantics=sem, vmem_limit_bytes=VMEM_LIMIT, flags=flags)


def _rms(x, g):
    r = lax.rsqrt(jnp.mean(x * x, axis=-1, keepdims=True) + NORM_EPS)
    return (x * r) * g


def _bdot(a, b):
    return jnp.dot(a.astype(BF16), b.astype(BF16), preferred_element_type=F32)


def _bdot_nt(a, b):
    return lax.dot_general(a.astype(BF16), b.astype(BF16), (((1,), (1,)), ((), ())),
                           preferred_element_type=F32)


def _even_in_body(x, g, w, cw, u1_fn, d_conv, d_att, q_scale):
    h = _rms(x, g)
    p = _bdot(h, w)
    gate_b = p[:, 0:d_conv]
    gate_c = p[:, d_conv:2 * d_conv]
    xv = p[:, 2 * d_conv:3 * d_conv]
    o = 3 * d_conv
    q = p[:, o:o + d_att] * q_scale
    k = p[:, o + d_att:o + 2 * d_att]
    v = p[:, o + 2 * d_att:o + 3 * d_att]
    u = gate_c * xv
    u1, u2 = u1_fn(u)
    y = cw[0:1] * u2 + cw[1:2] * u1 + cw[2:3] * u
    return gate_b * y, q, k, v, u


def _even_in_seq_kernel(x_ref, g_ref, w_ref, cw_ref, ya_ref, q_ref, k_ref, v_ref, ul_ref, carry_ref,
                        *, d_conv, d_att, q_scale):
    s = pl.program_id(1)

    @pl.when(s == 0)
    def _():
        carry_ref[...] = jnp.zeros_like(carry_ref)

    prev = carry_ref[...]
    tm = x_ref.shape[0]

    def shifted(u):
        rows = lax.broadcasted_iota(jnp.int32, u.shape, 0)
        u1 = jnp.where(rows == 0, prev[7:8], pltpu.roll(u, 1, 0))
        u2 = pltpu.roll(u, 2, 0)
        u2 = jnp.where(rows == 0, prev[6:7], jnp.where(rows == 1, prev[7:8], u2))
        return u1, u2

    ya, q, k, v, u = _even_in_body(x_ref[...], g_ref[...], w_ref[...], cw_ref[...], shifted,
                                   d_conv, d_att, q_scale)
    ya_ref[...] = ya
    q_ref[...] = q
    k_ref[...] = k
    v_ref[...] = v
    last = u[tm - SUBLANES:tm]
    carry_ref[...] = last
    ul_ref[...] = last


def _even_in_rows_kernel(x_ref, g_ref, w_ref, cw_ref, u1_ref, u2_ref, ya_ref, q_ref, k_ref, v_ref, u_ref,
                         *, d_conv, d_att, q_scale):
    ya, q, k, v, u = _even_in_body(x_ref[...], g_ref[...], w_ref[...], cw_ref[...],
                                   lambda _: (u1_ref[...], u2_ref[...]), d_conv, d_att, q_scale)
    ya_ref[...] = ya
    q_ref[...] = q
    k_ref[...] = k
    v_ref[...] = v
    u_ref[...] = u


def _even_in_prompt(x, g, w_bf, cw, d_conv, d_att, q_scale):
    b, s, d = x.shape
    tm = min(ROW_BLOCK, s)
    assert s % tm == 0 and tm % SUBLANES == 0
    ncol = w_bf.shape[1]
    row = lambda c: pl.BlockSpec((None, tm, c), lambda i, j: (i, j, 0))
    full = lambda shp: pl.BlockSpec(shp, lambda i, j: (0,) * len(shp))
    outs = pl.pallas_call(
        functools.partial(_even_in_seq_kernel, d_conv=d_conv, d_att=d_att, q_scale=q_scale),
        grid=(b, s // tm),
        in_specs=[row(d), full((1, d)), full((d, ncol)), full((cw.shape[0], d_conv))],
        out_specs=[row(d_conv), row(d_att), row(d_att), row(d_att),
                   pl.BlockSpec((None, SUBLANES, d_conv), lambda i, j: (i, 0, 0))],
        out_shape=[jax.ShapeDtypeStruct((b, s, d_conv), F32)] + [jax.ShapeDtypeStruct((b, s, d_att), F32)] * 3
                  + [jax.ShapeDtypeStruct((b, SUBLANES, d_conv), F32)],
        scratch_shapes=[pltpu.VMEM((SUBLANES, d_conv), F32)],
        compiler_params=_cparams(("arbitrary", "arbitrary")),
        name="even_in_prompt",
    )(x, g, w_bf, cw)
    return outs


def _even_in_sample(x, g, w_bf, cw, u1, u2, d_conv, d_att, q_scale):
    n, d = x.shape
    ncol = w_bf.shape[1]
    full = lambda shp: pl.BlockSpec(shp, lambda i: (0,) * len(shp))
    return pl.pallas_call(
        functools.partial(_even_in_rows_kernel, d_conv=d_conv, d_att=d_att, q_scale=q_scale),
        grid=(1,),
        in_specs=[full((n, d)), full((1, d)), full((d, ncol)), full((cw.shape[0], d_conv)),
                  full((n, d_conv)), full((n, d_conv))],
        out_specs=[full((n, d_conv)), full((n, d_att)), full((n, d_att)), full((n, d_att)), full((n, d_conv))],
        out_shape=[jax.ShapeDtypeStruct((n, d_conv), F32)] + [jax.ShapeDtypeStruct((n, d_att), F32)] * 3
                  + [jax.ShapeDtypeStruct((n, d_conv), F32)],
        compiler_params=_cparams(("arbitrary",)),
        name="even_in_sample",
    )(x, g, w_bf, cw, u1, u2)


def _attn_branch_kernel(q_ref, kp_ref, kc_ref, vp_ref, vc_ref, o_ref, l_ref, *, n_heads, hd, n_back):
    n = pl.program_id(2)
    blk = q_ref.shape[0]
    q = q_ref[...]
    k = jnp.concatenate([kp_ref[...], kc_ref[...]], axis=0)
    v = jnp.concatenate([vp_ref[...], vc_ref[...]], axis=0)
    qi = lax.broadcasted_iota(jnp.int32, (blk, 2 * blk), 0)
    ki = lax.broadcasted_iota(jnp.int32, (blk, 2 * blk), 1)
    dist = qi + blk - ki
    has_prev = jnp.where(n > 0, 0, blk)
    mask = (dist >= 0) & (dist <= n_back) & (ki >= has_prev)
    o_parts, l_parts = [], []
    for h in range(n_heads):
        sl = slice(h * hd, (h + 1) * hd)
        s = _bdot_nt(q[:, sl], k[:, sl])
        s = jnp.where(mask, s, NEG_INF)
        m = jnp.max(s, axis=-1, keepdims=True)
        p = jnp.exp(s - m)
        den = jnp.sum(p, axis=-1, keepdims=True)
        o_parts.append(_bdot(p, v[:, sl]) / den)
        l_parts.append(jnp.broadcast_to(m + jnp.log(den), (blk, hd)))
    o_ref[...] = jnp.concatenate(o_parts, axis=1)
    l_ref[...] = jnp.concatenate(l_parts, axis=1)


def _attn_branch(q, k, v, window, dil, n_heads):
    b, s, da = q.shape
    hd = da // n_heads
    n_back = window // dil
    assert s % (dil * ATT_BLOCK) == 0 and n_back <= ATT_BLOCK
    L = s // dil
    nb = L // ATT_BLOCK
    view = lambda t: t.reshape(b, L, dil * da)
    cur = pl.BlockSpec((None, ATT_BLOCK, da), lambda i, r, n: (i, n, r))
    prev = pl.BlockSpec((None, ATT_BLOCK, da), lambda i, r, n: (i, jnp.maximum(n - 1, 0), r))
    o, l = pl.pallas_call(
        functools.partial(_attn_branch_kernel, n_heads=n_heads, hd=hd, n_back=n_back),
        grid=(b, dil, nb),
        in_specs=[cur, prev, cur, prev, cur],
        out_specs=[cur, cur],
        out_shape=[jax.ShapeDtypeStruct((b, L, dil * da), F32)] * 2,
        compiler_params=_cparams(("arbitrary", "arbitrary", "arbitrary")),
        name=f"attn_branch_d{dil}",
    )(view(q), view(k), view(k), view(v), view(v))
    return o.reshape(b, s, da), l.reshape(b, s, da)


def _attn_sample_kernel(q_ref, kn_ref, vn_ref, *refs, n_heads, hd):
    n_br = (len(refs) - 1) // 2
    k_refs, v_refs, y_ref = refs[:n_br], refs[n_br:2 * n_br], refs[-1]
    q = q_ref[...]
    kn = kn_ref[...]
    vn = vn_ref[...]
    o_rows, l_rows = [], []
    for g in range(n_br):
        kq = k_refs[g][...] * q
        vv = v_refs[g][...]
        sq = kn * q
        o_parts, l_parts = [], []
        for h in range(n_heads):
            sl = slice(h * hd, (h + 1) * hd)
            s = jnp.sum(kq[:, sl], axis=-1, keepdims=True)
            s0 = jnp.sum(sq[:, sl], axis=-1, keepdims=True)
            m = jnp.maximum(jnp.max(s, axis=0, keepdims=True), s0)
            p = jnp.exp(s - m)
            p0 = jnp.exp(s0 - m)
            den = jnp.sum(p, axis=0, keepdims=True) + p0
            o = (jnp.sum(p * vv[:, sl], axis=0, keepdims=True) + p0 * vn[:, sl]) / den
            o_parts.append(o)
            l_parts.append(jnp.broadcast_to(m + jnp.log(den), (1, hd)))
        o_rows.append(jnp.concatenate(o_parts, axis=1))
        l_rows.append(jnp.concatenate(l_parts, axis=1))
    y_ref[...] = _merge_branches(o_rows, l_rows)


def _merge_branches(o_list, l_list):
    m = functools.reduce(jnp.maximum, l_list)
    e = [jnp.exp(l - m) for l in l_list]
    num = functools.reduce(lambda a, b: a + b, [ei * oi for ei, oi in zip(e, o_list)])
    return num / functools.reduce(lambda a, b: a + b, e)


def _attn_sample(q, k_new, v_new, k_state, v_state, n_heads):
    b, da = q.shape
    wb = k_state.shape[1]
    hd = da // n_heads
    row = pl.BlockSpec((None, 1, da), lambda i: (i, 0, 0))
    ins, specs = [], []
    for st in (k_state, v_state):
        for window, dil in DILATED_PAIRS:
            n_back = window // dil
            assert n_back * dil <= wb and wb % dil == 0 and (wb // dil) % n_back == 0
            L = wb // dil
            ins.append(st.reshape(b, L, dil * da))
            specs.append(pl.BlockSpec((None, n_back, da), lambda i, L=L, nbk=n_back: (i, L // nbk - 1, 0)))
    y = pl.pallas_call(
        functools.partial(_attn_sample_kernel, n_heads=n_heads, hd=hd),
        grid=(b,),
        in_specs=[row, row, row] + specs,
        out_specs=row,
        out_shape=jax.ShapeDtypeStruct((b, 1, da), F32),
        compiler_params=_cparams(("arbitrary",)),
        name="attn_sample",
    )(q[:, None], k_new[:, None], v_new[:, None], *ins)
    return y[:, 0]


def _out_proj_kernel(x_ref, ya_ref, *refs):
    wa_ref, wb_ref, xo_ref = refs[-3:]
    br = refs[:-3]
    if len(br) == 1:
        yb = br[0][...]
    else:
        n_br = len(br) // 2
        yb = _merge_branches([r[...] for r in br[:n_br]], [r[...] for r in br[n_br:]])
    xo_ref[...] = x_ref[...] + _bdot(ya_ref[...], wa_ref[...]) + _bdot(yb, wb_ref[...])


def _out_proj(x, ya, branch_arrays, wa_bf, wb_bf):
    n, d = x.shape
    tm = min(ROW_BLOCK, n)
    assert n % tm == 0
    row = lambda c: pl.BlockSpec((tm, c), lambda i: (i, 0))
    full = lambda shp: pl.BlockSpec(shp, lambda i: (0,) * len(shp))
    return pl.pallas_call(
        _out_proj_kernel,
        grid=(n // tm,),
        in_specs=[row(d), row(ya.shape[1])] + [row(a.shape[1]) for a in branch_arrays]
                 + [full(wa_bf.shape), full(wb_bf.shape)],
        out_specs=row(d),
        out_shape=jax.ShapeDtypeStruct((n, d), F32),
        compiler_params=_cparams(("arbitrary",)),
        name="out_proj",
    )(x, ya, *branch_arrays, wa_bf, wb_bf)


def _pool_groups(dmat, w_ref, b_ref, sc):
    n_g = w_ref.shape[0]
    gw = w_ref.shape[1]
    ys = [_bdot(dmat[g], w_ref[g]) + b_ref[g:g + 1] for g in range(n_g)]
    return jnp.concatenate(ys, axis=1) * sc


def _pool_prompt_kernel(x_ref, g_ref, w_ref, b_ref, sc_ref, xo_ref, hl_ref, carry_ref):
    s = pl.program_id(1)
    hist = carry_ref.shape[0]

    @pl.when(s == 0)
    def _():
        carry_ref[...] = jnp.zeros_like(carry_ref)

    x = x_ref[...]
    tm = x.shape[0]
    h = _rms(x, g_ref[...])
    ext = jnp.concatenate([carry_ref[...], h], axis=0)
    pos = s * tm + lax.broadcasted_iota(jnp.int32, (tm, 1), 0) + 1
    gw = w_ref.shape[1]
    acc = ext
    width = 1
    diffs = []
    for g, w in enumerate(POOL_WINDOWS):
        while width < w:
            acc = acc + pltpu.roll(acc, width, 0)
            width *= 2
        cols = slice(g * gw, (g + 1) * gw)
        win = acc[hist:, cols]
        div = jnp.minimum(pos, w).astype(F32)
        diffs.append(win / div - h[:, cols])
    xo_ref[...] = x + _pool_groups(diffs, w_ref, b_ref, sc_ref[...])
    last = ext[tm:tm + hist]
    carry_ref[...] = last
    hl_ref[...] = last


def _pool_prompt(x, g, w_bf, bias, scale):
    b, s, d = x.shape
    tm = min(ROW_BLOCK, s)
    hist = 16
    assert s % tm == 0 and tm >= hist and max(POOL_WINDOWS) <= hist
    assert all(w == 2 ** (i + 1) for i, w in enumerate(POOL_WINDOWS))
    row = pl.BlockSpec((None, tm, d), lambda i, j: (i, j, 0))
    full = lambda shp: pl.BlockSpec(shp, lambda i, j: (0,) * len(shp))
    return pl.pallas_call(
        _pool_prompt_kernel,
        grid=(b, s // tm),
        in_specs=[row, full((1, d)), full(w_bf.shape), full(bias.shape), full((1, d))],
        out_specs=[row, pl.BlockSpec((None, hist, d), lambda i, j: (i, 0, 0))],
        out_shape=[jax.ShapeDtypeStruct((b, s, d), F32), jax.ShapeDtypeStruct((b, hist, d), F32)],
        scratch_shapes=[pltpu.VMEM((hist, d), F32)],
        compiler_params=_cparams(("arbitrary", "arbitrary")),
        name="pool_prompt",
    )(x, g, w_bf, bias, scale)


def _pool_sample_kernel(x_ref, st_ref, g_ref, w_ref, b_ref, sc_ref, xo_ref, h_ref):
    x = x_ref[...]
    h = _rms(x, g_ref[...])
    n_st = st_ref.shape[0]
    gw = w_ref.shape[1]
    diffs = []
    for g, w in enumerate(POOL_WINDOWS):
        cols = slice(g * gw, (g + 1) * gw)
        tot = h[:, cols]
        for j in range(1, w):
            tot = tot + st_ref[n_st - j][:, cols]
        diffs.append(tot / float(w) - h[:, cols])
    xo_ref[...] = x + _pool_groups(diffs, w_ref, b_ref, sc_ref[...])
    h_ref[...] = h


def _pool_sample(x, state_t, g, w_bf, bias, scale):
    n, d = x.shape
    assert state_t.shape[0] + 1 >= max(POOL_WINDOWS)
    full = lambda shp: pl.BlockSpec(shp, lambda i: (0,) * len(shp))
    return pl.pallas_call(
        _pool_sample_kernel,
        grid=(1,),
        in_specs=[full(x.shape), full(state_t.shape), full((1, d)), full(w_bf.shape), full(bias.shape),
                  full((1, d))],
        out_specs=[full(x.shape), full(x.shape)],
        out_shape=[jax.ShapeDtypeStruct((n, d), F32)] * 2,
        compiler_params=_cparams(("arbitrary",)),
        name="pool_sample",
    )(x, state_t, g, w_bf, bias, scale)


def _top16_ranked(s):
    n, t = s.shape
    rows = lax.broadcasted_iota(jnp.int32, (n, t), 0).astype(F32)
    rows16 = lax.broadcasted_iota(jnp.int32, (PEER_TOPK, t), 0)

    def step(it, carry):
        s, rank, sv = carry
        m = jnp.max(s, axis=0, keepdims=True)
        first = jnp.min(jnp.where(s == m, rows, float(n)), axis=0, keepdims=True)
        sel = rows == first
        itf = jnp.asarray(it, jnp.int32).astype(F32)
        return (jnp.where(sel, -jnp.inf, s), jnp.where(sel, itf, rank), jnp.where(rows16 == it, m, sv))

    init = (s, jnp.full((n, t), float(PEER_TOPK), F32), jnp.zeros((PEER_TOPK, t), F32))
    _, rank, sv = lax.fori_loop(0, PEER_TOPK, step, init)
    return rank, sv


def _top16_values(sa, sb):
    t = sa.shape[1]
    rows16 = lax.broadcasted_iota(jnp.int32, (PEER_TOPK, t), 0)

    def step(it, carry):
        a, b, sva, svb = carry
        ma = jnp.max(a, axis=0, keepdims=True)
        mb = jnp.max(b, axis=0, keepdims=True)
        return (jnp.where(a == ma, -jnp.inf, a), jnp.where(b == mb, -jnp.inf, b),
                jnp.where(rows16 == it, ma, sva), jnp.where(rows16 == it, mb, svb))

    zero = jnp.zeros((PEER_TOPK, t), F32)
    a, b, sva, svb = lax.fori_loop(0, PEER_TOPK, step, (sa, sb, zero, zero))
    cover = lambda x: jnp.sum(jnp.where(x == -jnp.inf, 1.0, 0.0), axis=0, keepdims=True)
    return sva, svb, cover(a), cover(b)


def _pair_cells(sv1, sv2):
    t = sv1.shape[1]
    half = SUBLANES
    tiles = [sv1[0:1] + sv2[0:half], sv1[0:1] + sv2[half:2 * half]]
    tiles += [sv1[r:r + 1] + sv2[0:half] for r in range(1, half)]
    tiles += [sv1[half:2 * half] + sv2[0:1]]
    cand0 = jnp.concatenate(tiles, axis=0)
    nrow = cand0.shape[0]
    i = lax.broadcasted_iota(jnp.int32, (nrow, t), 0)
    j = i - 2 * half
    mid = (lax.shift_right_arithmetic(j, 3) + 1) * PEER_TOPK + lax.bitwise_and(j, half - 1)
    idx = jnp.where(i < 2 * half, i,
                    jnp.where(i < nrow - half, mid, (i - (nrow - half) + half) * PEER_TOPK)).astype(F32)
    return cand0, idx


def _pairs_exact(cand0, idx):
    def step(_, carry):
        cand, chosen = carry
        m = jnp.max(cand, axis=0, keepdims=True)
        first = jnp.min(jnp.where(cand == m, idx, 1e9), axis=0, keepdims=True)
        sel = idx == first
        return jnp.where(sel, -jnp.inf, cand), jnp.where(sel, 1.0, chosen)

    _, chosen = lax.fori_loop(0, PEER_TOPK, step, (cand0, jnp.zeros_like(cand0)))
    return chosen


def _pairs_fast(cand0):
    def step(_, cand):
        return jnp.where(cand == jnp.max(cand, axis=0, keepdims=True), -jnp.inf, cand)

    cand = lax.fori_loop(0, PEER_TOPK, step, cand0)
    return jnp.where(cand == -jnp.inf, 1.0, 0.0)


def _lam_den(chosen, cand0):
    half = SUBLANES
    nrow = cand0.shape[0]
    den = jnp.sum(jnp.where(chosen > 0, jnp.exp(cand0 - cand0[0:1]), 0.0), axis=0, keepdims=True)
    lam = [jnp.sum(chosen[0:2 * half], axis=0, keepdims=True)]
    lam += [jnp.sum(chosen[2 * half + half * (r - 1):2 * half + half * r], axis=0, keepdims=True)
            for r in range(1, half)]
    lam += [chosen[nrow - half + r:nrow - half + r + 1] for r in range(half)]
    return lam, den


def _peer_select_kernel(x_ref, g_ref, wqt_ref, sk_ref, hb_ref, lam1_ref, rank2_ref, p1_ref, p2_ref, s_ref,
                        *, n_heads):
    h = _rms(x_ref[...], g_ref[...])
    hb = h.astype(BF16)
    hb_ref[...] = hb
    nk = sk_ref.shape[1]
    qt = lax.dot_general(wqt_ref[...], hb, (((1,), (1,)), ((), ())), preferred_element_type=F32)
    q_hi = qt.astype(BF16)
    q_lo = (qt - q_hi.astype(F32)).astype(BF16)
    for c in range(2 * n_heads):
        rows = slice(c * nk, (c + 1) * nk)
        q3 = jnp.concatenate([q_hi[rows], q_hi[rows], q_lo[rows]], axis=0)
        s_ref[c] = jnp.dot(sk_ref[c], q3, preferred_element_type=F32)

    def emit(hh, lam1, rank2, s1, s2, top1, top2, den):
        lam1_ref[hh] = lam1
        rank2_ref[hh] = rank2.astype(rank2_ref.dtype)
        p1_ref[hh] = jnp.exp(s1 - top1) / den
        p2_ref[hh] = jnp.exp(s2 - top2).astype(p2_ref.dtype)

    for hh in range(n_heads):
        s1 = s_ref[2 * hh]
        s2 = s_ref[2 * hh + 1]
        sv1, sv2, cov1, cov2 = _top16_values(s1, s2)
        cand0, _ = _pair_cells(sv1, sv2)
        chosen = _pairs_fast(cand0)
        lam, den = _lam_den(chosen, cand0)
        lam1 = jnp.broadcast_to(lam[0], s1.shape)
        rank2 = jnp.zeros_like(s2)
        for r in range(PEER_TOPK):
            lam1 = jnp.where(sv1[r:r + 1] > s1, lam[r + 1] if r + 1 < PEER_TOPK else 0.0, lam1)
            rank2 = jnp.where(sv2[r:r + 1] > s2, float(r + 1), rank2)
        emit(hh, lam1, rank2, s1, s2, sv1[0:1], sv2[0:1], den)
        cov3 = jnp.sum(chosen, axis=0, keepdims=True)
        k = float(PEER_TOPK)
        tied = jnp.max(jnp.abs(cov1 - k) + jnp.abs(cov2 - k) + jnp.abs(cov3 - k)) > 0.0

        @pl.when(tied)
        def _(hh=hh):
            s1 = s_ref[2 * hh]
            s2 = s_ref[2 * hh + 1]
            rank1, sv1 = _top16_ranked(s1)
            rank2, sv2 = _top16_ranked(s2)
            cand0, idx = _pair_cells(sv1, sv2)
            lam, den = _lam_den(_pairs_exact(cand0, idx), cand0)
            lam1 = jnp.zeros_like(rank1)
            for r in range(PEER_TOPK):
                lam1 = jnp.where(rank1 == float(r), lam[r], lam1)
            emit(hh, lam1, rank2, s1, s2, sv1[0:1], sv2[0:1], den)


def _peer_select(x, g, wqt_bf, sk):
    n, d = x.shape
    n_heads = sk.shape[0] // 2
    nk = sk.shape[1]
    t = SEL_BLOCK
    assert n % t == 0 and nk == N_KEYS
    full = lambda shp: pl.BlockSpec(shp, lambda i: (0,) * len(shp))
    sel = pl.BlockSpec((n_heads, nk, t), lambda i: (0, 0, i))
    sel_f32 = jax.ShapeDtypeStruct((n_heads, nk, n), F32)
    sel_bf16 = jax.ShapeDtypeStruct((n_heads, nk, n), BF16)
    return pl.pallas_call(
        functools.partial(_peer_select_kernel, n_heads=n_heads),
        grid=(n // t,),
        in_specs=[pl.BlockSpec((t, d), lambda i: (i, 0)), full((1, d)), full(wqt_bf.shape), full(sk.shape)],
        out_specs=[pl.BlockSpec((t, d), lambda i: (i, 0)), sel, sel, sel, sel],
        out_shape=[jax.ShapeDtypeStruct((n, d), BF16), sel_f32, sel_bf16, sel_f32, sel_bf16],
        scratch_shapes=[pltpu.VMEM((2 * n_heads, nk, t), F32)],
        compiler_params=_cparams(("arbitrary",)),
        name="peer_select",
    )(x, g, wqt_bf, sk)


def _gelu(a):
    return 0.5 * a * (1.0 + lax.erf(a * (1.0 / math.sqrt(2.0))))


def _peer_dense_kernel(x_ref, hb_ref, u_ref, vt_ref, lam1_ref, rank2_ref, p1_ref, p2_ref, xo_ref,
                       acc_ref, a0_ref, a1_ref, z0_ref, z1_ref, *, n_heads, n_tiles):
    s = pl.program_id(0)
    n_items = pl.num_programs(0) - 2
    t = hb_ref.shape[0]
    nk = rank2_ref.shape[1]

    @pl.when(s == 0)
    def _():
        acc_ref[...] = jnp.zeros_like(acc_ref)
        a1_ref[...] = jnp.zeros_like(a1_ref)
        z0_ref[...] = jnp.zeros_like(z0_ref)

    eb = jnp.clip(s - 1, 0, n_items - 1) % n_tiles
    grp = pl.ds(pl.multiple_of(eb * SUBLANES, SUBLANES), SUBLANES)

    te = u_ref.shape[0]
    d = vt_ref.shape[0]
    n_tc = t // LANES

    def act_chunk(a_new, j):
        rows = slice(j * (te // MXU_CHUNKS), (j + 1) * (te // MXU_CHUNKS))
        a_new[rows, :] = lax.dot_general(u_ref[rows, :], hb_ref[...], (((1,), (1,)), ((), ())),
                                         preferred_element_type=F32)

    def val_chunk(z_old, j):
        rows = slice(j * (d // MXU_CHUNKS), (j + 1) * (d // MXU_CHUNKS))
        acc_ref[rows, :] += jnp.dot(vt_ref[rows, :], z_old[...], preferred_element_type=F32)

    def gate_chunk(a_cur, z_new, c):
        ii, tc = divmod(c, n_tc)
        lanes = slice(tc * LANES, (tc + 1) * LANES)
        rows = slice(ii * nk, (ii + 1) * nk)
        gate = jnp.zeros((nk, LANES), BF16)
        for hh in range(n_heads):
            l1 = jnp.broadcast_to(lam1_ref[hh, grp, lanes][ii:ii + 1], (nk, LANES)).astype(BF16)
            pb = jnp.broadcast_to(p1_ref[hh, grp, lanes][ii:ii + 1], (nk, LANES)).astype(BF16)
            gate = gate + jnp.where(l1 > rank2_ref[hh, :, lanes], pb * p2_ref[hh, :, lanes],
                                    jnp.zeros((), BF16))
        z_new[rows, lanes] = gate * _gelu(a_cur[rows, lanes]).astype(BF16)

    def step(a_new, a_cur, z_new, z_old):
        n_gate = SUBLANES * n_tc
        c = 0
        for k in range(2 * MXU_CHUNKS):
            if k % 2 == 0:
                act_chunk(a_new, k // 2)
            else:
                val_chunk(z_old, k // 2)
            while c < (k + 1) * n_gate // (2 * MXU_CHUNKS):
                gate_chunk(a_cur, z_new, c)
                c += 1

    @pl.when(s % 2 == 0)
    def _():
        step(a0_ref, a1_ref, z1_ref, z0_ref)

    @pl.when(s % 2 == 1)
    def _():
        step(a1_ref, a0_ref, z0_ref, z1_ref)

    @pl.when((s >= 2) & ((s - 2) % n_tiles == n_tiles - 1))
    def _():
        xo_ref[...] = x_ref[...] + acc_ref[...].T
        acc_ref[...] = jnp.zeros_like(acc_ref)


def _peer_dense(x, hb, u_all, vt_all, layer, lam1, rank2, p1, p2):
    n, d = x.shape
    n_exp = u_all.shape[1]
    n_heads, nk, _ = lam1.shape
    t = min(DENSE_TOKENS, n)
    te = DENSE_EXPERTS
    assert n % t == 0 and n_exp % te == 0 and te == SUBLANES * nk and n_exp == nk * nk and t % LANES == 0
    ne = n_exp // te
    n_items = (n // t) * ne
    assert ne >= 2
    item = lambda s, lag: jnp.clip(s - lag, 0, n_items - 1)
    tok = lambda lag: pl.BlockSpec((t, d), lambda s: (item(s, lag) // ne, 0))
    sel = pl.BlockSpec((n_heads, nk, t), lambda s: (0, 0, item(s, 1) // ne))
    return pl.pallas_call(
        functools.partial(_peer_dense_kernel, n_heads=n_heads, n_tiles=ne),
        grid=(n_items + 2,),
        in_specs=[tok(2), tok(0),
                  pl.BlockSpec((None, te, d), lambda s: (layer, item(s, 0) % ne, 0)),
                  pl.BlockSpec((None, d, te), lambda s: (layer, 0, item(s, 2) % ne)),
                  sel, sel, sel, sel],
        out_specs=tok(2),
        out_shape=jax.ShapeDtypeStruct((n, d), F32),
        scratch_shapes=[pltpu.VMEM((d, t), F32), pltpu.VMEM((te, t), F32), pltpu.VMEM((te, t), F32),
                        pltpu.VMEM((te, t), BF16), pltpu.VMEM((te, t), BF16)],
        compiler_params=_cparams(("arbitrary",)),
        name="peer_dense",
    )(x, hb, u_all, vt_all, lam1, rank2, p1, p2)


def _peer_tables_kernel(u_ref, v_ref, ub_ref, vt_ref):
    ub_ref[...] = u_ref[...].astype(BF16)
    vt_ref[...] = v_ref[...].T.astype(BF16)


def _peer_tables(peer_u, peer_v):
    n_l, n_exp, d = peer_u.shape
    te = ROW_BLOCK
    assert n_exp % te == 0
    src = pl.BlockSpec((None, te, d), lambda l, j: (l, j, 0))
    return pl.pallas_call(
        _peer_tables_kernel,
        grid=(n_l, n_exp // te),
        in_specs=[src, src],
        out_specs=[src, pl.BlockSpec((None, d, te), lambda l, j: (l, 0, j))],
        out_shape=[jax.ShapeDtypeStruct((n_l, n_exp, d), BF16), jax.ShapeDtypeStruct((n_l, d, n_exp), BF16)],
        compiler_params=_cparams(("arbitrary", "arbitrary")),
        name="peer_tables",
    )(peer_u, peer_v)


def _peer(x, g, wqt_bf, sk, u_all, vt_all, layer):
    n = x.shape[0]
    pad = (-n) % max(SEL_BLOCK, LANES)
    xp = jnp.pad(x, ((0, pad), (0, 0))) if pad else x
    hb, lam1, rank2, p1, p2 = _peer_select(xp, g, wqt_bf, sk)
    out = _peer_dense(xp, hb, u_all, vt_all, layer, lam1, rank2, p1, p2)
    return out[:n] if pad else out


def _final_norm_kernel(x_ref, g_ref, o_ref):
    o_ref[...] = _rms(x_ref[...], g_ref[...])


def _final_norm(x, g):
    n, d = x.shape
    tm = min(ROW_BLOCK, n)
    assert n % tm == 0
    return pl.pallas_call(
        _final_norm_kernel,
        grid=(n // tm,),
        in_specs=[pl.BlockSpec((tm, d), lambda i: (i, 0)), pl.BlockSpec((1, d), lambda i: (0, 0))],
        out_specs=pl.BlockSpec((tm, d), lambda i: (i, 0)),
        out_shape=jax.ShapeDtypeStruct((n, d), F32),
        compiler_params=_cparams(("arbitrary",)),
        name="final_norm",
    )(x, g)


def kernel(x_prompt, x_sample, state_conv, state_win_k, state_win_v, state_pool, norm_mix, w_in_ab, conv_w,
           w_out_ab, pool_w, pool_b, pool_scale, norm_ffn, peer_wq, peer_subkeys, peer_u, peer_v, norm_final):
    b, s, d = x_prompt.shape
    bs, ts, _ = x_sample.shape
    assert ts == 1
    depth = norm_mix.shape[0]
    d_conv = conv_w.shape[2]
    n_heads_b, hd = state_win_k.shape[3], state_win_k.shape[4]
    d_att = n_heads_b * hd
    wb_s = state_win_k.shape[2]
    wb_p = min(max(w for w, _ in DILATED_PAIRS), s)
    pool_state = state_pool.shape[2]
    q_scale = float(hd) ** -0.5
    n_peer_heads = peer_subkeys.shape[1]

    xp = x_prompt.reshape(b * s, d)
    xs = x_sample.reshape(bs, d)
    u_all, vt_all = _peer_tables(peer_u, peer_v)
    conv_p, conv_s, wk_p, wk_s, wv_p, wv_s, pool_p, pool_s = [], [], [], [], [], [], [], []
    for l in range(depth):
        g_mix = norm_mix[l][None]
        if l % 2 == 0:
            e = l // 2
            w_in = w_in_ab[e].astype(BF16)
            wa = w_out_ab[e, :d_conv].astype(BF16)
            wbm = w_out_ab[e, d_conv:].astype(BF16)
            ya, q, k, v, ul = _even_in_prompt(xp.reshape(b, s, d), g_mix, w_in, conv_w[e], d_conv, d_att, q_scale)
            branches = [_attn_branch(q, k, v, w, dl, n_heads_b) for w, dl in DILATED_PAIRS]
            o_list = [br[0].reshape(b * s, d_att) for br in branches]
            l_list = [br[1].reshape(b * s, d_att) for br in branches]
            xp = _out_proj(xp, ya.reshape(b * s, d_conv), o_list + l_list, wa, wbm)
            conv_p.append(ul[:, SUBLANES - 2:])
            wk_p.append(k[:, s - wb_p:].reshape(b, wb_p, n_heads_b, hd))
            wv_p.append(v[:, s - wb_p:].reshape(b, wb_p, n_heads_b, hd))
            cst = state_conv[e]
            ya_s, q_s, k_s, v_s, u_s = _even_in_sample(xs, g_mix, w_in, conv_w[e], cst[:, 1], cst[:, 0],
                                                       d_conv, d_att, q_scale)
            kst = state_win_k[e].reshape(bs, wb_s, d_att)
            vst = state_win_v[e].reshape(bs, wb_s, d_att)
            yb_s = _attn_sample(q_s, k_s, v_s, kst, vst, n_heads_b)
            xs = _out_proj(xs, ya_s, [yb_s], wa, wbm)
            conv_s.append(jnp.stack([cst[:, 1], u_s], axis=1))
            wk_s.append(jnp.concatenate([kst[:, 1:], k_s[:, None]], axis=1).reshape(bs, wb_s, n_heads_b, hd))
            wv_s.append(jnp.concatenate([vst[:, 1:], v_s[:, None]], axis=1).reshape(bs, wb_s, n_heads_b, hd))
        else:
            o = l // 2
            pw = pool_w[o].astype(BF16)
            xp3, hl = _pool_prompt(xp.reshape(b, s, d), g_mix, pw, pool_b[o], pool_scale[o][None])
            xp = xp3.reshape(b * s, d)
            pool_p.append(hl[:, hl.shape[1] - pool_state:])
            st = state_pool[o]
            xs, h_s = _pool_sample(xs, jnp.swapaxes(st, 0, 1), g_mix, pw, pool_b[o], pool_scale[o][None])
            pool_s.append(jnp.concatenate([st[:, 1:], h_s[:, None]], axis=1))
        g_ffn = norm_ffn[l][None]
        wqt = peer_wq[l].T.astype(BF16)
        sk = peer_subkeys[l].reshape(2 * n_peer_heads, N_KEYS, -1)
        sk_hi = sk.astype(BF16)
        sk_lo = (sk - sk_hi.astype(F32)).astype(BF16)
        sk = jnp.concatenate([sk_hi, sk_lo, sk_hi], axis=-1)
        xp = _peer(xp, g_ffn, wqt, sk, u_all, vt_all, l)
        xs = _peer(xs, g_ffn, wqt, sk, u_all, vt_all, l)
    gf = norm_final[None]
    y_prompt = _final_norm(xp, gf).reshape(b, s, d)
    y_sample = _final_norm(xs, gf).reshape(bs, ts, d)
    return (y_prompt, y_sample, jnp.stack(conv_p), jnp.stack(conv_s), jnp.stack(wk_p), jnp.stack(wk_s),
            jnp.stack(wv_p), jnp.stack(wv_s), jnp.stack(pool_p), jnp.stack(pool_s))
```

```python
import functools
import math

import jax
import jax.numpy as jnp
from jax import lax
from jax.experimental import pallas as pl
from jax.experimental.pallas import tpu as pltpu

F32 = jnp.float32
BF16 = jnp.bfloat16

NORM_EPS = 1e-6
NEG_INF = -1e30
DILATED_PAIRS = ((128, 1), (512, 4), (2048, 16))
POOL_WINDOWS = (2, 4, 8, 16)
PEER_TOPK = 16
N_KEYS = 128

LANES = 128
SUBLANES = 8
VMEM_LIMIT = 56 * 1024 * 1024

ATT_BLOCK = 128
ROW_BLOCK = 512
SEL_BLOCK = 128
DENSE_TOKENS = 256
MXU_CHUNKS = 4
DENSE_EXPERTS = SUBLANES * N_KEYS


def _cparams(sem, flags=None):
    return pltpu.CompilerParams(dimension_semantics=sem, vmem_limit_bytes=VMEM_LIMIT, flags=flags)


def _rms(x, g):
    r = lax.rsqrt(jnp.mean(x * x, axis=-1, keepdims=True) + NORM_EPS)
    return (x * r) * g


def _bdot(a, b):
    return jnp.dot(a.astype(BF16), b.astype(BF16), preferred_element_type=F32)


def _bdot_nt(a, b):
    return lax.dot_general(a.astype(BF16), b.astype(BF16), (((1,), (1,)), ((), ())),
                           preferred_element_type=F32)


def _even_in_body(x, g, w, cw, u1_fn, d_conv, d_att, q_scale):
    h = _rms(x, g)
    p = _bdot(h, w)
    gate_b = p[:, 0:d_conv]
    gate_c = p[:, d_conv:2 * d_conv]
    xv = p[:, 2 * d_conv:3 * d_conv]
    o = 3 * d_conv
    q = p[:, o:o + d_att] * q_scale
    k = p[:, o + d_att:o + 2 * d_att]
    v = p[:, o + 2 * d_att:o + 3 * d_att]
    u = gate_c * xv
    u1, u2 = u1_fn(u)
    y = cw[0:1] * u2 + cw[1:2] * u1 + cw[2:3] * u
    return gate_b * y, q, k, v, u


def _even_in_seq_kernel(x_ref, g_ref, w_ref, cw_ref, ya_ref, q_ref, k_ref, v_ref, ul_ref, *rest,
                        d_conv, d_att, q_scale):
    dil_refs, carry_ref, stage_ref = rest[:-2], rest[-2], rest[-1]
    s = pl.program_id(1)

    @pl.when(s == 0)
    def _():
        carry_ref[...] = jnp.zeros_like(carry_ref)

    prev = carry_ref[...]
    tm = x_ref.shape[0]

    def shifted(u):
        rows = lax.broadcasted_iota(jnp.int32, u.shape, 0)
        u1 = jnp.where(rows == 0, prev[7:8], pltpu.roll(u, 1, 0))
        u2 = pltpu.roll(u, 2, 0)
        u2 = jnp.where(rows == 0, prev[6:7], jnp.where(rows == 1, prev[7:8], u2))
        return u1, u2

    ya, q, k, v, u = _even_in_body(x_ref[...], g_ref[...], w_ref[...], cw_ref[...], shifted,
                                   d_conv, d_att, q_scale)
    ya_ref[...] = ya
    q_ref[...] = q
    k_ref[...] = k
    v_ref[...] = v
    last = u[tm - SUBLANES:tm]
    carry_ref[...] = last
    ul_ref[...] = last
    n_cb = d_att // LANES
    for j, val in enumerate((q, k, v)):
        for c in range(n_cb):
            stage_ref[j, c] = val[:, c * LANES:(c + 1) * LANES]
    for i, ref in enumerate(dil_refs):
        rows = ref.shape[0]
        dil = tm // rows
        for r in range(dil):
            for c in range(n_cb):
                col = r * d_att + c * LANES
                ref[:, col:col + LANES] = stage_ref[i % 3, c, pl.ds(r, rows, stride=dil), :]


def _even_in_rows_kernel(x_ref, g_ref, w_ref, cw_ref, u1_ref, u2_ref, ya_ref, q_ref, k_ref, v_ref, u_ref,
                         *, d_conv, d_att, q_scale):
    ya, q, k, v, u = _even_in_body(x_ref[...], g_ref[...], w_ref[...], cw_ref[...],
                                   lambda _: (u1_ref[...], u2_ref[...]), d_conv, d_att, q_scale)
    ya_ref[...] = ya
    q_ref[...] = q
    k_ref[...] = k
    v_ref[...] = v
    u_ref[...] = u


def _even_in_prompt(x, g, w_bf, cw, d_conv, d_att, q_scale):
    b, s, d = x.shape
    tm = min(ROW_BLOCK, s)
    assert s % tm == 0 and tm % SUBLANES == 0
    ncol = w_bf.shape[1]
    row = lambda c: pl.BlockSpec((None, tm, c), lambda i, j: (i, j, 0))
    full = lambda shp: pl.BlockSpec(shp, lambda i, j: (0,) * len(shp))
    dils = [dl for _, dl in DILATED_PAIRS if dl > 1]
    assert all(tm % (dl * SUBLANES) == 0 for dl in dils)
    dil_specs = [pl.BlockSpec((None, tm // dl, dl * d_att), lambda i, j: (i, j, 0)) for dl in dils for _ in range(3)]
    dil_shapes = [jax.ShapeDtypeStruct((b, s // dl, dl * d_att), F32) for dl in dils for _ in range(3)]
    outs = pl.pallas_call(
        functools.partial(_even_in_seq_kernel, d_conv=d_conv, d_att=d_att, q_scale=q_scale),
        grid=(b, s // tm),
        in_specs=[row(d), full((1, d)), full((d, ncol)), full((cw.shape[0], d_conv))],
        out_specs=[row(d_conv), row(d_att), row(d_att), row(d_att),
                   pl.BlockSpec((None, SUBLANES, d_conv), lambda i, j: (i, 0, 0))] + dil_specs,
        out_shape=[jax.ShapeDtypeStruct((b, s, d_conv), F32)] + [jax.ShapeDtypeStruct((b, s, d_att), F32)] * 3
                  + [jax.ShapeDtypeStruct((b, SUBLANES, d_conv), F32)] + dil_shapes,
        scratch_shapes=[pltpu.VMEM((SUBLANES, d_conv), F32), pltpu.VMEM((3, d_att // LANES, tm, LANES), F32)],
        compiler_params=_cparams(("arbitrary", "arbitrary")),
        name="even_in_prompt",
    )(x, g, w_bf, cw)
    ya, q, k, v, ul = outs[:5]
    qkv = {1: (q, k, v)}
    for i, dl in enumerate(dils):
        qkv[dl] = tuple(outs[5 + 3 * i:8 + 3 * i])
    return ya, k, v, ul, qkv


def _even_in_sample(x, g, w_bf, cw, u1, u2, d_conv, d_att, q_scale):
    n, d = x.shape
    ncol = w_bf.shape[1]
    full = lambda shp: pl.BlockSpec(shp, lambda i: (0,) * len(shp))
    return pl.pallas_call(
        functools.partial(_even_in_rows_kernel, d_conv=d_conv, d_att=d_att, q_scale=q_scale),
        grid=(1,),
        in_specs=[full((n, d)), full((1, d)), full((d, ncol)), full((cw.shape[0], d_conv)),
                  full((n, d_conv)), full((n, d_conv))],
        out_specs=[full((n, d_conv)), full((n, d_att)), full((n, d_att)), full((n, d_att)), full((n, d_conv))],
        out_shape=[jax.ShapeDtypeStruct((n, d_conv), F32)] + [jax.ShapeDtypeStruct((n, d_att), F32)] * 3
                  + [jax.ShapeDtypeStruct((n, d_conv), F32)],
        compiler_params=_cparams(("arbitrary",)),
        name="even_in_sample",
    )(x, g, w_bf, cw, u1, u2)


def _attn_branch_kernel(q_ref, kp_ref, kc_ref, vp_ref, vc_ref, o_ref, l_ref, *, n_heads, hd, n_back):
    n = pl.program_id(2)
    blk = q_ref.shape[0]
    q = q_ref[...]
    k = jnp.concatenate([kp_ref[...], kc_ref[...]], axis=0)
    v = jnp.concatenate([vp_ref[...], vc_ref[...]], axis=0)
    qi = lax.broadcasted_iota(jnp.int32, (blk, 2 * blk), 0)
    ki = lax.broadcasted_iota(jnp.int32, (blk, 2 * blk), 1)
    dist = qi + blk - ki
    has_prev = jnp.where(n > 0, 0, blk)
    mask = (dist >= 0) & (dist <= n_back) & (ki >= has_prev)
    o_parts, l_parts = [], []
    for h in range(n_heads):
        sl = slice(h * hd, (h + 1) * hd)
        s = _bdot_nt(q[:, sl], k[:, sl])
        s = jnp.where(mask, s, NEG_INF)
        m = jnp.max(s, axis=-1, keepdims=True)
        p = jnp.exp(s - m)
        den = jnp.sum(p, axis=-1, keepdims=True)
        o_parts.append(_bdot(p, v[:, sl]) / den)
        l_parts.append(jnp.broadcast_to(m + jnp.log(den), (blk, hd)))
    o_ref[...] = jnp.concatenate(o_parts, axis=1)
    l_ref[...] = jnp.concatenate(l_parts, axis=1)


def _attn_branch(q, k, v, window, dil, n_heads):
    b, L, dda = q.shape
    da = dda // dil
    hd = da // n_heads
    n_back = window // dil
    assert L % ATT_BLOCK == 0 and n_back <= ATT_BLOCK
    nb = L // ATT_BLOCK
    cur = pl.BlockSpec((None, ATT_BLOCK, da), lambda i, r, n: (i, n, r))
    prev = pl.BlockSpec((None, ATT_BLOCK, da), lambda i, r, n: (i, jnp.maximum(n - 1, 0), r))
    o, l = pl.pallas_call(
        functools.partial(_attn_branch_kernel, n_heads=n_heads, hd=hd, n_back=n_back),
        grid=(b, dil, nb),
        in_specs=[cur, prev, cur, prev, cur],
        out_specs=[cur, cur],
        out_shape=[jax.ShapeDtypeStruct((b, L, dil * da), F32)] * 2,
        compiler_params=_cparams(("arbitrary", "arbitrary", "arbitrary")),
        name=f"attn_branch_d{dil}",
    )(q, k, k, v, v)
    return o, l


def _attn_sample_kernel(q_ref, kn_ref, vn_ref, *refs):
    n_br = (len(refs) - 1) // 2
    k_refs, v_refs, y_ref = refs[:n_br], refs[n_br:2 * n_br], refs[-1]
    q = q_ref[...][None]
    kn = kn_ref[...][None]
    vn = vn_ref[...][None]
    s0 = jnp.sum(kn * q, axis=-1, keepdims=True)
    o_list, l_list = [], []
    for g in range(n_br):
        s = jnp.sum(k_refs[g][...] * q, axis=-1, keepdims=True)
        m = jnp.maximum(jnp.max(s, axis=0, keepdims=True), s0)
        p = jnp.exp(s - m)
        p0 = jnp.exp(s0 - m)
        den = jnp.sum(p, axis=0, keepdims=True) + p0
        o_list.append((jnp.sum(p * v_refs[g][...], axis=0, keepdims=True) + p0 * vn) / den)
        l_list.append(m + jnp.log(den))
    y_ref[...] = _merge_branches(o_list, l_list)[0]


def _merge_branches(o_list, l_list):
    m = functools.reduce(jnp.maximum, l_list)
    e = [jnp.exp(l - m) for l in l_list]
    num = functools.reduce(lambda a, b: a + b, [ei * oi for ei, oi in zip(e, o_list)])
    return num / functools.reduce(lambda a, b: a + b, e)


def _attn_sample(q, k_new, v_new, k_state_all, v_state_all, layer):
    b, da = q.shape
    n_l, _, wb, n_heads, hd = k_state_all.shape
    row = pl.BlockSpec((None, n_heads, hd), lambda i: (i, 0, 0))
    ins, specs = [], []
    for st in (k_state_all, v_state_all):
        for window, dil in DILATED_PAIRS:
            n_back = window // dil
            assert n_back * dil <= wb and wb % dil == 0 and (wb // dil) % n_back == 0
            L = wb // dil
            ins.append(st.reshape(n_l, b, L, dil, n_heads, hd))
            specs.append(pl.BlockSpec((None, None, n_back, None, n_heads, hd),
                                      lambda i, L=L, nbk=n_back: (layer, i, L // nbk - 1, 0, 0, 0)))
    as_heads = lambda t: t.reshape(b, n_heads, hd)
    y = pl.pallas_call(
        _attn_sample_kernel,
        grid=(b,),
        in_specs=[row, row, row] + specs,
        out_specs=row,
        out_shape=jax.ShapeDtypeStruct((b, n_heads, hd), F32),
        compiler_params=_cparams(("arbitrary",)),
        name="attn_sample",
    )(as_heads(q), as_heads(k_new), as_heads(v_new), *ins)
    return y.reshape(b, da)


def _out_proj_kernel(x_ref, ya_ref, *refs, n_in):
    br = refs[:n_in]
    wa_ref, wb_ref, xo_ref = refs[n_in:n_in + 3]
    scratch = list(refs[n_in + 3:])
    tm, da = ya_ref.shape[0], wb_ref.shape[0]

    def token_order(ref):
        rows = ref.shape[0]
        if rows == tm:
            return ref[...]
        dil = tm // rows
        sc = scratch.pop(0)
        n_cb = da // LANES
        for r in range(dil):
            for c in range(n_cb):
                col = r * da + c * LANES
                sc[c, pl.ds(r, rows, stride=dil), :] = ref[:, col:col + LANES]
        return jnp.concatenate([sc[c] for c in range(n_cb)], axis=1)

    vals = [token_order(r) for r in br]
    yb = vals[0] if n_in == 1 else _merge_branches(vals[:n_in // 2], vals[n_in // 2:])
    xo_ref[...] = x_ref[...] + _bdot(ya_ref[...], wa_ref[...]) + _bdot(yb, wb_ref[...])


def _out_proj(x, ya, branch_arrays, wa_bf, wb_bf):
    n, d = x.shape
    da = wb_bf.shape[0]
    tm = min(ROW_BLOCK, n)
    assert n % tm == 0
    row = lambda c: pl.BlockSpec((tm, c), lambda i: (i, 0))
    full = lambda shp: pl.BlockSpec(shp, lambda i: (0,) * len(shp))
    dils = [a.shape[1] // da for a in branch_arrays]
    assert all(a.shape == (n // dl, dl * da) and (dl == 1 or tm % (dl * SUBLANES) == 0)
               for a, dl in zip(branch_arrays, dils))
    return pl.pallas_call(
        functools.partial(_out_proj_kernel, n_in=len(branch_arrays)),
        grid=(n // tm,),
        in_specs=[row(d), row(ya.shape[1])]
                 + [pl.BlockSpec((tm // dl, dl * da), lambda i: (i, 0)) for dl in dils]
                 + [full(wa_bf.shape), full(wb_bf.shape)],
        out_specs=row(d),
        out_shape=jax.ShapeDtypeStruct((n, d), F32),
        scratch_shapes=[pltpu.VMEM((da // LANES, tm, LANES), F32) for dl in dils if dl > 1],
        compiler_params=_cparams(("arbitrary",)),
        name="out_proj",
    )(x, ya, *branch_arrays, wa_bf, wb_bf)


def _pool_groups(dmat, w_ref, b_ref, sc):
    n_g = w_ref.shape[0]
    gw = w_ref.shape[1]
    ys = [_bdot(dmat[g], w_ref[g]) + b_ref[g:g + 1] for g in range(n_g)]
    return jnp.concatenate(ys, axis=1) * sc


def _pool_prompt_kernel(x_ref, g_ref, w_ref, b_ref, sc_ref, xo_ref, hl_ref, carry_ref):
    s = pl.program_id(1)
    hist = carry_ref.shape[0]

    @pl.when(s == 0)
    def _():
        carry_ref[...] = jnp.zeros_like(carry_ref)

    x = x_ref[...]
    tm = x.shape[0]
    h = _rms(x, g_ref[...])
    ext = jnp.concatenate([carry_ref[...], h], axis=0)
    pos = s * tm + lax.broadcasted_iota(jnp.int32, (tm, 1), 0) + 1
    gw = w_ref.shape[1]
    acc = ext
    width = 1
    diffs = []
    for g, w in enumerate(POOL_WINDOWS):
        while width < w:
            acc = acc + pltpu.roll(acc, width, 0)
            width *= 2
        cols = slice(g * gw, (g + 1) * gw)
        win = acc[hist:, cols]
        div = jnp.minimum(pos, w).astype(F32)
        diffs.append(win / div - h[:, cols])
    xo_ref[...] = x + _pool_groups(diffs, w_ref, b_ref, sc_ref[...])
    last = ext[tm:tm + hist]
    carry_ref[...] = last
    hl_ref[...] = last


def _pool_prompt(x, g, w_bf, bias, scale):
    b, s, d = x.shape
    tm = min(ROW_BLOCK, s)
    hist = 16
    assert s % tm == 0 and tm >= hist and max(POOL_WINDOWS) <= hist
    assert all(w == 2 ** (i + 1) for i, w in enumerate(POOL_WINDOWS))
    row = pl.BlockSpec((None, tm, d), lambda i, j: (i, j, 0))
    full = lambda shp: pl.BlockSpec(shp, lambda i, j: (0,) * len(shp))
    return pl.pallas_call(
        _pool_prompt_kernel,
        grid=(b, s // tm),
        in_specs=[row, full((1, d)), full(w_bf.shape), full(bias.shape), full((1, d))],
        out_specs=[row, pl.BlockSpec((None, hist, d), lambda i, j: (i, 0, 0))],
        out_shape=[jax.ShapeDtypeStruct((b, s, d), F32), jax.ShapeDtypeStruct((b, hist, d), F32)],
        scratch_shapes=[pltpu.VMEM((hist, d), F32)],
        compiler_params=_cparams(("arbitrary", "arbitrary")),
        name="pool_prompt",
    )(x, g, w_bf, bias, scale)


def _pool_sample_kernel(x_ref, st_ref, g_ref, w_ref, b_ref, sc_ref, xo_ref, h_ref):
    x = x_ref[...]
    h = _rms(x, g_ref[...])
    n_st = st_ref.shape[0]
    gw = w_ref.shape[1]
    diffs = []
    for g, w in enumerate(POOL_WINDOWS):
        cols = slice(g * gw, (g + 1) * gw)
        tot = h[:, cols]
        for j in range(1, w):
            tot = tot + st_ref[n_st - j][:, cols]
        diffs.append(tot / float(w) - h[:, cols])
    xo_ref[...] = x + _pool_groups(diffs, w_ref, b_ref, sc_ref[...])
    h_ref[...] = h


def _pool_sample(x, state_t, g, w_bf, bias, scale):
    n, d = x.shape
    assert state_t.shape[0] + 1 >= max(POOL_WINDOWS)
    full = lambda shp: pl.BlockSpec(shp, lambda i: (0,) * len(shp))
    return pl.pallas_call(
        _pool_sample_kernel,
        grid=(1,),
        in_specs=[full(x.shape), full(state_t.shape), full((1, d)), full(w_bf.shape), full(bias.shape),
                  full((1, d))],
        out_specs=[full(x.shape), full(x.shape)],
        out_shape=[jax.ShapeDtypeStruct((n, d), F32)] * 2,
        compiler_params=_cparams(("arbitrary",)),
        name="pool_sample",
    )(x, state_t, g, w_bf, bias, scale)


def _top16_ranked(s):
    n, t = s.shape
    rows = lax.broadcasted_iota(jnp.int32, (n, t), 0).astype(F32)
    rows16 = lax.broadcasted_iota(jnp.int32, (PEER_TOPK, t), 0)

    def step(it, carry):
        s, rank, sv = carry
        m = jnp.max(s, axis=0, keepdims=True)
        first = jnp.min(jnp.where(s == m, rows, float(n)), axis=0, keepdims=True)
        sel = rows == first
        itf = jnp.asarray(it, jnp.int32).astype(F32)
        return (jnp.where(sel, -jnp.inf, s), jnp.where(sel, itf, rank), jnp.where(rows16 == it, m, sv))

    init = (s, jnp.full((n, t), float(PEER_TOPK), F32), jnp.zeros((PEER_TOPK, t), F32))
    _, rank, sv = lax.fori_loop(0, PEER_TOPK, step, init)
    return rank, sv


def _top16_values(sa, sb):
    t = sa.shape[1]
    rows16 = lax.broadcasted_iota(jnp.int32, (PEER_TOPK, t), 0)

    def step(it, carry):
        a, b, sva, svb = carry
        ma = jnp.max(a, axis=0, keepdims=True)
        mb = jnp.max(b, axis=0, keepdims=True)
        return (jnp.where(a == ma, -jnp.inf, a), jnp.where(b == mb, -jnp.inf, b),
                jnp.where(rows16 == it, ma, sva), jnp.where(rows16 == it, mb, svb))

    zero = jnp.zeros((PEER_TOPK, t), F32)
    a, b, sva, svb = lax.fori_loop(0, PEER_TOPK, step, (sa, sb, zero, zero))
    cover = lambda x: jnp.sum(jnp.where(x == -jnp.inf, 1.0, 0.0), axis=0, keepdims=True)
    return sva, svb, cover(a), cover(b)


def _pair_cells(sv1, sv2):
    t = sv1.shape[1]
    half = SUBLANES
    tiles = [sv1[0:1] + sv2[0:half], sv1[0:1] + sv2[half:2 * half]]
    tiles += [sv1[r:r + 1] + sv2[0:half] for r in range(1, half)]
    tiles += [sv1[half:2 * half] + sv2[0:1]]
    cand0 = jnp.concatenate(tiles, axis=0)
    nrow = cand0.shape[0]
    i = lax.broadcasted_iota(jnp.int32, (nrow, t), 0)
    j = i - 2 * half
    mid = (lax.shift_right_arithmetic(j, 3) + 1) * PEER_TOPK + lax.bitwise_and(j, half - 1)
    idx = jnp.where(i < 2 * half, i,
                    jnp.where(i < nrow - half, mid, (i - (nrow - half) + half) * PEER_TOPK)).astype(F32)
    return cand0, idx


def _pairs_exact(cand0, idx):
    def step(_, carry):
        cand, chosen = carry
        m = jnp.max(cand, axis=0, keepdims=True)
        first = jnp.min(jnp.where(cand == m, idx, 1e9), axis=0, keepdims=True)
        sel = idx == first
        return jnp.where(sel, -jnp.inf, cand), jnp.where(sel, 1.0, chosen)

    _, chosen = lax.fori_loop(0, PEER_TOPK, step, (cand0, jnp.zeros_like(cand0)))
    return chosen


def _pairs_fast(cand0):
    def step(_, cand):
        return jnp.where(cand == jnp.max(cand, axis=0, keepdims=True), -jnp.inf, cand)

    cand = lax.fori_loop(0, PEER_TOPK, step, cand0)
    return jnp.where(cand == -jnp.inf, 1.0, 0.0)


def _lam_den(chosen, cand0):
    half = SUBLANES
    nrow = cand0.shape[0]
    den = jnp.sum(jnp.where(chosen > 0, jnp.exp(cand0 - cand0[0:1]), 0.0), axis=0, keepdims=True)
    lam = [jnp.sum(chosen[0:2 * half], axis=0, keepdims=True)]
    lam += [jnp.sum(chosen[2 * half + half * (r - 1):2 * half + half * r], axis=0, keepdims=True)
            for r in range(1, half)]
    lam += [chosen[nrow - half + r:nrow - half + r + 1] for r in range(half)]
    return lam, den


def _peer_select_kernel(x_ref, g_ref, wqt_ref, sk_ref, hb_ref, lam1_ref, rank2_ref, p1_ref, p2_ref, s_ref,
                        *, n_heads):
    h = _rms(x_ref[...], g_ref[...])
    hb = h.astype(BF16)
    hb_ref[...] = hb
    nk = sk_ref.shape[1]
    qt = lax.dot_general(wqt_ref[...], hb, (((1,), (1,)), ((), ())), preferred_element_type=F32)
    q_hi = qt.astype(BF16)
    q_lo = (qt - q_hi.astype(F32)).astype(BF16)
    for c in range(2 * n_heads):
        rows = slice(c * nk, (c + 1) * nk)
        q3 = jnp.concatenate([q_hi[rows], q_hi[rows], q_lo[rows]], axis=0)
        s_ref[c] = jnp.dot(sk_ref[c], q3, preferred_element_type=F32)

    def emit(hh, lam1, rank2, s1, s2, top1, top2, den):
        lam1_ref[hh] = lam1
        rank2_ref[hh] = rank2.astype(rank2_ref.dtype)
        p1_ref[hh] = jnp.exp(s1 - top1) / (2.0 * den)
        p2_ref[hh] = jnp.exp(s2 - top2).astype(p2_ref.dtype)

    for hh in range(n_heads):
        s1 = s_ref[2 * hh]
        s2 = s_ref[2 * hh + 1]
        sv1, sv2, cov1, cov2 = _top16_values(s1, s2)
        cand0, _ = _pair_cells(sv1, sv2)
        chosen = _pairs_fast(cand0)
        lam, den = _lam_den(chosen, cand0)
        lam1 = jnp.broadcast_to(lam[0], s1.shape)
        rank2 = jnp.zeros_like(s2)
        for r in range(PEER_TOPK):
            lam1 = jnp.where(sv1[r:r + 1] > s1, lam[r + 1] if r + 1 < PEER_TOPK else 0.0, lam1)
            rank2 = jnp.where(sv2[r:r + 1] > s2, float(r + 1), rank2)
        emit(hh, lam1, rank2, s1, s2, sv1[0:1], sv2[0:1], den)
        cov3 = jnp.sum(chosen, axis=0, keepdims=True)
        k = float(PEER_TOPK)
        tied = jnp.max(jnp.abs(cov1 - k) + jnp.abs(cov2 - k) + jnp.abs(cov3 - k)) > 0.0

        @pl.when(tied)
        def _(hh=hh):
            s1 = s_ref[2 * hh]
            s2 = s_ref[2 * hh + 1]
            rank1, sv1 = _top16_ranked(s1)
            rank2, sv2 = _top16_ranked(s2)
            cand0, idx = _pair_cells(sv1, sv2)
            lam, den = _lam_den(_pairs_exact(cand0, idx), cand0)
            lam1 = jnp.zeros_like(rank1)
            for r in range(PEER_TOPK):
                lam1 = jnp.where(rank1 == float(r), lam[r], lam1)
            emit(hh, lam1, rank2, s1, s2, sv1[0:1], sv2[0:1], den)


def _peer_select(x, g, wqt_bf, sk):
    n, d = x.shape
    n_heads = sk.shape[0] // 2
    nk = sk.shape[1]
    t = SEL_BLOCK
    assert n % t == 0 and nk == N_KEYS
    full = lambda shp: pl.BlockSpec(shp, lambda i: (0,) * len(shp))
    sel = pl.BlockSpec((n_heads, nk, t), lambda i: (0, 0, i))
    sel_f32 = jax.ShapeDtypeStruct((n_heads, nk, n), F32)
    sel_bf16 = jax.ShapeDtypeStruct((n_heads, nk, n), BF16)
    return pl.pallas_call(
        functools.partial(_peer_select_kernel, n_heads=n_heads),
        grid=(n // t,),
        in_specs=[pl.BlockSpec((t, d), lambda i: (i, 0)), full((1, d)), full(wqt_bf.shape), full(sk.shape)],
        out_specs=[pl.BlockSpec((t, d), lambda i: (i, 0)), sel, sel, sel, sel],
        out_shape=[jax.ShapeDtypeStruct((n, d), BF16), sel_f32, sel_bf16, sel_f32, sel_bf16],
        scratch_shapes=[pltpu.VMEM((2 * n_heads, nk, t), F32)],
        compiler_params=_cparams(("arbitrary",)),
        name="peer_select",
    )(x, g, wqt_bf, sk)


def _gelu_x2(a):
    return a * (1.0 + lax.erf(a * (1.0 / math.sqrt(2.0))))


def _peer_dense_kernel(x_ref, hb_ref, u_ref, vt_ref, lam1_ref, rank2_ref, p1_ref, p2_ref, xo_ref,
                       acc_ref, a0_ref, a1_ref, z0_ref, z1_ref, *, n_heads, n_tiles):
    s = pl.program_id(0)
    n_items = pl.num_programs(0) - 2
    t = hb_ref.shape[0]
    nk = rank2_ref.shape[1]

    @pl.when(s == 0)
    def _():
        acc_ref[...] = jnp.zeros_like(acc_ref)
        a1_ref[...] = jnp.zeros_like(a1_ref)
        z0_ref[...] = jnp.zeros_like(z0_ref)

    eb = jnp.clip(s - 1, 0, n_items - 1) % n_tiles
    grp = pl.ds(pl.multiple_of(eb * SUBLANES, SUBLANES), SUBLANES)

    te = u_ref.shape[0]
    d = vt_ref.shape[0]
    n_tc = t // LANES

    def act_chunk(a_new, j):
        rows = slice(j * (te // MXU_CHUNKS), (j + 1) * (te // MXU_CHUNKS))
        a_new[rows, :] = lax.dot_general(u_ref[rows, :], hb_ref[...], (((1,), (1,)), ((), ())),
                                         preferred_element_type=F32)

    def val_chunk(z_old, j):
        rows = slice(j * (d // MXU_CHUNKS), (j + 1) * (d // MXU_CHUNKS))
        acc_ref[rows, :] += jnp.dot(vt_ref[rows, :], z_old[...], preferred_element_type=F32)

    def gate_chunk(a_cur, z_new, c):
        ii, tc = divmod(c, n_tc)
        lanes = slice(tc * LANES, (tc + 1) * LANES)
        rows = slice(ii * nk, (ii + 1) * nk)
        gate = jnp.zeros((nk, LANES), BF16)
        for hh in range(n_heads):
            l1 = jnp.broadcast_to(lam1_ref[hh, grp, lanes][ii:ii + 1], (nk, LANES)).astype(BF16)
            pb = jnp.broadcast_to(p1_ref[hh, grp, lanes][ii:ii + 1], (nk, LANES)).astype(BF16)
            gate = gate + jnp.where(l1 > rank2_ref[hh, :, lanes], pb * p2_ref[hh, :, lanes],
                                    jnp.zeros((), BF16))
        z_new[rows, lanes] = gate * _gelu_x2(a_cur[rows, lanes]).astype(BF16)

    def step(a_new, a_cur, z_new, z_old):
        n_gate = SUBLANES * n_tc
        c = 0
        for k in range(2 * MXU_CHUNKS):
            if k % 2 == 0:
                act_chunk(a_new, k // 2)
            else:
                val_chunk(z_old, k // 2)
            while c < (k + 1) * n_gate // (2 * MXU_CHUNKS):
                gate_chunk(a_cur, z_new, c)
                c += 1

    @pl.when(s % 2 == 0)
    def _():
        step(a0_ref, a1_ref, z1_ref, z0_ref)

    @pl.when(s % 2 == 1)
    def _():
        step(a1_ref, a0_ref, z0_ref, z1_ref)

    @pl.when((s >= 2) & ((s - 2) % n_tiles == n_tiles - 1))
    def _():
        xo_ref[...] = x_ref[...] + acc_ref[...].T
        acc_ref[...] = jnp.zeros_like(acc_ref)


def _peer_dense(x, hb, u_all, vt_all, layer, lam1, rank2, p1, p2):
    n, d = x.shape
    n_exp = u_all.shape[1]
    n_heads, nk, _ = lam1.shape
    t = min(DENSE_TOKENS, n)
    te = DENSE_EXPERTS
    assert n % t == 0 and n_exp % te == 0 and te == SUBLANES * nk and n_exp == nk * nk and t % LANES == 0
    ne = n_exp // te
    n_items = (n // t) * ne
    assert ne >= 2
    item = lambda s, lag: jnp.clip(s - lag, 0, n_items - 1)
    tok = lambda lag: pl.BlockSpec((t, d), lambda s: (item(s, lag) // ne, 0))
    sel = pl.BlockSpec((n_heads, nk, t), lambda s: (0, 0, item(s, 1) // ne))
    return pl.pallas_call(
        functools.partial(_peer_dense_kernel, n_heads=n_heads, n_tiles=ne),
        grid=(n_items + 2,),
        in_specs=[tok(2), tok(0),
                  pl.BlockSpec((None, te, d), lambda s: (layer, item(s, 0) % ne, 0)),
                  pl.BlockSpec((None, d, te), lambda s: (layer, 0, item(s, 2) % ne)),
                  sel, sel, sel, sel],
        out_specs=tok(2),
        out_shape=jax.ShapeDtypeStruct((n, d), F32),
        scratch_shapes=[pltpu.VMEM((d, t), F32), pltpu.VMEM((te, t), F32), pltpu.VMEM((te, t), F32),
                        pltpu.VMEM((te, t), BF16), pltpu.VMEM((te, t), BF16)],
        compiler_params=_cparams(("arbitrary",)),
        name="peer_dense",
    )(x, hb, u_all, vt_all, lam1, rank2, p1, p2)


def _peer_tables_kernel(u_ref, v_ref, ub_ref, vt_ref):
    ub_ref[...] = u_ref[...].astype(BF16)
    vt_ref[...] = v_ref[...].T.astype(BF16)


def _peer_tables(peer_u, peer_v):
    n_l, n_exp, d = peer_u.shape
    te = ROW_BLOCK
    assert n_exp % te == 0
    src = pl.BlockSpec((None, te, d), lambda l, j: (l, j, 0))
    return pl.pallas_call(
        _peer_tables_kernel,
        grid=(n_l, n_exp // te),
        in_specs=[src, src],
        out_specs=[src, pl.BlockSpec((None, d, te), lambda l, j: (l, 0, j))],
        out_shape=[jax.ShapeDtypeStruct((n_l, n_exp, d), BF16), jax.ShapeDtypeStruct((n_l, d, n_exp), BF16)],
        compiler_params=_cparams(("arbitrary", "arbitrary")),
        name="peer_tables",
    )(peer_u, peer_v)


def _peer(x, g, wqt_bf, sk, u_all, vt_all, layer):
    n = x.shape[0]
    pad = (-n) % max(SEL_BLOCK, LANES)
    xp = jnp.pad(x, ((0, pad), (0, 0))) if pad else x
    hb, lam1, rank2, p1, p2 = _peer_select(xp, g, wqt_bf, sk)
    out = _peer_dense(xp, hb, u_all, vt_all, layer, lam1, rank2, p1, p2)
    return out[:n] if pad else out


def _final_norm_kernel(x_ref, g_ref, o_ref):
    o_ref[...] = _rms(x_ref[...], g_ref[...])


def _final_norm(x, g):
    n, d = x.shape
    tm = min(ROW_BLOCK, n)
    assert n % tm == 0
    return pl.pallas_call(
        _final_norm_kernel,
        grid=(n // tm,),
        in_specs=[pl.BlockSpec((tm, d), lambda i: (i, 0)), pl.BlockSpec((1, d), lambda i: (0, 0))],
        out_specs=pl.BlockSpec((tm, d), lambda i: (i, 0)),
        out_shape=jax.ShapeDtypeStruct((n, d), F32),
        compiler_params=_cparams(("arbitrary",)),
        name="final_norm",
    )(x, g)


def kernel(x_prompt, x_sample, state_conv, state_win_k, state_win_v, state_pool, norm_mix, w_in_ab, conv_w,
           w_out_ab, pool_w, pool_b, pool_scale, norm_ffn, peer_wq, peer_subkeys, peer_u, peer_v, norm_final):
    b, s, d = x_prompt.shape
    bs, ts, _ = x_sample.shape
    assert ts == 1
    depth = norm_mix.shape[0]
    d_conv = conv_w.shape[2]
    n_heads_b, hd = state_win_k.shape[3], state_win_k.shape[4]
    d_att = n_heads_b * hd
    wb_s = state_win_k.shape[2]
    wb_p = min(max(w for w, _ in DILATED_PAIRS), s)
    pool_state = state_pool.shape[2]
    q_scale = float(hd) ** -0.5
    n_peer_heads = peer_subkeys.shape[1]

    xp = x_prompt.reshape(b * s, d)
    xs = x_sample.reshape(bs, d)
    u_all, vt_all = _peer_tables(peer_u, peer_v)
    conv_p, conv_s, wk_p, wk_s, wv_p, wv_s, pool_p, pool_s = [], [], [], [], [], [], [], []
    for l in range(depth):
        g_mix = norm_mix[l][None]
        if l % 2 == 0:
            e = l // 2
            w_in = w_in_ab[e].astype(BF16)
            wa = w_out_ab[e, :d_conv].astype(BF16)
            wbm = w_out_ab[e, d_conv:].astype(BF16)
            ya, k, v, ul, qkv = _even_in_prompt(xp.reshape(b, s, d), g_mix, w_in, conv_w[e], d_conv, d_att, q_scale)
            branches = [_attn_branch(*qkv[dl], w, dl, n_heads_b) for w, dl in DILATED_PAIRS]
            o_list = [br[0].reshape(b * s // dl, dl * d_att) for br, (_, dl) in zip(branches, DILATED_PAIRS)]
            l_list = [br[1].reshape(b * s // dl, dl * d_att) for br, (_, dl) in zip(branches, DILATED_PAIRS)]
            xp = _out_proj(xp, ya.reshape(b * s, d_conv), o_list + l_list, wa, wbm)
            conv_p.append(ul[:, SUBLANES - 2:])
            wk_p.append(k[:, s - wb_p:].reshape(b, wb_p, n_heads_b, hd))
            wv_p.append(v[:, s - wb_p:].reshape(b, wb_p, n_heads_b, hd))
            cst = state_conv[e]
            ya_s, q_s, k_s, v_s, u_s = _even_in_sample(xs, g_mix, w_in, conv_w[e], cst[:, 1], cst[:, 0],
                                                       d_conv, d_att, q_scale)
            yb_s = _attn_sample(q_s, k_s, v_s, state_win_k, state_win_v, e)
            xs = _out_proj(xs, ya_s, [yb_s], wa, wbm)
            conv_s.append(jnp.stack([cst[:, 1], u_s], axis=1))
            wk_s.append(k_s.reshape(bs, 1, n_heads_b, hd))
            wv_s.append(v_s.reshape(bs, 1, n_heads_b, hd))
        else:
            o = l // 2
            pw = pool_w[o].astype(BF16)
            xp3, hl = _pool_prompt(xp.reshape(b, s, d), g_mix, pw, pool_b[o], pool_scale[o][None])
            xp = xp3.reshape(b * s, d)
            pool_p.append(hl[:, hl.shape[1] - pool_state:])
            st = state_pool[o]
            xs, h_s = _pool_sample(xs, jnp.swapaxes(st, 0, 1), g_mix, pw, pool_b[o], pool_scale[o][None])
            pool_s.append(jnp.concatenate([st[:, 1:], h_s[:, None]], axis=1))
        g_ffn = norm_ffn[l][None]
        wqt = peer_wq[l].T.astype(BF16)
        sk = peer_subkeys[l].reshape(2 * n_peer_heads, N_KEYS, -1)
        sk_hi = sk.astype(BF16)
        sk_lo = (sk - sk_hi.astype(F32)).astype(BF16)
        sk = jnp.concatenate([sk_hi, sk_lo, sk_hi], axis=-1)
        xp = _peer(xp, g_ffn, wqt, sk, u_all, vt_all, l)
        xs = _peer(xs, g_ffn, wqt, sk, u_all, vt_all, l)
    gf = norm_final[None]
    y_prompt = _final_norm(xp, gf).reshape(b, s, d)
    y_sample = _final_norm(xs, gf).reshape(bs, ts, d)
    win_k_s = jnp.concatenate([state_win_k[:, :, 1:], jnp.stack(wk_s)], axis=2)
    win_v_s = jnp.concatenate([state_win_v[:, :, 1:], jnp.stack(wv_s)], axis=2)
    return (y_prompt, y_sample, jnp.stack(conv_p), jnp.stack(conv_s), jnp.stack(wk_p), win_k_s,
            jnp.stack(wv_p), win_v_s, jnp.stack(pool_p), jnp.stack(pool_s))
```

```python
import functools
import math

import jax
import jax.numpy as jnp
from jax import lax
from jax.experimental import pallas as pl
from jax.experimental.pallas import tpu as pltpu

F32 = jnp.float32
BF16 = jnp.bfloat16

NORM_EPS = 1e-6
NEG_INF = -1e30
DILATED_PAIRS = ((128, 1), (512, 4), (2048, 16))
POOL_WINDOWS = (2, 4, 8, 16)
PEER_TOPK = 16
N_KEYS = 128

LANES = 128
SUBLANES = 8
VMEM_LIMIT = 56 * 1024 * 1024

ATT_BLOCK = 128
ROW_BLOCK = 512
SEL_BLOCK = 128
DENSE_TOKENS = 256
MXU_CHUNKS = 4
DENSE_KEY_GROUP = 16
DENSE_EXPERTS = DENSE_KEY_GROUP * N_KEYS


def _cparams(sem, flags=None):
    return pltpu.CompilerParams(dimension_semantics=sem, vmem_limit_bytes=VMEM_LIMIT, flags=flags)


def _rms(x, g):
    r = lax.rsqrt(jnp.mean(x * x, axis=-1, keepdims=True) + NORM_EPS)
    return (x * r) * g


def _bdot(a, b):
    return jnp.dot(a.astype(BF16), b.astype(BF16), preferred_element_type=F32)


def _bdot_nt(a, b):
    return lax.dot_general(a.astype(BF16), b.astype(BF16), (((1,), (1,)), ((), ())),
                           preferred_element_type=F32)


def _even_in_body(x, g, w, cw, u1_fn, d_conv, d_att, q_scale):
    h = _rms(x, g)
    p = _bdot(h, w)
    gate_b = p[:, 0:d_conv]
    gate_c = p[:, d_conv:2 * d_conv]
    xv = p[:, 2 * d_conv:3 * d_conv]
    o = 3 * d_conv
    q = p[:, o:o + d_att] * q_scale
    k = p[:, o + d_att:o + 2 * d_att]
    v = p[:, o + 2 * d_att:o + 3 * d_att]
    u = gate_c * xv
    u1, u2 = u1_fn(u)
    y = cw[0:1] * u2 + cw[1:2] * u1 + cw[2:3] * u
    return gate_b * y, q, k, v, u


def _even_in_seq_kernel(x_ref, g_ref, w_ref, cw_ref, ya_ref, q_ref, k_ref, v_ref, ul_ref, *rest,
                        d_conv, d_att, q_scale):
    dil_refs, carry_ref, stage_ref = rest[:-2], rest[-2], rest[-1]
    s = pl.program_id(1)

    @pl.when(s == 0)
    def _():
        carry_ref[...] = jnp.zeros_like(carry_ref)

    prev = carry_ref[...]
    tm = x_ref.shape[0]

    def shifted(u):
        rows = lax.broadcasted_iota(jnp.int32, u.shape, 0)
        u1 = jnp.where(rows == 0, prev[7:8], pltpu.roll(u, 1, 0))
        u2 = pltpu.roll(u, 2, 0)
        u2 = jnp.where(rows == 0, prev[6:7], jnp.where(rows == 1, prev[7:8], u2))
        return u1, u2

    ya, q, k, v, u = _even_in_body(x_ref[...], g_ref[...], w_ref[...], cw_ref[...], shifted,
                                   d_conv, d_att, q_scale)
    ya_ref[...] = ya
    q_ref[...] = q
    k_ref[...] = k
    v_ref[...] = v
    last = u[tm - SUBLANES:tm]
    carry_ref[...] = last
    ul_ref[...] = last
    n_cb = d_att // LANES
    for j, val in enumerate((q, k, v)):
        for c in range(n_cb):
            stage_ref[j, c] = val[:, c * LANES:(c + 1) * LANES]
    for i, ref in enumerate(dil_refs):
        rows = ref.shape[0]
        dil = tm // rows
        for r in range(dil):
            for c in range(n_cb):
                col = r * d_att + c * LANES
                ref[:, col:col + LANES] = stage_ref[i % 3, c, pl.ds(r, rows, stride=dil), :]


def _even_in_rows_kernel(x_ref, g_ref, w_ref, cw_ref, u1_ref, u2_ref, ya_ref, q_ref, k_ref, v_ref, u_ref,
                         *, d_conv, d_att, q_scale):
    ya, q, k, v, u = _even_in_body(x_ref[...], g_ref[...], w_ref[...], cw_ref[...],
                                   lambda _: (u1_ref[...], u2_ref[...]), d_conv, d_att, q_scale)
    ya_ref[...] = ya
    q_ref[...] = q
    k_ref[...] = k
    v_ref[...] = v
    u_ref[...] = u


def _even_in_prompt(x, g, w_bf, cw, d_conv, d_att, q_scale):
    b, s, d = x.shape
    tm = min(ROW_BLOCK, s)
    assert s % tm == 0 and tm % SUBLANES == 0
    ncol = w_bf.shape[1]
    row = lambda c: pl.BlockSpec((None, tm, c), lambda i, j: (i, j, 0))
    full = lambda shp: pl.BlockSpec(shp, lambda i, j: (0,) * len(shp))
    dils = [dl for _, dl in DILATED_PAIRS if dl > 1]
    assert all(tm % (dl * SUBLANES) == 0 for dl in dils)
    dil_specs = [pl.BlockSpec((None, tm // dl, dl * d_att), lambda i, j: (i, j, 0)) for dl in dils for _ in range(3)]
    dil_shapes = [jax.ShapeDtypeStruct((b, s // dl, dl * d_att), F32) for dl in dils for _ in range(3)]
    outs = pl.pallas_call(
        functools.partial(_even_in_seq_kernel, d_conv=d_conv, d_att=d_att, q_scale=q_scale),
        grid=(b, s // tm),
        in_specs=[row(d), full((1, d)), full((d, ncol)), full((cw.shape[0], d_conv))],
        out_specs=[row(d_conv), row(d_att), row(d_att), row(d_att),
                   pl.BlockSpec((None, SUBLANES, d_conv), lambda i, j: (i, 0, 0))] + dil_specs,
        out_shape=[jax.ShapeDtypeStruct((b, s, d_conv), F32)] + [jax.ShapeDtypeStruct((b, s, d_att), F32)] * 3
                  + [jax.ShapeDtypeStruct((b, SUBLANES, d_conv), F32)] + dil_shapes,
        scratch_shapes=[pltpu.VMEM((SUBLANES, d_conv), F32), pltpu.VMEM((3, d_att // LANES, tm, LANES), F32)],
        compiler_params=_cparams(("arbitrary", "arbitrary")),
        name="even_in_prompt",
    )(x, g, w_bf, cw)
    ya, q, k, v, ul = outs[:5]
    qkv = {1: (q, k, v)}
    for i, dl in enumerate(dils):
        qkv[dl] = tuple(outs[5 + 3 * i:8 + 3 * i])
    return ya, k, v, ul, qkv


def _even_in_sample(x, g, w_bf, cw, u1, u2, d_conv, d_att, q_scale):
    n, d = x.shape
    ncol = w_bf.shape[1]
    full = lambda shp: pl.BlockSpec(shp, lambda i: (0,) * len(shp))
    return pl.pallas_call(
        functools.partial(_even_in_rows_kernel, d_conv=d_conv, d_att=d_att, q_scale=q_scale),
        grid=(1,),
        in_specs=[full((n, d)), full((1, d)), full((d, ncol)), full((cw.shape[0], d_conv)),
                  full((n, d_conv)), full((n, d_conv))],
        out_specs=[full((n, d_conv)), full((n, d_att)), full((n, d_att)), full((n, d_att)), full((n, d_conv))],
        out_shape=[jax.ShapeDtypeStruct((n, d_conv), F32)] + [jax.ShapeDtypeStruct((n, d_att), F32)] * 3
                  + [jax.ShapeDtypeStruct((n, d_conv), F32)],
        compiler_params=_cparams(("arbitrary",)),
        name="even_in_sample",
    )(x, g, w_bf, cw, u1, u2)


def _attn_branch_kernel(q_ref, kp_ref, kc_ref, vp_ref, vc_ref, o_ref, l_ref, *, n_heads, hd, n_back):
    n = pl.program_id(2)
    blk = q_ref.shape[0]
    q = q_ref[...]
    k = jnp.concatenate([kp_ref[...], kc_ref[...]], axis=0)
    v = jnp.concatenate([vp_ref[...], vc_ref[...]], axis=0)
    qi = lax.broadcasted_iota(jnp.int32, (blk, 2 * blk), 0)
    ki = lax.broadcasted_iota(jnp.int32, (blk, 2 * blk), 1)
    dist = qi + blk - ki
    has_prev = jnp.where(n > 0, 0, blk)
    mask = (dist >= 0) & (dist <= n_back) & (ki >= has_prev)
    o_parts, l_parts = [], []
    for h in range(n_heads):
        sl = slice(h * hd, (h + 1) * hd)
        s = _bdot_nt(q[:, sl], k[:, sl])
        s = jnp.where(mask, s, NEG_INF)
        m = jnp.max(s, axis=-1, keepdims=True)
        p = jnp.exp(s - m)
        den = jnp.sum(p, axis=-1, keepdims=True)
        o_parts.append(_bdot(p, v[:, sl]) / den)
        l_parts.append(jnp.broadcast_to(m + jnp.log(den), (blk, hd)))
    o_ref[...] = jnp.concatenate(o_parts, axis=1)
    l_ref[...] = jnp.concatenate(l_parts, axis=1)


def _attn_branch(q, k, v, window, dil, n_heads):
    b, L, dda = q.shape
    da = dda // dil
    hd = da // n_heads
    n_back = window // dil
    assert L % ATT_BLOCK == 0 and n_back <= ATT_BLOCK
    nb = L // ATT_BLOCK
    cur = pl.BlockSpec((None, ATT_BLOCK, da), lambda i, r, n: (i, n, r))
    prev = pl.BlockSpec((None, ATT_BLOCK, da), lambda i, r, n: (i, jnp.maximum(n - 1, 0), r))
    o, l = pl.pallas_call(
        functools.partial(_attn_branch_kernel, n_heads=n_heads, hd=hd, n_back=n_back),
        grid=(b, dil, nb),
        in_specs=[cur, prev, cur, prev, cur],
        out_specs=[cur, cur],
        out_shape=[jax.ShapeDtypeStruct((b, L, dil * da), F32)] * 2,
        compiler_params=_cparams(("arbitrary", "arbitrary", "arbitrary")),
        name=f"attn_branch_d{dil}",
    )(q, k, k, v, v)
    return o, l


def _attn_sample_kernel(q_ref, kn_ref, vn_ref, *refs):
    n_br = (len(refs) - 1) // 2
    k_refs, v_refs, y_ref = refs[:n_br], refs[n_br:2 * n_br], refs[-1]
    q = q_ref[...][None]
    kn = kn_ref[...][None]
    vn = vn_ref[...][None]
    s0 = jnp.sum(kn * q, axis=-1, keepdims=True)
    o_list, l_list = [], []
    for g in range(n_br):
        s = jnp.sum(k_refs[g][...] * q, axis=-1, keepdims=True)
        m = jnp.maximum(jnp.max(s, axis=0, keepdims=True), s0)
        p = jnp.exp(s - m)
        p0 = jnp.exp(s0 - m)
        den = jnp.sum(p, axis=0, keepdims=True) + p0
        o_list.append((jnp.sum(p * v_refs[g][...], axis=0, keepdims=True) + p0 * vn) / den)
        l_list.append(m + jnp.log(den))
    y_ref[...] = _merge_branches(o_list, l_list)[0]


def _merge_branches(o_list, l_list):
    m = functools.reduce(jnp.maximum, l_list)
    e = [jnp.exp(l - m) for l in l_list]
    num = functools.reduce(lambda a, b: a + b, [ei * oi for ei, oi in zip(e, o_list)])
    return num / functools.reduce(lambda a, b: a + b, e)


def _attn_sample(q, k_new, v_new, k_state_all, v_state_all, layer):
    b, da = q.shape
    n_l, _, wb, n_heads, hd = k_state_all.shape
    row = pl.BlockSpec((None, n_heads, hd), lambda i: (i, 0, 0))
    ins, specs = [], []
    for st in (k_state_all, v_state_all):
        for window, dil in DILATED_PAIRS:
            n_back = window // dil
            assert n_back * dil <= wb and wb % dil == 0 and (wb // dil) % n_back == 0
            L = wb // dil
            ins.append(st.reshape(n_l, b, L, dil, n_heads, hd))
            specs.append(pl.BlockSpec((None, None, n_back, None, n_heads, hd),
                                      lambda i, L=L, nbk=n_back: (layer, i, L // nbk - 1, 0, 0, 0)))
    as_heads = lambda t: t.reshape(b, n_heads, hd)
    y = pl.pallas_call(
        _attn_sample_kernel,
        grid=(b,),
        in_specs=[row, row, row] + specs,
        out_specs=row,
        out_shape=jax.ShapeDtypeStruct((b, n_heads, hd), F32),
        compiler_params=_cparams(("arbitrary",)),
        name="attn_sample",
    )(as_heads(q), as_heads(k_new), as_heads(v_new), *ins)
    return y.reshape(b, da)


def _out_proj_kernel(x_ref, ya_ref, *refs, n_in):
    br = refs[:n_in]
    wa_ref, wb_ref, xo_ref = refs[n_in:n_in + 3]
    scratch = list(refs[n_in + 3:])
    tm, da = ya_ref.shape[0], wb_ref.shape[0]

    def token_order(ref):
        rows = ref.shape[0]
        if rows == tm:
            return ref[...]
        dil = tm // rows
        sc = scratch.pop(0)
        n_cb = da // LANES
        for r in range(dil):
            for c in range(n_cb):
                col = r * da + c * LANES
                sc[c, pl.ds(r, rows, stride=dil), :] = ref[:, col:col + LANES]
        return jnp.concatenate([sc[c] for c in range(n_cb)], axis=1)

    vals = [token_order(r) for r in br]
    yb = vals[0] if n_in == 1 else _merge_branches(vals[:n_in // 2], vals[n_in // 2:])
    xo_ref[...] = x_ref[...] + _bdot(ya_ref[...], wa_ref[...]) + _bdot(yb, wb_ref[...])


def _out_proj(x, ya, branch_arrays, wa_bf, wb_bf):
    n, d = x.shape
    da = wb_bf.shape[0]
    tm = min(ROW_BLOCK, n)
    assert n % tm == 0
    row = lambda c: pl.BlockSpec((tm, c), lambda i: (i, 0))
    full = lambda shp: pl.BlockSpec(shp, lambda i: (0,) * len(shp))
    dils = [a.shape[1] // da for a in branch_arrays]
    assert all(a.shape == (n // dl, dl * da) and (dl == 1 or tm % (dl * SUBLANES) == 0)
               for a, dl in zip(branch_arrays, dils))
    return pl.pallas_call(
        functools.partial(_out_proj_kernel, n_in=len(branch_arrays)),
        grid=(n // tm,),
        in_specs=[row(d), row(ya.shape[1])]
                 + [pl.BlockSpec((tm // dl, dl * da), lambda i: (i, 0)) for dl in dils]
                 + [full(wa_bf.shape), full(wb_bf.shape)],
        out_specs=row(d),
        out_shape=jax.ShapeDtypeStruct((n, d), F32),
        scratch_shapes=[pltpu.VMEM((da // LANES, tm, LANES), F32) for dl in dils if dl > 1],
        compiler_params=_cparams(("arbitrary",)),
        name="out_proj",
    )(x, ya, *branch_arrays, wa_bf, wb_bf)


def _pool_groups(dmat, w_ref, b_ref, sc):
    n_g = w_ref.shape[0]
    gw = w_ref.shape[1]
    ys = [_bdot(dmat[g], w_ref[g]) + b_ref[g:g + 1] for g in range(n_g)]
    return jnp.concatenate(ys, axis=1) * sc


def _pool_prompt_kernel(x_ref, g_ref, w_ref, b_ref, sc_ref, xo_ref, hl_ref, carry_ref):
    s = pl.program_id(1)
    hist = carry_ref.shape[0]

    @pl.when(s == 0)
    def _():
        carry_ref[...] = jnp.zeros_like(carry_ref)

    x = x_ref[...]
    tm = x.shape[0]
    h = _rms(x, g_ref[...])
    ext = jnp.concatenate([carry_ref[...], h], axis=0)
    pos = s * tm + lax.broadcasted_iota(jnp.int32, (tm, 1), 0) + 1
    gw = w_ref.shape[1]
    acc = ext
    width = 1
    diffs = []
    for g, w in enumerate(POOL_WINDOWS):
        while width < w:
            acc = acc + pltpu.roll(acc, width, 0)
            width *= 2
        cols = slice(g * gw, (g + 1) * gw)
        win = acc[hist:, cols]
        div = jnp.minimum(pos, w).astype(F32)
        diffs.append(win / div - h[:, cols])
    xo_ref[...] = x + _pool_groups(diffs, w_ref, b_ref, sc_ref[...])
    last = ext[tm:tm + hist]
    carry_ref[...] = last
    hl_ref[...] = last


def _pool_prompt(x, g, w_bf, bias, scale):
    b, s, d = x.shape
    tm = min(ROW_BLOCK, s)
    hist = 16
    assert s % tm == 0 and tm >= hist and max(POOL_WINDOWS) <= hist
    assert all(w == 2 ** (i + 1) for i, w in enumerate(POOL_WINDOWS))
    row = pl.BlockSpec((None, tm, d), lambda i, j: (i, j, 0))
    full = lambda shp: pl.BlockSpec(shp, lambda i, j: (0,) * len(shp))
    return pl.pallas_call(
        _pool_prompt_kernel,
        grid=(b, s // tm),
        in_specs=[row, full((1, d)), full(w_bf.shape), full(bias.shape), full((1, d))],
        out_specs=[row, pl.BlockSpec((None, hist, d), lambda i, j: (i, 0, 0))],
        out_shape=[jax.ShapeDtypeStruct((b, s, d), F32), jax.ShapeDtypeStruct((b, hist, d), F32)],
        scratch_shapes=[pltpu.VMEM((hist, d), F32)],
        compiler_params=_cparams(("arbitrary", "arbitrary")),
        name="pool_prompt",
    )(x, g, w_bf, bias, scale)


def _pool_sample_kernel(x_ref, st_ref, g_ref, w_ref, b_ref, sc_ref, xo_ref, h_ref):
    x = x_ref[...]
    h = _rms(x, g_ref[...])
    n_st = st_ref.shape[0]
    gw = w_ref.shape[1]
    diffs = []
    for g, w in enumerate(POOL_WINDOWS):
        cols = slice(g * gw, (g + 1) * gw)
        tot = h[:, cols]
        for j in range(1, w):
            tot = tot + st_ref[n_st - j][:, cols]
        diffs.append(tot / float(w) - h[:, cols])
    xo_ref[...] = x + _pool_groups(diffs, w_ref, b_ref, sc_ref[...])
    h_ref[...] = h


def _pool_sample(x, state_t, g, w_bf, bias, scale):
    n, d = x.shape
    assert state_t.shape[0] + 1 >= max(POOL_WINDOWS)
    full = lambda shp: pl.BlockSpec(shp, lambda i: (0,) * len(shp))
    return pl.pallas_call(
        _pool_sample_kernel,
        grid=(1,),
        in_specs=[full(x.shape), full(state_t.shape), full((1, d)), full(w_bf.shape), full(bias.shape),
                  full((1, d))],
        out_specs=[full(x.shape), full(x.shape)],
        out_shape=[jax.ShapeDtypeStruct((n, d), F32)] * 2,
        compiler_params=_cparams(("arbitrary",)),
        name="pool_sample",
    )(x, state_t, g, w_bf, bias, scale)


def _top16_ranked(s):
    n, t = s.shape
    rows = lax.broadcasted_iota(jnp.int32, (n, t), 0).astype(F32)
    rows16 = lax.broadcasted_iota(jnp.int32, (PEER_TOPK, t), 0)

    def step(it, carry):
        s, rank, sv = carry
        m = jnp.max(s, axis=0, keepdims=True)
        first = jnp.min(jnp.where(s == m, rows, float(n)), axis=0, keepdims=True)
        sel = rows == first
        itf = jnp.asarray(it, jnp.int32).astype(F32)
        return (jnp.where(sel, -jnp.inf, s), jnp.where(sel, itf, rank), jnp.where(rows16 == it, m, sv))

    init = (s, jnp.full((n, t), float(PEER_TOPK), F32), jnp.zeros((PEER_TOPK, t), F32))
    _, rank, sv = lax.fori_loop(0, PEER_TOPK, step, init)
    return rank, sv


def _top16_values(sa, sb):
    t = sa.shape[1]
    rows16 = lax.broadcasted_iota(jnp.int32, (PEER_TOPK, t), 0)

    def step(it, carry):
        a, b, sva, svb = carry
        ma = jnp.max(a, axis=0, keepdims=True)
        mb = jnp.max(b, axis=0, keepdims=True)
        return (jnp.where(a == ma, -jnp.inf, a), jnp.where(b == mb, -jnp.inf, b),
                jnp.where(rows16 == it, ma, sva), jnp.where(rows16 == it, mb, svb))

    zero = jnp.zeros((PEER_TOPK, t), F32)
    a, b, sva, svb = lax.fori_loop(0, PEER_TOPK, step, (sa, sb, zero, zero))
    cover = lambda x: jnp.sum(jnp.where(x == -jnp.inf, 1.0, 0.0), axis=0, keepdims=True)
    return sva, svb, cover(a), cover(b)


def _pair_cells(sv1, sv2):
    t = sv1.shape[1]
    half = SUBLANES
    tiles = [sv1[0:1] + sv2[0:half], sv1[0:1] + sv2[half:2 * half]]
    tiles += [sv1[r:r + 1] + sv2[0:half] for r in range(1, half)]
    tiles += [sv1[half:2 * half] + sv2[0:1]]
    cand0 = jnp.concatenate(tiles, axis=0)
    nrow = cand0.shape[0]
    i = lax.broadcasted_iota(jnp.int32, (nrow, t), 0)
    j = i - 2 * half
    mid = (lax.shift_right_arithmetic(j, 3) + 1) * PEER_TOPK + lax.bitwise_and(j, half - 1)
    idx = jnp.where(i < 2 * half, i,
                    jnp.where(i < nrow - half, mid, (i - (nrow - half) + half) * PEER_TOPK)).astype(F32)
    return cand0, idx


def _pairs_exact(cand0, idx):
    def step(_, carry):
        cand, chosen = carry
        m = jnp.max(cand, axis=0, keepdims=True)
        first = jnp.min(jnp.where(cand == m, idx, 1e9), axis=0, keepdims=True)
        sel = idx == first
        return jnp.where(sel, -jnp.inf, cand), jnp.where(sel, 1.0, chosen)

    _, chosen = lax.fori_loop(0, PEER_TOPK, step, (cand0, jnp.zeros_like(cand0)))
    return chosen


def _pairs_fast(cand0):
    def step(_, cand):
        return jnp.where(cand == jnp.max(cand, axis=0, keepdims=True), -jnp.inf, cand)

    cand = lax.fori_loop(0, PEER_TOPK, step, cand0)
    return jnp.where(cand == -jnp.inf, 1.0, 0.0)


def _lam_den(chosen, cand0):
    half = SUBLANES
    nrow = cand0.shape[0]
    den = jnp.sum(jnp.where(chosen > 0, jnp.exp(cand0 - cand0[0:1]), 0.0), axis=0, keepdims=True)
    lam = [jnp.sum(chosen[0:2 * half], axis=0, keepdims=True)]
    lam += [jnp.sum(chosen[2 * half + half * (r - 1):2 * half + half * r], axis=0, keepdims=True)
            for r in range(1, half)]
    lam += [chosen[nrow - half + r:nrow - half + r + 1] for r in range(half)]
    return lam, den


def _peer_select_kernel(x_ref, g_ref, wqt_ref, sk_ref, hb_ref, lam1_ref, rank2_ref, p1_ref, p2_ref, s_ref,
                        *, n_heads):
    h = _rms(x_ref[...], g_ref[...])
    hb = h.astype(BF16)
    hb_ref[...] = hb
    nk = sk_ref.shape[1]
    qt = lax.dot_general(wqt_ref[...], hb, (((1,), (1,)), ((), ())), preferred_element_type=F32)
    q_hi = qt.astype(BF16)
    q_lo = (qt - q_hi.astype(F32)).astype(BF16)
    for c in range(2 * n_heads):
        rows = slice(c * nk, (c + 1) * nk)
        q3 = jnp.concatenate([q_hi[rows], q_hi[rows], q_lo[rows]], axis=0)
        s_ref[c] = jnp.dot(sk_ref[c], q3, preferred_element_type=F32)

    def emit(hh, lam1, rank2, s1, s2, top1, top2, den):
        lam1_ref[hh] = lam1
        rank2_ref[hh] = rank2.astype(rank2_ref.dtype)
        p1_ref[hh] = jnp.exp(s1 - top1) / (2.0 * den)
        p2_ref[hh] = jnp.exp(s2 - top2).astype(p2_ref.dtype)

    for hh in range(n_heads):
        s1 = s_ref[2 * hh]
        s2 = s_ref[2 * hh + 1]
        sv1, sv2, cov1, cov2 = _top16_values(s1, s2)
        cand0, _ = _pair_cells(sv1, sv2)
        chosen = _pairs_fast(cand0)
        lam, den = _lam_den(chosen, cand0)
        lam1 = jnp.broadcast_to(lam[0], s1.shape)
        rank2 = jnp.zeros_like(s2)
        for r in range(PEER_TOPK):
            lam1 = jnp.where(sv1[r:r + 1] > s1, lam[r + 1] if r + 1 < PEER_TOPK else 0.0, lam1)
            rank2 = jnp.where(sv2[r:r + 1] > s2, float(r + 1), rank2)
        emit(hh, lam1, rank2, s1, s2, sv1[0:1], sv2[0:1], den)
        cov3 = jnp.sum(chosen, axis=0, keepdims=True)
        k = float(PEER_TOPK)
        tied = jnp.max(jnp.abs(cov1 - k) + jnp.abs(cov2 - k) + jnp.abs(cov3 - k)) > 0.0

        @pl.when(tied)
        def _(hh=hh):
            s1 = s_ref[2 * hh]
            s2 = s_ref[2 * hh + 1]
            rank1, sv1 = _top16_ranked(s1)
            rank2, sv2 = _top16_ranked(s2)
            cand0, idx = _pair_cells(sv1, sv2)
            lam, den = _lam_den(_pairs_exact(cand0, idx), cand0)
            lam1 = jnp.zeros_like(rank1)
            for r in range(PEER_TOPK):
                lam1 = jnp.where(rank1 == float(r), lam[r], lam1)
            emit(hh, lam1, rank2, s1, s2, sv1[0:1], sv2[0:1], den)


def _peer_select(x, g, wqt_bf, sk):
    n, d = x.shape
    n_heads = sk.shape[0] // 2
    nk = sk.shape[1]
    t = SEL_BLOCK
    assert n % t == 0 and nk == N_KEYS
    full = lambda shp: pl.BlockSpec(shp, lambda i: (0,) * len(shp))
    sel = pl.BlockSpec((n_heads, nk, t), lambda i: (0, 0, i))
    sel_f32 = jax.ShapeDtypeStruct((n_heads, nk, n), F32)
    sel_bf16 = jax.ShapeDtypeStruct((n_heads, nk, n), BF16)
    return pl.pallas_call(
        functools.partial(_peer_select_kernel, n_heads=n_heads),
        grid=(n // t,),
        in_specs=[pl.BlockSpec((t, d), lambda i: (i, 0)), full((1, d)), full(wqt_bf.shape), full(sk.shape)],
        out_specs=[pl.BlockSpec((t, d), lambda i: (i, 0)), sel, sel, sel, sel],
        out_shape=[jax.ShapeDtypeStruct((n, d), BF16), sel_f32, sel_bf16, sel_f32, sel_bf16],
        scratch_shapes=[pltpu.VMEM((2 * n_heads, nk, t), F32)],
        compiler_params=_cparams(("arbitrary",)),
        name="peer_select",
    )(x, g, wqt_bf, sk)


def _gelu_x2(a):
    return a * (1.0 + lax.erf(a * (1.0 / math.sqrt(2.0))))


def _peer_dense_kernel(x_ref, hb_ref, u_ref, vt_ref, lam1_ref, rank2_ref, p1_ref, p2_ref, xo_ref,
                       acc_ref, a0_ref, a1_ref, z0_ref, z1_ref, *, n_heads, n_tiles):
    s = pl.program_id(0)
    n_items = pl.num_programs(0) - 2
    t = hb_ref.shape[0]
    nk = rank2_ref.shape[1]

    @pl.when(s == 0)
    def _():
        acc_ref[...] = jnp.zeros_like(acc_ref)
        a1_ref[...] = jnp.zeros_like(a1_ref)
        z0_ref[...] = jnp.zeros_like(z0_ref)

    n_grp = u_ref.shape[0] // nk
    eb = jnp.clip(s - 1, 0, n_items - 1) % n_tiles
    grp = pl.ds(pl.multiple_of(eb * n_grp, SUBLANES), n_grp)

    te = u_ref.shape[0]
    d = vt_ref.shape[0]
    n_tc = t // LANES

    def act_chunk(a_new, j):
        rows = slice(j * (te // MXU_CHUNKS), (j + 1) * (te // MXU_CHUNKS))
        a_new[rows, :] = lax.dot_general(u_ref[rows, :], hb_ref[...], (((1,), (1,)), ((), ())),
                                         preferred_element_type=F32)

    def val_chunk(z_old, j):
        rows = slice(j * (d // MXU_CHUNKS), (j + 1) * (d // MXU_CHUNKS))
        acc_ref[rows, :] += jnp.dot(vt_ref[rows, :], z_old[...], preferred_element_type=F32)

    def gate_chunk(a_cur, z_new, c):
        ii, tc = divmod(c, n_tc)
        lanes = slice(tc * LANES, (tc + 1) * LANES)
        rows = slice(ii * nk, (ii + 1) * nk)
        gate = jnp.zeros((nk, LANES), BF16)
        for hh in range(n_heads):
            l1 = jnp.broadcast_to(lam1_ref[hh, grp, lanes][ii:ii + 1], (nk, LANES)).astype(BF16)
            pb = jnp.broadcast_to(p1_ref[hh, grp, lanes][ii:ii + 1], (nk, LANES)).astype(BF16)
            gate = gate + jnp.where(l1 > rank2_ref[hh, :, lanes], pb * p2_ref[hh, :, lanes],
                                    jnp.zeros((), BF16))
        z_new[rows, lanes] = gate * _gelu_x2(a_cur[rows, lanes]).astype(BF16)

    def step(a_new, a_cur, z_new, z_old):
        n_gate = n_grp * n_tc
        c = 0
        for k in range(2 * MXU_CHUNKS):
            if k % 2 == 0:
                act_chunk(a_new, k // 2)
            else:
                val_chunk(z_old, k // 2)
            while c < (k + 1) * n_gate // (2 * MXU_CHUNKS):
                gate_chunk(a_cur, z_new, c)
                c += 1

    @pl.when(s % 2 == 0)
    def _():
        step(a0_ref, a1_ref, z1_ref, z0_ref)

    @pl.when(s % 2 == 1)
    def _():
        step(a1_ref, a0_ref, z0_ref, z1_ref)

    @pl.when((s >= 2) & ((s - 2) % n_tiles == n_tiles - 1))
    def _():
        xo_ref[...] = x_ref[...] + acc_ref[...].T
        acc_ref[...] = jnp.zeros_like(acc_ref)


def _peer_dense(x, hb, u_all, vt_all, layer, lam1, rank2, p1, p2):
    n, d = x.shape
    n_exp = u_all.shape[1]
    n_heads, nk, _ = lam1.shape
    t = min(DENSE_TOKENS, n)
    te = DENSE_EXPERTS
    assert n % t == 0 and n_exp % te == 0 and te % (SUBLANES * nk) == 0 and n_exp == nk * nk and t % LANES == 0
    ne = n_exp // te
    n_items = (n // t) * ne
    assert ne >= 2
    item = lambda s, lag: jnp.clip(s - lag, 0, n_items - 1)
    tok = lambda lag: pl.BlockSpec((t, d), lambda s: (item(s, lag) // ne, 0))
    sel = pl.BlockSpec((n_heads, nk, t), lambda s: (0, 0, item(s, 1) // ne))
    return pl.pallas_call(
        functools.partial(_peer_dense_kernel, n_heads=n_heads, n_tiles=ne),
        grid=(n_items + 2,),
        in_specs=[tok(2), tok(0),
                  pl.BlockSpec((None, te, d), lambda s: (layer, item(s, 0) % ne, 0)),
                  pl.BlockSpec((None, d, te), lambda s: (layer, 0, item(s, 2) % ne)),
                  sel, sel, sel, sel],
        out_specs=tok(2),
        out_shape=jax.ShapeDtypeStruct((n, d), F32),
        scratch_shapes=[pltpu.VMEM((d, t), F32), pltpu.VMEM((te, t), F32), pltpu.VMEM((te, t), F32),
                        pltpu.VMEM((te, t), BF16), pltpu.VMEM((te, t), BF16)],
        compiler_params=_cparams(("arbitrary",)),
        name="peer_dense",
    )(x, hb, u_all, vt_all, lam1, rank2, p1, p2)


def _peer_tables_kernel(u_ref, v_ref, ub_ref, vt_ref):
    ub_ref[...] = u_ref[...].astype(BF16)
    vt_ref[...] = v_ref[...].T.astype(BF16)


def _peer_tables(peer_u, peer_v):
    n_l, n_exp, d = peer_u.shape
    te = ROW_BLOCK
    assert n_exp % te == 0
    src = pl.BlockSpec((None, te, d), lambda l, j: (l, j, 0))
    return pl.pallas_call(
        _peer_tables_kernel,
        grid=(n_l, n_exp // te),
        in_specs=[src, src],
        out_specs=[src, pl.BlockSpec((None, d, te), lambda l, j: (l, 0, j))],
        out_shape=[jax.ShapeDtypeStruct((n_l, n_exp, d), BF16), jax.ShapeDtypeStruct((n_l, d, n_exp), BF16)],
        compiler_params=_cparams(("arbitrary", "arbitrary")),
        name="peer_tables",
    )(peer_u, peer_v)


def _peer(x, g, wqt_bf, sk, u_all, vt_all, layer):
    n = x.shape[0]
    pad = (-n) % max(SEL_BLOCK, LANES)
    xp = jnp.pad(x, ((0, pad), (0, 0))) if pad else x
    hb, lam1, rank2, p1, p2 = _peer_select(xp, g, wqt_bf, sk)
    out = _peer_dense(xp, hb, u_all, vt_all, layer, lam1, rank2, p1, p2)
    return out[:n] if pad else out


def _shift_state_kernel(st_ref, new_ref, out_ref):
    wb = st_ref.shape[0]
    chunk = 64
    n_full = (wb - 1) // chunk

    def move(i, carry):
        r = i * chunk
        out_ref[pl.ds(r, chunk)] = st_ref[pl.ds(r + 1, chunk)]
        return carry

    lax.fori_loop(0, n_full, move, 0)
    out_ref[n_full * chunk:wb - 1] = st_ref[n_full * chunk + 1:wb]
    out_ref[wb - 1:wb] = new_ref[...]


def _shift_state(state, new):
    n_l, b, wb, n_heads, hd = state.shape
    blk = lambda rows: pl.BlockSpec((None, None, rows, n_heads, hd), lambda l, i: (l, i, 0, 0, 0))
    return pl.pallas_call(
        _shift_state_kernel,
        grid=(n_l, b),
        in_specs=[blk(wb), blk(1)],
        out_specs=blk(wb),
        out_shape=jax.ShapeDtypeStruct(state.shape, state.dtype),
        compiler_params=_cparams(("arbitrary", "arbitrary")),
        name="shift_state",
    )(state, new)


def _final_norm_kernel(x_ref, g_ref, o_ref):
    o_ref[...] = _rms(x_ref[...], g_ref[...])


def _final_norm(x, g):
    n, d = x.shape
    tm = min(ROW_BLOCK, n)
    assert n % tm == 0
    return pl.pallas_call(
        _final_norm_kernel,
        grid=(n // tm,),
        in_specs=[pl.BlockSpec((tm, d), lambda i: (i, 0)), pl.BlockSpec((1, d), lambda i: (0, 0))],
        out_specs=pl.BlockSpec((tm, d), lambda i: (i, 0)),
        out_shape=jax.ShapeDtypeStruct((n, d), F32),
        compiler_params=_cparams(("arbitrary",)),
        name="final_norm",
    )(x, g)


def kernel(x_prompt, x_sample, state_conv, state_win_k, state_win_v, state_pool, norm_mix, w_in_ab, conv_w,
           w_out_ab, pool_w, pool_b, pool_scale, norm_ffn, peer_wq, peer_subkeys, peer_u, peer_v, norm_final):
    b, s, d = x_prompt.shape
    bs, ts, _ = x_sample.shape
    assert ts == 1
    depth = norm_mix.shape[0]
    d_conv = conv_w.shape[2]
    n_heads_b, hd = state_win_k.shape[3], state_win_k.shape[4]
    d_att = n_heads_b * hd
    wb_s = state_win_k.shape[2]
    wb_p = min(max(w for w, _ in DILATED_PAIRS), s)
    pool_state = state_pool.shape[2]
    q_scale = float(hd) ** -0.5
    n_peer_heads = peer_subkeys.shape[1]

    xp = x_prompt.reshape(b * s, d)
    xs = x_sample.reshape(bs, d)
    u_all, vt_all = _peer_tables(peer_u, peer_v)
    conv_p, conv_s, wk_p, wk_s, wv_p, wv_s, pool_p, pool_s = [], [], [], [], [], [], [], []
    for l in range(depth):
        g_mix = norm_mix[l][None]
        if l % 2 == 0:
            e = l // 2
            w_in = w_in_ab[e].astype(BF16)
            wa = w_out_ab[e, :d_conv].astype(BF16)
            wbm = w_out_ab[e, d_conv:].astype(BF16)
            ya, k, v, ul, qkv = _even_in_prompt(xp.reshape(b, s, d), g_mix, w_in, conv_w[e], d_conv, d_att, q_scale)
            branches = [_attn_branch(*qkv[dl], w, dl, n_heads_b) for w, dl in DILATED_PAIRS]
            o_list = [br[0].reshape(b * s // dl, dl * d_att) for br, (_, dl) in zip(branches, DILATED_PAIRS)]
            l_list = [br[1].reshape(b * s // dl, dl * d_att) for br, (_, dl) in zip(branches, DILATED_PAIRS)]
            xp = _out_proj(xp, ya.reshape(b * s, d_conv), o_list + l_list, wa, wbm)
            conv_p.append(ul[:, SUBLANES - 2:])
            wk_p.append(k[:, s - wb_p:].reshape(b, wb_p, n_heads_b, hd))
            wv_p.append(v[:, s - wb_p:].reshape(b, wb_p, n_heads_b, hd))
            cst = state_conv[e]
            ya_s, q_s, k_s, v_s, u_s = _even_in_sample(xs, g_mix, w_in, conv_w[e], cst[:, 1], cst[:, 0],
                                                       d_conv, d_att, q_scale)
            yb_s = _attn_sample(q_s, k_s, v_s, state_win_k, state_win_v, e)
            xs = _out_proj(xs, ya_s, [yb_s], wa, wbm)
            conv_s.append(jnp.stack([cst[:, 1], u_s], axis=1))
            wk_s.append(k_s.reshape(bs, 1, n_heads_b, hd))
            wv_s.append(v_s.reshape(bs, 1, n_heads_b, hd))
        else:
            o = l // 2
            pw = pool_w[o].astype(BF16)
            xp3, hl = _pool_prompt(xp.reshape(b, s, d), g_mix, pw, pool_b[o], pool_scale[o][None])
            xp = xp3.reshape(b * s, d)
            pool_p.append(hl[:, hl.shape[1] - pool_state:])
            st = state_pool[o]
            xs, h_s = _pool_sample(xs, jnp.swapaxes(st, 0, 1), g_mix, pw, pool_b[o], pool_scale[o][None])
            pool_s.append(jnp.concatenate([st[:, 1:], h_s[:, None]], axis=1))
        g_ffn = norm_ffn[l][None]
        wqt = peer_wq[l].T.astype(BF16)
        sk = peer_subkeys[l].reshape(2 * n_peer_heads, N_KEYS, -1)
        sk_hi = sk.astype(BF16)
        sk_lo = (sk - sk_hi.astype(F32)).astype(BF16)
        sk = jnp.concatenate([sk_hi, sk_lo, sk_hi], axis=-1)
        xp = _peer(xp, g_ffn, wqt, sk, u_all, vt_all, l)
        xs = _peer(xs, g_ffn, wqt, sk, u_all, vt_all, l)
    gf = norm_final[None]
    y_prompt = _final_norm(xp, gf).reshape(b, s, d)
    y_sample = _final_norm(xs, gf).reshape(bs, ts, d)
    win_k_s = _shift_state(state_win_k, jnp.stack(wk_s))
    win_v_s = _shift_state(state_win_v, jnp.stack(wv_s))
    return (y_prompt, y_sample, jnp.stack(conv_p), jnp.stack(conv_s), jnp.stack(wk_p), win_k_s,
            jnp.stack(wv_p), win_v_s, jnp.stack(pool_p), jnp.stack(pool_s))
```

```python
import functools
import math

import jax
import jax.numpy as jnp
from jax import lax
from jax.experimental import pallas as pl
from jax.experimental.pallas import tpu as pltpu

F32 = jnp.float32
BF16 = jnp.bfloat16

NORM_EPS = 1e-6
NEG_INF = -1e30
DILATED_PAIRS = ((128, 1), (512, 4), (2048, 16))
POOL_WINDOWS = (2, 4, 8, 16)
PEER_TOPK = 16
N_KEYS = 128

LANES = 128
SUBLANES = 8
VMEM_LIMIT = 56 * 1024 * 1024

ATT_BLOCK = 128
ROW_BLOCK = 512
SEL_BLOCK = 256
DENSE_TOKENS = 256
MXU_CHUNKS = 4
DENSE_KEY_GROUP = 16
DENSE_EXPERTS = DENSE_KEY_GROUP * N_KEYS


def _cparams(sem, flags=None):
    return pltpu.CompilerParams(dimension_semantics=sem, vmem_limit_bytes=VMEM_LIMIT, flags=flags)


def _rms(x, g):
    r = lax.rsqrt(jnp.mean(x * x, axis=-1, keepdims=True) + NORM_EPS)
    return (x * r) * g


def _bdot(a, b):
    return jnp.dot(a.astype(BF16), b.astype(BF16), preferred_element_type=F32)


def _bdot_nt(a, b):
    return lax.dot_general(a.astype(BF16), b.astype(BF16), (((1,), (1,)), ((), ())),
                           preferred_element_type=F32)


def _even_in_body(x, g, w, cw, u1_fn, d_conv, d_att, q_scale):
    h = _rms(x, g)
    p = _bdot(h, w)
    gate_b = p[:, 0:d_conv]
    gate_c = p[:, d_conv:2 * d_conv]
    xv = p[:, 2 * d_conv:3 * d_conv]
    o = 3 * d_conv
    q = p[:, o:o + d_att] * q_scale
    k = p[:, o + d_att:o + 2 * d_att]
    v = p[:, o + 2 * d_att:o + 3 * d_att]
    u = gate_c * xv
    u1, u2 = u1_fn(u)
    y = cw[0:1] * u2 + cw[1:2] * u1 + cw[2:3] * u
    return gate_b * y, q, k, v, u


def _even_in_seq_kernel(x_ref, g_ref, w_ref, cw_ref, ya_ref, q_ref, k_ref, v_ref, ul_ref, *rest,
                        d_conv, d_att, q_scale):
    dil_refs, carry_ref, stage_ref = rest[:-2], rest[-2], rest[-1]
    s = pl.program_id(1)

    @pl.when(s == 0)
    def _():
        carry_ref[...] = jnp.zeros_like(carry_ref)

    prev = carry_ref[...]
    tm = x_ref.shape[0]

    def shifted(u):
        rows = lax.broadcasted_iota(jnp.int32, u.shape, 0)
        u1 = jnp.where(rows == 0, prev[7:8], pltpu.roll(u, 1, 0))
        u2 = pltpu.roll(u, 2, 0)
        u2 = jnp.where(rows == 0, prev[6:7], jnp.where(rows == 1, prev[7:8], u2))
        return u1, u2

    ya, q, k, v, u = _even_in_body(x_ref[...], g_ref[...], w_ref[...], cw_ref[...], shifted,
                                   d_conv, d_att, q_scale)
    ya_ref[...] = ya
    q_ref[...] = q
    k_ref[...] = k
    v_ref[...] = v
    last = u[tm - SUBLANES:tm]
    carry_ref[...] = last
    ul_ref[...] = last
    n_cb = d_att // LANES
    for j, val in enumerate((q, k, v)):
        for c in range(n_cb):
            stage_ref[j, c] = val[:, c * LANES:(c + 1) * LANES]
    for i, ref in enumerate(dil_refs):
        rows = ref.shape[0]
        dil = tm // rows
        for r in range(dil):
            for c in range(n_cb):
                col = r * d_att + c * LANES
                ref[:, col:col + LANES] = stage_ref[i % 3, c, pl.ds(r, rows, stride=dil), :]


def _even_in_rows_kernel(x_ref, g_ref, w_ref, cw_ref, u1_ref, u2_ref, ya_ref, q_ref, k_ref, v_ref, u_ref,
                         *, d_conv, d_att, q_scale):
    ya, q, k, v, u = _even_in_body(x_ref[...], g_ref[...], w_ref[...], cw_ref[...],
                                   lambda _: (u1_ref[...], u2_ref[...]), d_conv, d_att, q_scale)
    ya_ref[...] = ya
    q_ref[...] = q
    k_ref[...] = k
    v_ref[...] = v
    u_ref[...] = u


def _even_in_prompt(x, g, w_bf, cw, d_conv, d_att, q_scale):
    b, s, d = x.shape
    tm = min(ROW_BLOCK, s)
    assert s % tm == 0 and tm % SUBLANES == 0
    ncol = w_bf.shape[1]
    row = lambda c: pl.BlockSpec((None, tm, c), lambda i, j: (i, j, 0))
    full = lambda shp: pl.BlockSpec(shp, lambda i, j: (0,) * len(shp))
    dils = [dl for _, dl in DILATED_PAIRS if dl > 1]
    assert all(tm % (dl * SUBLANES) == 0 for dl in dils)
    dil_specs = [pl.BlockSpec((None, tm // dl, dl * d_att), lambda i, j: (i, j, 0)) for dl in dils for _ in range(3)]
    dil_shapes = [jax.ShapeDtypeStruct((b, s // dl, dl * d_att), F32) for dl in dils for _ in range(3)]
    outs = pl.pallas_call(
        functools.partial(_even_in_seq_kernel, d_conv=d_conv, d_att=d_att, q_scale=q_scale),
        grid=(b, s // tm),
        in_specs=[row(d), full((1, d)), full((d, ncol)), full((cw.shape[0], d_conv))],
        out_specs=[row(d_conv), row(d_att), row(d_att), row(d_att),
                   pl.BlockSpec((None, SUBLANES, d_conv), lambda i, j: (i, 0, 0))] + dil_specs,
        out_shape=[jax.ShapeDtypeStruct((b, s, d_conv), F32)] + [jax.ShapeDtypeStruct((b, s, d_att), F32)] * 3
                  + [jax.ShapeDtypeStruct((b, SUBLANES, d_conv), F32)] + dil_shapes,
        scratch_shapes=[pltpu.VMEM((SUBLANES, d_conv), F32), pltpu.VMEM((3, d_att // LANES, tm, LANES), F32)],
        compiler_params=_cparams(("arbitrary", "arbitrary")),
        name="even_in_prompt",
    )(x, g, w_bf, cw)
    ya, q, k, v, ul = outs[:5]
    qkv = {1: (q, k, v)}
    for i, dl in enumerate(dils):
        qkv[dl] = tuple(outs[5 + 3 * i:8 + 3 * i])
    return ya, k, v, ul, qkv


def _even_in_sample(x, g, w_bf, cw, u1, u2, d_conv, d_att, q_scale):
    n, d = x.shape
    ncol = w_bf.shape[1]
    full = lambda shp: pl.BlockSpec(shp, lambda i: (0,) * len(shp))
    return pl.pallas_call(
        functools.partial(_even_in_rows_kernel, d_conv=d_conv, d_att=d_att, q_scale=q_scale),
        grid=(1,),
        in_specs=[full((n, d)), full((1, d)), full((d, ncol)), full((cw.shape[0], d_conv)),
                  full((n, d_conv)), full((n, d_conv))],
        out_specs=[full((n, d_conv)), full((n, d_att)), full((n, d_att)), full((n, d_att)), full((n, d_conv))],
        out_shape=[jax.ShapeDtypeStruct((n, d_conv), F32)] + [jax.ShapeDtypeStruct((n, d_att), F32)] * 3
                  + [jax.ShapeDtypeStruct((n, d_conv), F32)],
        compiler_params=_cparams(("arbitrary",)),
        name="even_in_sample",
    )(x, g, w_bf, cw, u1, u2)


def _attn_branch_kernel(q_ref, kp_ref, kc_ref, vp_ref, vc_ref, o_ref, l_ref, *, n_heads, hd, n_back):
    n = pl.program_id(2)
    blk = q_ref.shape[0]
    q = q_ref[...]
    k = jnp.concatenate([kp_ref[...], kc_ref[...]], axis=0)
    v = jnp.concatenate([vp_ref[...], vc_ref[...]], axis=0)
    qi = lax.broadcasted_iota(jnp.int32, (blk, 2 * blk), 0)
    ki = lax.broadcasted_iota(jnp.int32, (blk, 2 * blk), 1)
    dist = qi + blk - ki
    has_prev = jnp.where(n > 0, 0, blk)
    mask = (dist >= 0) & (dist <= n_back) & (ki >= has_prev)
    o_parts, l_parts = [], []
    for h in range(n_heads):
        sl = slice(h * hd, (h + 1) * hd)
        s = _bdot_nt(q[:, sl], k[:, sl])
        s = jnp.where(mask, s, NEG_INF)
        m = jnp.max(s, axis=-1, keepdims=True)
        p = jnp.exp(s - m)
        den = jnp.sum(p, axis=-1, keepdims=True)
        o_parts.append(_bdot(p, v[:, sl]) / den)
        l_parts.append(jnp.broadcast_to(m + jnp.log(den), (blk, hd)))
    o_ref[...] = jnp.concatenate(o_parts, axis=1)
    l_ref[...] = jnp.concatenate(l_parts, axis=1)


def _attn_branch(q, k, v, window, dil, n_heads):
    b, L, dda = q.shape
    da = dda // dil
    hd = da // n_heads
    n_back = window // dil
    assert L % ATT_BLOCK == 0 and n_back <= ATT_BLOCK
    nb = L // ATT_BLOCK
    cur = pl.BlockSpec((None, ATT_BLOCK, da), lambda i, r, n: (i, n, r))
    prev = pl.BlockSpec((None, ATT_BLOCK, da), lambda i, r, n: (i, jnp.maximum(n - 1, 0), r))
    o, l = pl.pallas_call(
        functools.partial(_attn_branch_kernel, n_heads=n_heads, hd=hd, n_back=n_back),
        grid=(b, dil, nb),
        in_specs=[cur, prev, cur, prev, cur],
        out_specs=[cur, cur],
        out_shape=[jax.ShapeDtypeStruct((b, L, dil * da), F32)] * 2,
        compiler_params=_cparams(("arbitrary", "arbitrary", "arbitrary")),
        name=f"attn_branch_d{dil}",
    )(q, k, k, v, v)
    return o, l


def _attn_sample_kernel(q_ref, kn_ref, vn_ref, *refs):
    n_br = (len(refs) - 1) // 2
    k_refs, v_refs, y_ref = refs[:n_br], refs[n_br:2 * n_br], refs[-1]
    q = q_ref[...][None]
    kn = kn_ref[...][None]
    vn = vn_ref[...][None]
    s0 = jnp.sum(kn * q, axis=-1, keepdims=True)
    o_list, l_list = [], []
    for g in range(n_br):
        s = jnp.sum(k_refs[g][...] * q, axis=-1, keepdims=True)
        m = jnp.maximum(jnp.max(s, axis=0, keepdims=True), s0)
        p = jnp.exp(s - m)
        p0 = jnp.exp(s0 - m)
        den = jnp.sum(p, axis=0, keepdims=True) + p0
        o_list.append((jnp.sum(p * v_refs[g][...], axis=0, keepdims=True) + p0 * vn) / den)
        l_list.append(m + jnp.log(den))
    y_ref[...] = _merge_branches(o_list, l_list)[0]


def _merge_branches(o_list, l_list):
    m = functools.reduce(jnp.maximum, l_list)
    e = [jnp.exp(l - m) for l in l_list]
    num = functools.reduce(lambda a, b: a + b, [ei * oi for ei, oi in zip(e, o_list)])
    return num / functools.reduce(lambda a, b: a + b, e)


def _attn_sample(q, k_new, v_new, k_state_all, v_state_all, layer):
    b, da = q.shape
    n_l, _, wb, n_heads, hd = k_state_all.shape
    row = pl.BlockSpec((None, n_heads, hd), lambda i: (i, 0, 0))
    ins, specs = [], []
    for st in (k_state_all, v_state_all):
        for window, dil in DILATED_PAIRS:
            n_back = window // dil
            assert n_back * dil <= wb and wb % dil == 0 and (wb // dil) % n_back == 0
            L = wb // dil
            ins.append(st.reshape(n_l, b, L, dil, n_heads, hd))
            specs.append(pl.BlockSpec((None, None, n_back, None, n_heads, hd),
                                      lambda i, L=L, nbk=n_back: (layer, i, L // nbk - 1, 0, 0, 0)))
    as_heads = lambda t: t.reshape(b, n_heads, hd)
    y = pl.pallas_call(
        _attn_sample_kernel,
        grid=(b,),
        in_specs=[row, row, row] + specs,
        out_specs=row,
        out_shape=jax.ShapeDtypeStruct((b, n_heads, hd), F32),
        compiler_params=_cparams(("arbitrary",)),
        name="attn_sample",
    )(as_heads(q), as_heads(k_new), as_heads(v_new), *ins)
    return y.reshape(b, da)


def _out_proj_kernel(x_ref, ya_ref, *refs, n_in):
    br = refs[:n_in]
    wa_ref, wb_ref, xo_ref = refs[n_in:n_in + 3]
    scratch = list(refs[n_in + 3:])
    tm, da = ya_ref.shape[0], wb_ref.shape[0]

    def token_order(ref):
        rows = ref.shape[0]
        if rows == tm:
            return ref[...]
        dil = tm // rows
        sc = scratch.pop(0)
        n_cb = da // LANES
        for r in range(dil):
            for c in range(n_cb):
                col = r * da + c * LANES
                sc[c, pl.ds(r, rows, stride=dil), :] = ref[:, col:col + LANES]
        return jnp.concatenate([sc[c] for c in range(n_cb)], axis=1)

    vals = [token_order(r) for r in br]
    yb = vals[0] if n_in == 1 else _merge_branches(vals[:n_in // 2], vals[n_in // 2:])
    xo_ref[...] = x_ref[...] + _bdot(ya_ref[...], wa_ref[...]) + _bdot(yb, wb_ref[...])


def _out_proj(x, ya, branch_arrays, wa_bf, wb_bf):
    n, d = x.shape
    da = wb_bf.shape[0]
    tm = min(ROW_BLOCK, n)
    assert n % tm == 0
    row = lambda c: pl.BlockSpec((tm, c), lambda i: (i, 0))
    full = lambda shp: pl.BlockSpec(shp, lambda i: (0,) * len(shp))
    dils = [a.shape[1] // da for a in branch_arrays]
    assert all(a.shape == (n // dl, dl * da) and (dl == 1 or tm % (dl * SUBLANES) == 0)
               for a, dl in zip(branch_arrays, dils))
    return pl.pallas_call(
        functools.partial(_out_proj_kernel, n_in=len(branch_arrays)),
        grid=(n // tm,),
        in_specs=[row(d), row(ya.shape[1])]
                 + [pl.BlockSpec((tm // dl, dl * da), lambda i: (i, 0)) for dl in dils]
                 + [full(wa_bf.shape), full(wb_bf.shape)],
        out_specs=row(d),
        out_shape=jax.ShapeDtypeStruct((n, d), F32),
        scratch_shapes=[pltpu.VMEM((da // LANES, tm, LANES), F32) for dl in dils if dl > 1],
        compiler_params=_cparams(("arbitrary",)),
        name="out_proj",
    )(x, ya, *branch_arrays, wa_bf, wb_bf)


def _pool_groups(dmat, w_ref, b_ref, sc):
    n_g = w_ref.shape[0]
    gw = w_ref.shape[1]
    ys = [_bdot(dmat[g], w_ref[g]) + b_ref[g:g + 1] for g in range(n_g)]
    return jnp.concatenate(ys, axis=1) * sc


def _pool_prompt_kernel(x_ref, g_ref, w_ref, b_ref, sc_ref, xo_ref, hl_ref, carry_ref):
    s = pl.program_id(1)
    hist = carry_ref.shape[0]

    @pl.when(s == 0)
    def _():
        carry_ref[...] = jnp.zeros_like(carry_ref)

    x = x_ref[...]
    tm = x.shape[0]
    h = _rms(x, g_ref[...])
    ext = jnp.concatenate([carry_ref[...], h], axis=0)
    pos = s * tm + lax.broadcasted_iota(jnp.int32, (tm, 1), 0) + 1
    gw = w_ref.shape[1]
    acc = ext
    width = 1
    diffs = []
    for g, w in enumerate(POOL_WINDOWS):
        while width < w:
            acc = acc + pltpu.roll(acc, width, 0)
            width *= 2
        cols = slice(g * gw, (g + 1) * gw)
        win = acc[hist:, cols]
        div = jnp.minimum(pos, w).astype(F32)
        diffs.append(win / div - h[:, cols])
    xo_ref[...] = x + _pool_groups(diffs, w_ref, b_ref, sc_ref[...])
    last = ext[tm:tm + hist]
    carry_ref[...] = last
    hl_ref[...] = last


def _pool_prompt(x, g, w_bf, bias, scale):
    b, s, d = x.shape
    tm = min(ROW_BLOCK, s)
    hist = 16
    assert s % tm == 0 and tm >= hist and max(POOL_WINDOWS) <= hist
    assert all(w == 2 ** (i + 1) for i, w in enumerate(POOL_WINDOWS))
    row = pl.BlockSpec((None, tm, d), lambda i, j: (i, j, 0))
    full = lambda shp: pl.BlockSpec(shp, lambda i, j: (0,) * len(shp))
    return pl.pallas_call(
        _pool_prompt_kernel,
        grid=(b, s // tm),
        in_specs=[row, full((1, d)), full(w_bf.shape), full(bias.shape), full((1, d))],
        out_specs=[row, pl.BlockSpec((None, hist, d), lambda i, j: (i, 0, 0))],
        out_shape=[jax.ShapeDtypeStruct((b, s, d), F32), jax.ShapeDtypeStruct((b, hist, d), F32)],
        scratch_shapes=[pltpu.VMEM((hist, d), F32)],
        compiler_params=_cparams(("arbitrary", "arbitrary")),
        name="pool_prompt",
    )(x, g, w_bf, bias, scale)


def _pool_sample_kernel(x_ref, st_ref, g_ref, w_ref, b_ref, sc_ref, xo_ref, h_ref):
    x = x_ref[...]
    h = _rms(x, g_ref[...])
    n_st = st_ref.shape[0]
    gw = w_ref.shape[1]
    diffs = []
    for g, w in enumerate(POOL_WINDOWS):
        cols = slice(g * gw, (g + 1) * gw)
        tot = h[:, cols]
        for j in range(1, w):
            tot = tot + st_ref[n_st - j][:, cols]
        diffs.append(tot / float(w) - h[:, cols])
    xo_ref[...] = x + _pool_groups(diffs, w_ref, b_ref, sc_ref[...])
    h_ref[...] = h


def _pool_sample(x, state_t, g, w_bf, bias, scale):
    n, d = x.shape
    assert state_t.shape[0] + 1 >= max(POOL_WINDOWS)
    full = lambda shp: pl.BlockSpec(shp, lambda i: (0,) * len(shp))
    return pl.pallas_call(
        _pool_sample_kernel,
        grid=(1,),
        in_specs=[full(x.shape), full(state_t.shape), full((1, d)), full(w_bf.shape), full(bias.shape),
                  full((1, d))],
        out_specs=[full(x.shape), full(x.shape)],
        out_shape=[jax.ShapeDtypeStruct((n, d), F32)] * 2,
        compiler_params=_cparams(("arbitrary",)),
        name="pool_sample",
    )(x, state_t, g, w_bf, bias, scale)


def _top16_ranked(s):
    n, t = s.shape
    rows = lax.broadcasted_iota(jnp.int32, (n, t), 0).astype(F32)
    rows16 = lax.broadcasted_iota(jnp.int32, (PEER_TOPK, t), 0)

    def step(it, carry):
        s, rank, sv = carry
        m = jnp.max(s, axis=0, keepdims=True)
        first = jnp.min(jnp.where(s == m, rows, float(n)), axis=0, keepdims=True)
        sel = rows == first
        itf = jnp.asarray(it, jnp.int32).astype(F32)
        return (jnp.where(sel, -jnp.inf, s), jnp.where(sel, itf, rank), jnp.where(rows16 == it, m, sv))

    init = (s, jnp.full((n, t), float(PEER_TOPK), F32), jnp.zeros((PEER_TOPK, t), F32))
    _, rank, sv = lax.fori_loop(0, PEER_TOPK, step, init)
    return rank, sv


def _top16_values(sa, sb):
    t = sa.shape[1]
    rows16 = lax.broadcasted_iota(jnp.int32, (PEER_TOPK, t), 0)

    def step(it, carry):
        a, b, sva, svb = carry
        ma = jnp.max(a, axis=0, keepdims=True)
        mb = jnp.max(b, axis=0, keepdims=True)
        return (jnp.where(a == ma, -jnp.inf, a), jnp.where(b == mb, -jnp.inf, b),
                jnp.where(rows16 == it, ma, sva), jnp.where(rows16 == it, mb, svb))

    zero = jnp.zeros((PEER_TOPK, t), F32)
    a, b, sva, svb = lax.fori_loop(0, PEER_TOPK, step, (sa, sb, zero, zero))
    cover = lambda x: jnp.sum(jnp.where(x == -jnp.inf, 1.0, 0.0), axis=0, keepdims=True)
    return sva, svb, cover(a), cover(b)


def _pair_cells(sv1, sv2):
    t = sv1.shape[1]
    half = SUBLANES
    tiles = [sv1[0:1] + sv2[0:half], sv1[0:1] + sv2[half:2 * half]]
    tiles += [sv1[r:r + 1] + sv2[0:half] for r in range(1, half)]
    tiles += [sv1[half:2 * half] + sv2[0:1]]
    cand0 = jnp.concatenate(tiles, axis=0)
    nrow = cand0.shape[0]
    i = lax.broadcasted_iota(jnp.int32, (nrow, t), 0)
    j = i - 2 * half
    mid = (lax.shift_right_arithmetic(j, 3) + 1) * PEER_TOPK + lax.bitwise_and(j, half - 1)
    idx = jnp.where(i < 2 * half, i,
                    jnp.where(i < nrow - half, mid, (i - (nrow - half) + half) * PEER_TOPK)).astype(F32)
    return cand0, idx


def _pairs_exact(cand0, idx):
    def step(_, carry):
        cand, chosen = carry
        m = jnp.max(cand, axis=0, keepdims=True)
        first = jnp.min(jnp.where(cand == m, idx, 1e9), axis=0, keepdims=True)
        sel = idx == first
        return jnp.where(sel, -jnp.inf, cand), jnp.where(sel, 1.0, chosen)

    _, chosen = lax.fori_loop(0, PEER_TOPK, step, (cand0, jnp.zeros_like(cand0)))
    return chosen


def _pairs_fast(cand0):
    def step(_, cand):
        return jnp.where(cand == jnp.max(cand, axis=0, keepdims=True), -jnp.inf, cand)

    cand = lax.fori_loop(0, PEER_TOPK, step, cand0)
    return jnp.where(cand == -jnp.inf, 1.0, 0.0)


def _lam_den(chosen, cand0):
    half = SUBLANES
    nrow = cand0.shape[0]
    den = jnp.sum(jnp.where(chosen > 0, jnp.exp(cand0 - cand0[0:1]), 0.0), axis=0, keepdims=True)
    lam = [jnp.sum(chosen[0:2 * half], axis=0, keepdims=True)]
    lam += [jnp.sum(chosen[2 * half + half * (r - 1):2 * half + half * r], axis=0, keepdims=True)
            for r in range(1, half)]
    lam += [chosen[nrow - half + r:nrow - half + r + 1] for r in range(half)]
    return lam, den


def _peer_select_kernel(x_ref, g_ref, wqt_ref, sk_ref, hb_ref, lam1_ref, rank2_ref, p1_ref, p2_ref, s_ref,
                        *, n_heads):
    h = _rms(x_ref[...], g_ref[...])
    hb = h.astype(BF16)
    hb_ref[...] = hb
    nk = sk_ref.shape[1]
    qt = lax.dot_general(wqt_ref[...], hb, (((1,), (1,)), ((), ())), preferred_element_type=F32)
    q_hi = qt.astype(BF16)
    q_lo = (qt - q_hi.astype(F32)).astype(BF16)
    for c in range(2 * n_heads):
        rows = slice(c * nk, (c + 1) * nk)
        q3 = jnp.concatenate([q_hi[rows], q_hi[rows], q_lo[rows]], axis=0)
        s_ref[c] = jnp.dot(sk_ref[c], q3, preferred_element_type=F32)

    def emit(hh, lanes, lam1, rank2, s1, s2, top1, top2, den):
        lam1_ref[hh, :, lanes] = lam1
        rank2_ref[hh, :, lanes] = rank2.astype(rank2_ref.dtype)
        p1_ref[hh, :, lanes] = jnp.exp(s1 - top1) / (2.0 * den)
        p2_ref[hh, :, lanes] = jnp.exp(s2 - top2).astype(p2_ref.dtype)

    for hh, lc in ((hh, lc) for hh in range(n_heads) for lc in range(x_ref.shape[0] // LANES)):
        lanes = slice(lc * LANES, (lc + 1) * LANES)
        s1 = s_ref[2 * hh, :, lanes]
        s2 = s_ref[2 * hh + 1, :, lanes]
        sv1, sv2, cov1, cov2 = _top16_values(s1, s2)
        cand0, _ = _pair_cells(sv1, sv2)
        chosen = _pairs_fast(cand0)
        lam, den = _lam_den(chosen, cand0)
        lam1 = jnp.broadcast_to(lam[0], s1.shape)
        rank2 = jnp.zeros_like(s2)
        for r in range(PEER_TOPK):
            lam1 = jnp.where(sv1[r:r + 1] > s1, lam[r + 1] if r + 1 < PEER_TOPK else 0.0, lam1)
            rank2 = jnp.where(sv2[r:r + 1] > s2, float(r + 1), rank2)
        emit(hh, lanes, lam1, rank2, s1, s2, sv1[0:1], sv2[0:1], den)
        cov3 = jnp.sum(chosen, axis=0, keepdims=True)
        k = float(PEER_TOPK)
        tied = jnp.max(jnp.abs(cov1 - k) + jnp.abs(cov2 - k) + jnp.abs(cov3 - k)) > 0.0

        @pl.when(tied)
        def _(hh=hh, lanes=lanes):
            s1 = s_ref[2 * hh, :, lanes]
            s2 = s_ref[2 * hh + 1, :, lanes]
            rank1, sv1 = _top16_ranked(s1)
            rank2, sv2 = _top16_ranked(s2)
            cand0, idx = _pair_cells(sv1, sv2)
            lam, den = _lam_den(_pairs_exact(cand0, idx), cand0)
            lam1 = jnp.zeros_like(rank1)
            for r in range(PEER_TOPK):
                lam1 = jnp.where(rank1 == float(r), lam[r], lam1)
            emit(hh, lanes, lam1, rank2, s1, s2, sv1[0:1], sv2[0:1], den)


def _peer_select(x, g, wqt_bf, sk):
    n, d = x.shape
    n_heads = sk.shape[0] // 2
    nk = sk.shape[1]
    t = min(SEL_BLOCK, n)
    assert n % t == 0 and nk == N_KEYS
    full = lambda shp: pl.BlockSpec(shp, lambda i: (0,) * len(shp))
    sel = pl.BlockSpec((n_heads, nk, t), lambda i: (0, 0, i))
    sel_f32 = jax.ShapeDtypeStruct((n_heads, nk, n), F32)
    sel_bf16 = jax.ShapeDtypeStruct((n_heads, nk, n), BF16)
    return pl.pallas_call(
        functools.partial(_peer_select_kernel, n_heads=n_heads),
        grid=(n // t,),
        in_specs=[pl.BlockSpec((t, d), lambda i: (i, 0)), full((1, d)), full(wqt_bf.shape), full(sk.shape)],
        out_specs=[pl.BlockSpec((t, d), lambda i: (i, 0)), sel, sel, sel, sel],
        out_shape=[jax.ShapeDtypeStruct((n, d), BF16), sel_f32, sel_bf16, sel_f32, sel_bf16],
        scratch_shapes=[pltpu.VMEM((2 * n_heads, nk, t), F32)],
        compiler_params=_cparams(("arbitrary",)),
        name="peer_select",
    )(x, g, wqt_bf, sk)


def _gelu_x2(a):
    return a * (1.0 + lax.erf(a * (1.0 / math.sqrt(2.0))))


def _peer_dense_kernel(x_ref, hb_ref, u_ref, vt_ref, lam1_ref, rank2_ref, p1_ref, p2_ref, xo_ref,
                       acc_ref, a0_ref, a1_ref, z0_ref, z1_ref, *, n_heads, n_tiles):
    s = pl.program_id(0)
    n_items = pl.num_programs(0) - 2
    t = hb_ref.shape[0]
    nk = rank2_ref.shape[1]

    @pl.when(s == 0)
    def _():
        acc_ref[...] = jnp.zeros_like(acc_ref)
        a1_ref[...] = jnp.zeros_like(a1_ref)
        z0_ref[...] = jnp.zeros_like(z0_ref)

    n_grp = u_ref.shape[0] // nk
    eb = jnp.clip(s - 1, 0, n_items - 1) % n_tiles
    grp = pl.ds(pl.multiple_of(eb * n_grp, SUBLANES), n_grp)

    te = u_ref.shape[0]
    d = vt_ref.shape[0]
    n_tc = t // LANES

    def act_chunk(a_new, j):
        rows = slice(j * (te // MXU_CHUNKS), (j + 1) * (te // MXU_CHUNKS))
        a_new[rows, :] = lax.dot_general(u_ref[rows, :], hb_ref[...], (((1,), (1,)), ((), ())),
                                         preferred_element_type=F32)

    def val_chunk(z_old, j):
        rows = slice(j * (d // MXU_CHUNKS), (j + 1) * (d // MXU_CHUNKS))
        acc_ref[rows, :] += jnp.dot(vt_ref[rows, :], z_old[...], preferred_element_type=F32)

    def gate_chunk(a_cur, z_new, c):
        ii, tc = divmod(c, n_tc)
        lanes = slice(tc * LANES, (tc + 1) * LANES)
        rows = slice(ii * nk, (ii + 1) * nk)
        gate = jnp.zeros((nk, LANES), BF16)
        for hh in range(n_heads):
            l1 = jnp.broadcast_to(lam1_ref[hh, grp, lanes][ii:ii + 1], (nk, LANES)).astype(BF16)
            pb = jnp.broadcast_to(p1_ref[hh, grp, lanes][ii:ii + 1], (nk, LANES)).astype(BF16)
            gate = gate + jnp.where(l1 > rank2_ref[hh, :, lanes], pb * p2_ref[hh, :, lanes],
                                    jnp.zeros((), BF16))
        z_new[rows, lanes] = gate * _gelu_x2(a_cur[rows, lanes]).astype(BF16)

    def step(a_new, a_cur, z_new, z_old):
        n_gate = n_grp * n_tc
        c = 0
        for k in range(2 * MXU_CHUNKS):
            if k % 2 == 0:
                act_chunk(a_new, k // 2)
            else:
                val_chunk(z_old, k // 2)
            while c < (k + 1) * n_gate // (2 * MXU_CHUNKS):
                gate_chunk(a_cur, z_new, c)
                c += 1

    @pl.when(s % 2 == 0)
    def _():
        step(a0_ref, a1_ref, z1_ref, z0_ref)

    @pl.when(s % 2 == 1)
    def _():
        step(a1_ref, a0_ref, z0_ref, z1_ref)

    @pl.when((s >= 2) & ((s - 2) % n_tiles == n_tiles - 1))
    def _():
        xo_ref[...] = x_ref[...] + acc_ref[...].T
        acc_ref[...] = jnp.zeros_like(acc_ref)


def _peer_dense(x, hb, u_all, vt_all, layer, lam1, rank2, p1, p2):
    n, d = x.shape
    n_exp = u_all.shape[1]
    n_heads, nk, _ = lam1.shape
    t = min(DENSE_TOKENS, n)
    te = DENSE_EXPERTS
    assert n % t == 0 and n_exp % te == 0 and te % (SUBLANES * nk) == 0 and n_exp == nk * nk and t % LANES == 0
    ne = n_exp // te
    n_items = (n // t) * ne
    assert ne >= 2
    item = lambda s, lag: jnp.clip(s - lag, 0, n_items - 1)
    tok = lambda lag: pl.BlockSpec((t, d), lambda s: (item(s, lag) // ne, 0))
    sel = pl.BlockSpec((n_heads, nk, t), lambda s: (0, 0, item(s, 1) // ne))
    return pl.pallas_call(
        functools.partial(_peer_dense_kernel, n_heads=n_heads, n_tiles=ne),
        grid=(n_items + 2,),
        in_specs=[tok(2), tok(0),
                  pl.BlockSpec((None, te, d), lambda s: (layer, item(s, 0) % ne, 0)),
                  pl.BlockSpec((None, d, te), lambda s: (layer, 0, item(s, 2) % ne)),
                  sel, sel, sel, sel],
        out_specs=tok(2),
        out_shape=jax.ShapeDtypeStruct((n, d), F32),
        scratch_shapes=[pltpu.VMEM((d, t), F32), pltpu.VMEM((te, t), F32), pltpu.VMEM((te, t), F32),
                        pltpu.VMEM((te, t), BF16), pltpu.VMEM((te, t), BF16)],
        compiler_params=_cparams(("arbitrary",)),
        name="peer_dense",
    )(x, hb, u_all, vt_all, lam1, rank2, p1, p2)


def _peer_tables_kernel(u_ref, v_ref, ub_ref, vt_ref):
    ub_ref[...] = u_ref[...].astype(BF16)
    vt_ref[...] = v_ref[...].T.astype(BF16)


def _peer_tables(peer_u, peer_v):
    n_l, n_exp, d = peer_u.shape
    te = ROW_BLOCK
    assert n_exp % te == 0
    src = pl.BlockSpec((None, te, d), lambda l, j: (l, j, 0))
    return pl.pallas_call(
        _peer_tables_kernel,
        grid=(n_l, n_exp // te),
        in_specs=[src, src],
        out_specs=[src, pl.BlockSpec((None, d, te), lambda l, j: (l, 0, j))],
        out_shape=[jax.ShapeDtypeStruct((n_l, n_exp, d), BF16), jax.ShapeDtypeStruct((n_l, d, n_exp), BF16)],
        compiler_params=_cparams(("arbitrary", "arbitrary")),
        name="peer_tables",
    )(peer_u, peer_v)


def _peer(x, g, wqt_bf, sk, u_all, vt_all, layer):
    n = x.shape[0]
    pad = (-n) % LANES
    xp = jnp.pad(x, ((0, pad), (0, 0))) if pad else x
    hb, lam1, rank2, p1, p2 = _peer_select(xp, g, wqt_bf, sk)
    out = _peer_dense(xp, hb, u_all, vt_all, layer, lam1, rank2, p1, p2)
    return out[:n] if pad else out


def _final_norm_kernel(x_ref, g_ref, o_ref):
    o_ref[...] = _rms(x_ref[...], g_ref[...])


def _final_norm(x, g):
    n, d = x.shape
    tm = min(ROW_BLOCK, n)
    assert n % tm == 0
    return pl.pallas_call(
        _final_norm_kernel,
        grid=(n // tm,),
        in_specs=[pl.BlockSpec((tm, d), lambda i: (i, 0)), pl.BlockSpec((1, d), lambda i: (0, 0))],
        out_specs=pl.BlockSpec((tm, d), lambda i: (i, 0)),
        out_shape=jax.ShapeDtypeStruct((n, d), F32),
        compiler_params=_cparams(("arbitrary",)),
        name="final_norm",
    )(x, g)


def kernel(x_prompt, x_sample, state_conv, state_win_k, state_win_v, state_pool, norm_mix, w_in_ab, conv_w,
           w_out_ab, pool_w, pool_b, pool_scale, norm_ffn, peer_wq, peer_subkeys, peer_u, peer_v, norm_final):
    b, s, d = x_prompt.shape
    bs, ts, _ = x_sample.shape
    assert ts == 1
    depth = norm_mix.shape[0]
    d_conv = conv_w.shape[2]
    n_heads_b, hd = state_win_k.shape[3], state_win_k.shape[4]
    d_att = n_heads_b * hd
    wb_s = state_win_k.shape[2]
    wb_p = min(max(w for w, _ in DILATED_PAIRS), s)
    pool_state = state_pool.shape[2]
    q_scale = float(hd) ** -0.5
    n_peer_heads = peer_subkeys.shape[1]

    xp = x_prompt.reshape(b * s, d)
    xs = x_sample.reshape(bs, d)
    u_all, vt_all = _peer_tables(peer_u, peer_v)
    conv_p, conv_s, wk_p, wk_s, wv_p, wv_s, pool_p, pool_s = [], [], [], [], [], [], [], []
    for l in range(depth):
        g_mix = norm_mix[l][None]
        if l % 2 == 0:
            e = l // 2
            w_in = w_in_ab[e].astype(BF16)
            wa = w_out_ab[e, :d_conv].astype(BF16)
            wbm = w_out_ab[e, d_conv:].astype(BF16)
            ya, k, v, ul, qkv = _even_in_prompt(xp.reshape(b, s, d), g_mix, w_in, conv_w[e], d_conv, d_att, q_scale)
            branches = [_attn_branch(*qkv[dl], w, dl, n_heads_b) for w, dl in DILATED_PAIRS]
            o_list = [br[0].reshape(b * s // dl, dl * d_att) for br, (_, dl) in zip(branches, DILATED_PAIRS)]
            l_list = [br[1].reshape(b * s // dl, dl * d_att) for br, (_, dl) in zip(branches, DILATED_PAIRS)]
            xp = _out_proj(xp, ya.reshape(b * s, d_conv), o_list + l_list, wa, wbm)
            conv_p.append(ul[:, SUBLANES - 2:])
            wk_p.append(k[:, s - wb_p:].reshape(b, wb_p, n_heads_b, hd))
            wv_p.append(v[:, s - wb_p:].reshape(b, wb_p, n_heads_b, hd))
            cst = state_conv[e]
            ya_s, q_s, k_s, v_s, u_s = _even_in_sample(xs, g_mix, w_in, conv_w[e], cst[:, 1], cst[:, 0],
                                                       d_conv, d_att, q_scale)
            yb_s = _attn_sample(q_s, k_s, v_s, state_win_k, state_win_v, e)
            xs = _out_proj(xs, ya_s, [yb_s], wa, wbm)
            conv_s.append(jnp.stack([cst[:, 1], u_s], axis=1))
            wk_s.append(k_s.reshape(bs, 1, n_heads_b, hd))
            wv_s.append(v_s.reshape(bs, 1, n_heads_b, hd))
        else:
            o = l // 2
            pw = pool_w[o].astype(BF16)
            xp3, hl = _pool_prompt(xp.reshape(b, s, d), g_mix, pw, pool_b[o], pool_scale[o][None])
            xp = xp3.reshape(b * s, d)
            pool_p.append(hl[:, hl.shape[1] - pool_state:])
            st = state_pool[o]
            xs, h_s = _pool_sample(xs, jnp.swapaxes(st, 0, 1), g_mix, pw, pool_b[o], pool_scale[o][None])
            pool_s.append(jnp.concatenate([st[:, 1:], h_s[:, None]], axis=1))
        g_ffn = norm_ffn[l][None]
        wqt = peer_wq[l].T.astype(BF16)
        sk = peer_subkeys[l].reshape(2 * n_peer_heads, N_KEYS, -1)
        sk_hi = sk.astype(BF16)
        sk_lo = (sk - sk_hi.astype(F32)).astype(BF16)
        sk = jnp.concatenate([sk_hi, sk_lo, sk_hi], axis=-1)
        xp = _peer(xp, g_ffn, wqt, sk, u_all, vt_all, l)
        xs = _peer(xs, g_ffn, wqt, sk, u_all, vt_all, l)
    gf = norm_final[None]
    y_prompt = _final_norm(xp, gf).reshape(b, s, d)
    y_sample = _final_norm(xs, gf).reshape(bs, ts, d)
    win_k_s = jnp.concatenate([state_win_k[:, :, 1:], jnp.stack(wk_s)], axis=2)
    win_v_s = jnp.concatenate([state_win_v[:, :, 1:], jnp.stack(wv_s)], axis=2)
    return (y_prompt, y_sample, jnp.stack(conv_p), jnp.stack(conv_s), jnp.stack(wk_p), win_k_s,
            jnp.stack(wv_p), win_v_s, jnp.stack(pool_p), jnp.stack(pool_s))
```

```python
import functools
import math

import jax
import jax.numpy as jnp
from jax import lax
from jax.experimental import pallas as pl
from jax.experimental.pallas import tpu as pltpu

F32 = jnp.float32
BF16 = jnp.bfloat16

NORM_EPS = 1e-6
NEG_INF = -1e30
DILATED_PAIRS = ((128, 1), (512, 4), (2048, 16))
POOL_WINDOWS = (2, 4, 8, 16)
PEER_TOPK = 16
N_KEYS = 128

LANES = 128
SUBLANES = 8
VMEM_LIMIT = 56 * 1024 * 1024

ATT_BLOCK = 128
ROW_BLOCK = 512
SEL_BLOCK = 256
DENSE_TOKENS = 256
MXU_CHUNKS = 4
DENSE_KEY_GROUP = 16
DENSE_EXPERTS = DENSE_KEY_GROUP * N_KEYS


def _cparams(sem, flags=None):
    return pltpu.CompilerParams(dimension_semantics=sem, vmem_limit_bytes=VMEM_LIMIT, flags=flags)


def _rms(x, g):
    r = lax.rsqrt(jnp.mean(x * x, axis=-1, keepdims=True) + NORM_EPS)
    return (x * r) * g


def _bdot(a, b):
    return jnp.dot(a.astype(BF16), b.astype(BF16), preferred_element_type=F32)


def _bdot_nt(a, b):
    return lax.dot_general(a.astype(BF16), b.astype(BF16), (((1,), (1,)), ((), ())),
                           preferred_element_type=F32)


def _even_in_body(x, g, w, cw, u1_fn, d_conv, d_att, q_scale):
    h = _rms(x, g)
    p = _bdot(h, w)
    gate_b = p[:, 0:d_conv]
    gate_c = p[:, d_conv:2 * d_conv]
    xv = p[:, 2 * d_conv:3 * d_conv]
    o = 3 * d_conv
    q = p[:, o:o + d_att] * q_scale
    k = p[:, o + d_att:o + 2 * d_att]
    v = p[:, o + 2 * d_att:o + 3 * d_att]
    u = gate_c * xv
    u1, u2 = u1_fn(u)
    y = cw[0:1] * u2 + cw[1:2] * u1 + cw[2:3] * u
    return gate_b * y, q, k, v, u


def _even_in_seq_kernel(x_ref, g_ref, w_ref, cw_ref, ya_ref, q_ref, k_ref, v_ref, ul_ref, *rest,
                        d_conv, d_att, q_scale):
    dil_refs, carry_ref, stage_ref = rest[:-2], rest[-2], rest[-1]
    s = pl.program_id(1)

    @pl.when(s == 0)
    def _():
        carry_ref[...] = jnp.zeros_like(carry_ref)

    prev = carry_ref[...]
    tm = x_ref.shape[0]

    def shifted(u):
        rows = lax.broadcasted_iota(jnp.int32, u.shape, 0)
        u1 = jnp.where(rows == 0, prev[7:8], pltpu.roll(u, 1, 0))
        u2 = pltpu.roll(u, 2, 0)
        u2 = jnp.where(rows == 0, prev[6:7], jnp.where(rows == 1, prev[7:8], u2))
        return u1, u2

    ya, q, k, v, u = _even_in_body(x_ref[...], g_ref[...], w_ref[...], cw_ref[...], shifted,
                                   d_conv, d_att, q_scale)
    ya_ref[...] = ya
    q_ref[...] = q
    k_ref[...] = k
    v_ref[...] = v
    last = u[tm - SUBLANES:tm]
    carry_ref[...] = last
    ul_ref[...] = last
    n_cb = d_att // LANES
    for j, val in enumerate((q, k, v)):
        for c in range(n_cb):
            stage_ref[j, c] = val[:, c * LANES:(c + 1) * LANES]
    for i, ref in enumerate(dil_refs):
        rows = ref.shape[0]
        dil = tm // rows
        for r in range(dil):
            for c in range(n_cb):
                col = r * d_att + c * LANES
                ref[:, col:col + LANES] = stage_ref[i % 3, c, pl.ds(r, rows, stride=dil), :]


def _even_in_rows_kernel(x_ref, g_ref, w_ref, cw_ref, u1_ref, u2_ref, ya_ref, q_ref, k_ref, v_ref, u_ref,
                         *, d_conv, d_att, q_scale):
    ya, q, k, v, u = _even_in_body(x_ref[...], g_ref[...], w_ref[...], cw_ref[...],
                                   lambda _: (u1_ref[...], u2_ref[...]), d_conv, d_att, q_scale)
    ya_ref[...] = ya
    q_ref[...] = q
    k_ref[...] = k
    v_ref[...] = v
    u_ref[...] = u


def _even_in_prompt(x, g, w_bf, cw, d_conv, d_att, q_scale):
    b, s, d = x.shape
    tm = min(ROW_BLOCK, s)
    assert s % tm == 0 and tm % SUBLANES == 0
    ncol = w_bf.shape[1]
    row = lambda c: pl.BlockSpec((None, tm, c), lambda i, j: (i, j, 0))
    full = lambda shp: pl.BlockSpec(shp, lambda i, j: (0,) * len(shp))
    dils = [dl for _, dl in DILATED_PAIRS if dl > 1]
    assert all(tm % (dl * SUBLANES) == 0 for dl in dils)
    dil_specs = [pl.BlockSpec((None, tm // dl, dl * d_att), lambda i, j: (i, j, 0)) for dl in dils for _ in range(3)]
    dil_shapes = [jax.ShapeDtypeStruct((b, s // dl, dl * d_att), F32) for dl in dils for _ in range(3)]
    outs = pl.pallas_call(
        functools.partial(_even_in_seq_kernel, d_conv=d_conv, d_att=d_att, q_scale=q_scale),
        grid=(b, s // tm),
        in_specs=[row(d), full((1, d)), full((d, ncol)), full((cw.shape[0], d_conv))],
        out_specs=[row(d_conv), row(d_att), row(d_att), row(d_att),
                   pl.BlockSpec((None, SUBLANES, d_conv), lambda i, j: (i, 0, 0))] + dil_specs,
        out_shape=[jax.ShapeDtypeStruct((b, s, d_conv), F32)] + [jax.ShapeDtypeStruct((b, s, d_att), F32)] * 3
                  + [jax.ShapeDtypeStruct((b, SUBLANES, d_conv), F32)] + dil_shapes,
        scratch_shapes=[pltpu.VMEM((SUBLANES, d_conv), F32), pltpu.VMEM((3, d_att // LANES, tm, LANES), F32)],
        compiler_params=_cparams(("arbitrary", "arbitrary")),
        name="even_in_prompt",
    )(x, g, w_bf, cw)
    ya, q, k, v, ul = outs[:5]
    qkv = {1: (q, k, v)}
    for i, dl in enumerate(dils):
        qkv[dl] = tuple(outs[5 + 3 * i:8 + 3 * i])
    return ya, k, v, ul, qkv


def _even_in_sample(x, g, w_bf, cw, u1, u2, d_conv, d_att, q_scale):
    n, d = x.shape
    ncol = w_bf.shape[1]
    full = lambda shp: pl.BlockSpec(shp, lambda i: (0,) * len(shp))
    return pl.pallas_call(
        functools.partial(_even_in_rows_kernel, d_conv=d_conv, d_att=d_att, q_scale=q_scale),
        grid=(1,),
        in_specs=[full((n, d)), full((1, d)), full((d, ncol)), full((cw.shape[0], d_conv)),
                  full((n, d_conv)), full((n, d_conv))],
        out_specs=[full((n, d_conv)), full((n, d_att)), full((n, d_att)), full((n, d_att)), full((n, d_conv))],
        out_shape=[jax.ShapeDtypeStruct((n, d_conv), F32)] + [jax.ShapeDtypeStruct((n, d_att), F32)] * 3
                  + [jax.ShapeDtypeStruct((n, d_conv), F32)],
        compiler_params=_cparams(("arbitrary",)),
        name="even_in_sample",
    )(x, g, w_bf, cw, u1, u2)


def _attn_branch_kernel(q_ref, kp_ref, kc_ref, vp_ref, vc_ref, o_ref, l_ref, *, n_heads, hd, n_back):
    n = pl.program_id(2)
    blk = q_ref.shape[0]
    q = q_ref[...]
    k = jnp.concatenate([kp_ref[...], kc_ref[...]], axis=0)
    v = jnp.concatenate([vp_ref[...], vc_ref[...]], axis=0)
    qi = lax.broadcasted_iota(jnp.int32, (blk, 2 * blk), 0)
    ki = lax.broadcasted_iota(jnp.int32, (blk, 2 * blk), 1)
    dist = qi + blk - ki
    has_prev = jnp.where(n > 0, 0, blk)
    mask = (dist >= 0) & (dist <= n_back) & (ki >= has_prev)
    o_parts, l_parts = [], []
    for h in range(n_heads):
        sl = slice(h * hd, (h + 1) * hd)
        s = _bdot_nt(q[:, sl], k[:, sl])
        s = jnp.where(mask, s, NEG_INF)
        m = jnp.max(s, axis=-1, keepdims=True)
        p = jnp.exp(s - m)
        den = jnp.sum(p, axis=-1, keepdims=True)
        o_parts.append(_bdot(p, v[:, sl]) / den)
        l_parts.append(jnp.broadcast_to(m + jnp.log(den), (blk, hd)))
    o_ref[...] = jnp.concatenate(o_parts, axis=1)
    l_ref[...] = jnp.concatenate(l_parts, axis=1)


def _attn_branch(q, k, v, window, dil, n_heads):
    b, L, dda = q.shape
    da = dda // dil
    hd = da // n_heads
    n_back = window // dil
    assert L % ATT_BLOCK == 0 and n_back <= ATT_BLOCK
    nb = L // ATT_BLOCK
    cur = pl.BlockSpec((None, ATT_BLOCK, da), lambda i, r, n: (i, n, r))
    prev = pl.BlockSpec((None, ATT_BLOCK, da), lambda i, r, n: (i, jnp.maximum(n - 1, 0), r))
    o, l = pl.pallas_call(
        functools.partial(_attn_branch_kernel, n_heads=n_heads, hd=hd, n_back=n_back),
        grid=(b, dil, nb),
        in_specs=[cur, prev, cur, prev, cur],
        out_specs=[cur, cur],
        out_shape=[jax.ShapeDtypeStruct((b, L, dil * da), F32)] * 2,
        compiler_params=_cparams(("arbitrary", "arbitrary", "arbitrary")),
        name=f"attn_branch_d{dil}",
    )(q, k, k, v, v)
    return o, l


def _attn_sample_kernel(q_ref, kn_ref, vn_ref, *refs):
    n_br = (len(refs) - 1) // 2
    k_refs, v_refs, y_ref = refs[:n_br], refs[n_br:2 * n_br], refs[-1]
    q = q_ref[...][None]
    kn = kn_ref[...][None]
    vn = vn_ref[...][None]
    s0 = jnp.sum(kn * q, axis=-1, keepdims=True)
    o_list, l_list = [], []
    for g in range(n_br):
        s = jnp.sum(k_refs[g][...] * q, axis=-1, keepdims=True)
        m = jnp.maximum(jnp.max(s, axis=0, keepdims=True), s0)
        p = jnp.exp(s - m)
        p0 = jnp.exp(s0 - m)
        den = jnp.sum(p, axis=0, keepdims=True) + p0
        o_list.append((jnp.sum(p * v_refs[g][...], axis=0, keepdims=True) + p0 * vn) / den)
        l_list.append(m + jnp.log(den))
    y_ref[...] = _merge_branches(o_list, l_list)[0]


def _merge_branches(o_list, l_list):
    m = functools.reduce(jnp.maximum, l_list)
    e = [jnp.exp(l - m) for l in l_list]
    num = functools.reduce(lambda a, b: a + b, [ei * oi for ei, oi in zip(e, o_list)])
    return num / functools.reduce(lambda a, b: a + b, e)


def _attn_sample(q, k_new, v_new, k_state_all, v_state_all, layer):
    b, da = q.shape
    n_l, _, wb, n_heads, hd = k_state_all.shape
    row = pl.BlockSpec((None, n_heads, hd), lambda i: (i, 0, 0))
    ins, specs = [], []
    for st in (k_state_all, v_state_all):
        for window, dil in DILATED_PAIRS:
            n_back = window // dil
            assert n_back * dil <= wb and wb % dil == 0 and (wb // dil) % n_back == 0
            L = wb // dil
            ins.append(st.reshape(n_l, b, L, dil, n_heads, hd))
            specs.append(pl.BlockSpec((None, None, n_back, None, n_heads, hd),
                                      lambda i, L=L, nbk=n_back: (layer, i, L // nbk - 1, 0, 0, 0)))
    as_heads = lambda t: t.reshape(b, n_heads, hd)
    y = pl.pallas_call(
        _attn_sample_kernel,
        grid=(b,),
        in_specs=[row, row, row] + specs,
        out_specs=row,
        out_shape=jax.ShapeDtypeStruct((b, n_heads, hd), F32),
        compiler_params=_cparams(("arbitrary",)),
        name="attn_sample",
    )(as_heads(q), as_heads(k_new), as_heads(v_new), *ins)
    return y.reshape(b, da)


def _out_proj_kernel(x_ref, ya_ref, *refs, n_in):
    br = refs[:n_in]
    wa_ref, wb_ref, xo_ref = refs[n_in:n_in + 3]
    scratch = list(refs[n_in + 3:])
    tm, da = ya_ref.shape[0], wb_ref.shape[0]

    def token_order(ref):
        rows = ref.shape[0]
        if rows == tm:
            return ref[...]
        dil = tm // rows
        sc = scratch.pop(0)
        n_cb = da // LANES
        for r in range(dil):
            for c in range(n_cb):
                col = r * da + c * LANES
                sc[c, pl.ds(r, rows, stride=dil), :] = ref[:, col:col + LANES]
        return jnp.concatenate([sc[c] for c in range(n_cb)], axis=1)

    vals = [token_order(r) for r in br]
    yb = vals[0] if n_in == 1 else _merge_branches(vals[:n_in // 2], vals[n_in // 2:])
    xo_ref[...] = x_ref[...] + _bdot(ya_ref[...], wa_ref[...]) + _bdot(yb, wb_ref[...])


def _out_proj(x, ya, branch_arrays, wa_bf, wb_bf):
    n, d = x.shape
    da = wb_bf.shape[0]
    tm = min(ROW_BLOCK, n)
    assert n % tm == 0
    row = lambda c: pl.BlockSpec((tm, c), lambda i: (i, 0))
    full = lambda shp: pl.BlockSpec(shp, lambda i: (0,) * len(shp))
    dils = [a.shape[1] // da for a in branch_arrays]
    assert all(a.shape == (n // dl, dl * da) and (dl == 1 or tm % (dl * SUBLANES) == 0)
               for a, dl in zip(branch_arrays, dils))
    return pl.pallas_call(
        functools.partial(_out_proj_kernel, n_in=len(branch_arrays)),
        grid=(n // tm,),
        in_specs=[row(d), row(ya.shape[1])]
                 + [pl.BlockSpec((tm // dl, dl * da), lambda i: (i, 0)) for dl in dils]
                 + [full(wa_bf.shape), full(wb_bf.shape)],
        out_specs=row(d),
        out_shape=jax.ShapeDtypeStruct((n, d), F32),
        scratch_shapes=[pltpu.VMEM((da // LANES, tm, LANES), F32) for dl in dils if dl > 1],
        compiler_params=_cparams(("arbitrary",)),
        name="out_proj",
    )(x, ya, *branch_arrays, wa_bf, wb_bf)


def _pool_groups(dmat, w_ref, b_ref, sc):
    n_g = w_ref.shape[0]
    gw = w_ref.shape[1]
    ys = [_bdot(dmat[g], w_ref[g]) + b_ref[g:g + 1] for g in range(n_g)]
    return jnp.concatenate(ys, axis=1) * sc


def _pool_prompt_kernel(x_ref, g_ref, w_ref, b_ref, sc_ref, xo_ref, hl_ref, carry_ref):
    s = pl.program_id(1)
    hist = carry_ref.shape[0]

    @pl.when(s == 0)
    def _():
        carry_ref[...] = jnp.zeros_like(carry_ref)

    x = x_ref[...]
    tm = x.shape[0]
    h = _rms(x, g_ref[...])
    ext = jnp.concatenate([carry_ref[...], h], axis=0)
    pos = s * tm + lax.broadcasted_iota(jnp.int32, (tm, 1), 0) + 1
    gw = w_ref.shape[1]
    acc = ext
    width = 1
    diffs = []
    for g, w in enumerate(POOL_WINDOWS):
        while width < w:
            acc = acc + pltpu.roll(acc, width, 0)
            width *= 2
        cols = slice(g * gw, (g + 1) * gw)
        win = acc[hist:, cols]
        div = jnp.minimum(pos, w).astype(F32)
        diffs.append(win / div - h[:, cols])
    xo_ref[...] = x + _pool_groups(diffs, w_ref, b_ref, sc_ref[...])
    last = ext[tm:tm + hist]
    carry_ref[...] = last
    hl_ref[...] = last


def _pool_prompt(x, g, w_bf, bias, scale):
    b, s, d = x.shape
    tm = min(ROW_BLOCK, s)
    hist = 16
    assert s % tm == 0 and tm >= hist and max(POOL_WINDOWS) <= hist
    assert all(w == 2 ** (i + 1) for i, w in enumerate(POOL_WINDOWS))
    row = pl.BlockSpec((None, tm, d), lambda i, j: (i, j, 0))
    full = lambda shp: pl.BlockSpec(shp, lambda i, j: (0,) * len(shp))
    return pl.pallas_call(
        _pool_prompt_kernel,
        grid=(b, s // tm),
        in_specs=[row, full((1, d)), full(w_bf.shape), full(bias.shape), full((1, d))],
        out_specs=[row, pl.BlockSpec((None, hist, d), lambda i, j: (i, 0, 0))],
        out_shape=[jax.ShapeDtypeStruct((b, s, d), F32), jax.ShapeDtypeStruct((b, hist, d), F32)],
        scratch_shapes=[pltpu.VMEM((hist, d), F32)],
        compiler_params=_cparams(("arbitrary", "arbitrary")),
        name="pool_prompt",
    )(x, g, w_bf, bias, scale)


def _pool_sample_kernel(x_ref, st_ref, g_ref, w_ref, b_ref, sc_ref, xo_ref, h_ref):
    x = x_ref[...]
    h = _rms(x, g_ref[...])
    n_st = st_ref.shape[0]
    gw = w_ref.shape[1]
    diffs = []
    for g, w in enumerate(POOL_WINDOWS):
        cols = slice(g * gw, (g + 1) * gw)
        tot = h[:, cols]
        for j in range(1, w):
            tot = tot + st_ref[n_st - j][:, cols]
        diffs.append(tot / float(w) - h[:, cols])
    xo_ref[...] = x + _pool_groups(diffs, w_ref, b_ref, sc_ref[...])
    h_ref[...] = h


def _pool_sample(x, state_t, g, w_bf, bias, scale):
    n, d = x.shape
    assert state_t.shape[0] + 1 >= max(POOL_WINDOWS)
    full = lambda shp: pl.BlockSpec(shp, lambda i: (0,) * len(shp))
    return pl.pallas_call(
        _pool_sample_kernel,
        grid=(1,),
        in_specs=[full(x.shape), full(state_t.shape), full((1, d)), full(w_bf.shape), full(bias.shape),
                  full((1, d))],
        out_specs=[full(x.shape), full(x.shape)],
        out_shape=[jax.ShapeDtypeStruct((n, d), F32)] * 2,
        compiler_params=_cparams(("arbitrary",)),
        name="pool_sample",
    )(x, state_t, g, w_bf, bias, scale)


def _top16_ranked(s):
    n, t = s.shape
    rows = lax.broadcasted_iota(jnp.int32, (n, t), 0).astype(F32)
    rows16 = lax.broadcasted_iota(jnp.int32, (PEER_TOPK, t), 0)

    def step(it, carry):
        s, rank, sv = carry
        m = jnp.max(s, axis=0, keepdims=True)
        first = jnp.min(jnp.where(s == m, rows, float(n)), axis=0, keepdims=True)
        sel = rows == first
        itf = jnp.asarray(it, jnp.int32).astype(F32)
        return (jnp.where(sel, -jnp.inf, s), jnp.where(sel, itf, rank), jnp.where(rows16 == it, m, sv))

    init = (s, jnp.full((n, t), float(PEER_TOPK), F32), jnp.zeros((PEER_TOPK, t), F32))
    _, rank, sv = lax.fori_loop(0, PEER_TOPK, step, init)
    return rank, sv


def _sort_network(n_in):
    n, pairs, p = PEER_TOPK, [], 1
    assert n_in <= n
    while p < n:
        k = p
        while k >= 1:
            for j in range(k % p, n - k, 2 * k):
                for i in range(min(k, n - j - k)):
                    if (i + j) // (2 * p) == (i + j + k) // (2 * p):
                        pairs.append((i + j, i + j + k))
            k //= 2
        p *= 2
    return [(i, j) for i, j in pairs if j < n_in]


def _largest_distinct(groups, k):
    v = list(groups)
    n_g = len(v)
    for i, j in _sort_network(n_g):
        v[i], v[j] = jnp.maximum(v[i], v[j]), jnp.minimum(v[i], v[j])
    out = []
    for it in range(k):
        m = jnp.max(v[0], axis=0, keepdims=True)
        out.append(m)
        keep = min(n_g, k - it)
        pop = v[0] == m
        for j in range(keep - 1):
            v[j] = jnp.where(pop, v[j + 1], v[j])
        if keep == n_g and keep > 1:
            v[n_g - 1] = jnp.where(pop, -jnp.inf, v[n_g - 1])
    return out


def _top16_values(s):
    n, t = s.shape
    tops = _largest_distinct([s[SUBLANES * j:SUBLANES * (j + 1)] for j in range(n // SUBLANES)], PEER_TOPK)
    rows16 = lax.broadcasted_iota(jnp.int32, (PEER_TOPK, t), 0)
    sv = jnp.zeros((PEER_TOPK, t), F32)
    for it, m in enumerate(tops):
        sv = jnp.where(rows16 == it, m, sv)
    cover = jnp.sum(jnp.where(s >= tops[-1], 1.0, 0.0), axis=0, keepdims=True)
    return sv, cover


def _pair_cells(sv1, sv2):
    t = sv1.shape[1]
    half = SUBLANES
    tiles = [sv1[0:1] + sv2[0:half], sv1[0:1] + sv2[half:2 * half]]
    tiles += [sv1[r:r + 1] + sv2[0:half] for r in range(1, half)]
    tiles += [sv1[half:2 * half] + sv2[0:1]]
    cand0 = jnp.concatenate(tiles, axis=0)
    nrow = cand0.shape[0]
    i = lax.broadcasted_iota(jnp.int32, (nrow, t), 0)
    j = i - 2 * half
    mid = (lax.shift_right_arithmetic(j, 3) + 1) * PEER_TOPK + lax.bitwise_and(j, half - 1)
    idx = jnp.where(i < 2 * half, i,
                    jnp.where(i < nrow - half, mid, (i - (nrow - half) + half) * PEER_TOPK)).astype(F32)
    return cand0, idx


def _pairs_exact(cand0, idx):
    def step(_, carry):
        cand, chosen = carry
        m = jnp.max(cand, axis=0, keepdims=True)
        first = jnp.min(jnp.where(cand == m, idx, 1e9), axis=0, keepdims=True)
        sel = idx == first
        return jnp.where(sel, -jnp.inf, cand), jnp.where(sel, 1.0, chosen)

    _, chosen = lax.fori_loop(0, PEER_TOPK, step, (cand0, jnp.zeros_like(cand0)))
    return chosen


def _pairs_fast(cand0):
    tiles = [cand0[SUBLANES * j:SUBLANES * (j + 1)] for j in range(cand0.shape[0] // SUBLANES)]
    return jnp.where(cand0 >= _largest_distinct(tiles, PEER_TOPK)[-1], 1.0, 0.0)


def _lam_den(chosen, cand0):
    half = SUBLANES
    nrow = cand0.shape[0]
    den = jnp.sum(jnp.where(chosen > 0, jnp.exp(cand0 - cand0[0:1]), 0.0), axis=0, keepdims=True)
    lam = [jnp.sum(chosen[0:2 * half], axis=0, keepdims=True)]
    lam += [jnp.sum(chosen[2 * half + half * (r - 1):2 * half + half * r], axis=0, keepdims=True)
            for r in range(1, half)]
    lam += [chosen[nrow - half + r:nrow - half + r + 1] for r in range(half)]
    return lam, den


def _peer_select_kernel(x_ref, g_ref, wqt_ref, sk_ref, hb_ref, lam1_ref, rank2_ref, p1_ref, p2_ref, s_ref,
                        *, n_heads):
    h = _rms(x_ref[...], g_ref[...])
    hb = h.astype(BF16)
    hb_ref[...] = hb
    nk = sk_ref.shape[1]
    qt = lax.dot_general(wqt_ref[...], hb, (((1,), (1,)), ((), ())), preferred_element_type=F32)
    q_hi = qt.astype(BF16)
    q_lo = (qt - q_hi.astype(F32)).astype(BF16)
    for c in range(2 * n_heads):
        rows = slice(c * nk, (c + 1) * nk)
        q3 = jnp.concatenate([q_hi[rows], q_hi[rows], q_lo[rows]], axis=0)
        s_ref[c] = jnp.dot(sk_ref[c], q3, preferred_element_type=F32)

    def emit(hh, lanes, lam1, rank2, s1, s2, top1, top2, den):
        lam1_ref[hh, :, lanes] = lam1
        rank2_ref[hh, :, lanes] = rank2.astype(rank2_ref.dtype)
        p1_ref[hh, :, lanes] = jnp.exp(s1 - top1) / (2.0 * den)
        p2_ref[hh, :, lanes] = jnp.exp(s2 - top2).astype(p2_ref.dtype)

    for hh, lc in ((hh, lc) for hh in range(n_heads) for lc in range(x_ref.shape[0] // LANES)):
        lanes = slice(lc * LANES, (lc + 1) * LANES)
        s1 = s_ref[2 * hh, :, lanes]
        s2 = s_ref[2 * hh + 1, :, lanes]
        sv1, cov1 = _top16_values(s1)
        sv2, cov2 = _top16_values(s2)
        cand0, _ = _pair_cells(sv1, sv2)
        chosen = _pairs_fast(cand0)
        lam, den = _lam_den(chosen, cand0)
        lam1 = jnp.broadcast_to(lam[0], s1.shape)
        rank2 = jnp.zeros_like(s2)
        for r in range(PEER_TOPK):
            lam1 = jnp.where(sv1[r:r + 1] > s1, lam[r + 1] if r + 1 < PEER_TOPK else 0.0, lam1)
            rank2 = jnp.where(sv2[r:r + 1] > s2, float(r + 1), rank2)
        emit(hh, lanes, lam1, rank2, s1, s2, sv1[0:1], sv2[0:1], den)
        cov3 = jnp.sum(chosen, axis=0, keepdims=True)
        k = float(PEER_TOPK)
        tied = jnp.max(jnp.abs(cov1 - k) + jnp.abs(cov2 - k) + jnp.abs(cov3 - k)) > 0.0

        @pl.when(tied)
        def _(hh=hh, lanes=lanes):
            s1 = s_ref[2 * hh, :, lanes]
            s2 = s_ref[2 * hh + 1, :, lanes]
            rank1, sv1 = _top16_ranked(s1)
            rank2, sv2 = _top16_ranked(s2)
            cand0, idx = _pair_cells(sv1, sv2)
            lam, den = _lam_den(_pairs_exact(cand0, idx), cand0)
            lam1 = jnp.zeros_like(rank1)
            for r in range(PEER_TOPK):
                lam1 = jnp.where(rank1 == float(r), lam[r], lam1)
            emit(hh, lanes, lam1, rank2, s1, s2, sv1[0:1], sv2[0:1], den)


def _peer_select(x, g, wqt_bf, sk):
    n, d = x.shape
    n_heads = sk.shape[0] // 2
    nk = sk.shape[1]
    t = min(SEL_BLOCK, n)
    assert n % t == 0 and nk == N_KEYS
    full = lambda shp: pl.BlockSpec(shp, lambda i: (0,) * len(shp))
    sel = pl.BlockSpec((n_heads, nk, t), lambda i: (0, 0, i))
    sel_f32 = jax.ShapeDtypeStruct((n_heads, nk, n), F32)
    sel_bf16 = jax.ShapeDtypeStruct((n_heads, nk, n), BF16)
    return pl.pallas_call(
        functools.partial(_peer_select_kernel, n_heads=n_heads),
        grid=(n // t,),
        in_specs=[pl.BlockSpec((t, d), lambda i: (i, 0)), full((1, d)), full(wqt_bf.shape), full(sk.shape)],
        out_specs=[pl.BlockSpec((t, d), lambda i: (i, 0)), sel, sel, sel, sel],
        out_shape=[jax.ShapeDtypeStruct((n, d), BF16), sel_f32, sel_bf16, sel_f32, sel_bf16],
        scratch_shapes=[pltpu.VMEM((2 * n_heads, nk, t), F32)],
        compiler_params=_cparams(("arbitrary",)),
        name="peer_select",
    )(x, g, wqt_bf, sk)


def _gelu_x2(a):
    return a * (1.0 + lax.erf(a * (1.0 / math.sqrt(2.0))))


def _peer_dense_kernel(x_ref, hb_ref, u_ref, vt_ref, lam1_ref, rank2_ref, p1_ref, p2_ref, xo_ref,
                       acc_ref, a0_ref, a1_ref, z0_ref, z1_ref, *, n_heads, n_tiles):
    s = pl.program_id(0)
    n_items = pl.num_programs(0) - 2
    t = hb_ref.shape[0]
    nk = rank2_ref.shape[1]

    @pl.when(s == 0)
    def _():
        acc_ref[...] = jnp.zeros_like(acc_ref)
        a1_ref[...] = jnp.zeros_like(a1_ref)
        z0_ref[...] = jnp.zeros_like(z0_ref)

    n_grp = u_ref.shape[0] // nk
    eb = jnp.clip(s - 1, 0, n_items - 1) % n_tiles
    grp = pl.ds(pl.multiple_of(eb * n_grp, SUBLANES), n_grp)

    te = u_ref.shape[0]
    d = vt_ref.shape[0]
    n_tc = t // LANES

    def act_chunk(a_new, j):
        rows = slice(j * (te // MXU_CHUNKS), (j + 1) * (te // MXU_CHUNKS))
        a_new[rows, :] = lax.dot_general(u_ref[rows, :], hb_ref[...], (((1,), (1,)), ((), ())),
                                         preferred_element_type=F32)

    def val_chunk(z_old, j):
        rows = slice(j * (d // MXU_CHUNKS), (j + 1) * (d // MXU_CHUNKS))
        acc_ref[rows, :] += jnp.dot(vt_ref[rows, :], z_old[...], preferred_element_type=F32)

    def gate_chunk(a_cur, z_new, c):
        ii, tc = divmod(c, n_tc)
        lanes = slice(tc * LANES, (tc + 1) * LANES)
        rows = slice(ii * nk, (ii + 1) * nk)
        gate = jnp.zeros((nk, LANES), BF16)
        for hh in range(n_heads):
            l1 = jnp.broadcast_to(lam1_ref[hh, grp, lanes][ii:ii + 1], (nk, LANES)).astype(BF16)
            pb = jnp.broadcast_to(p1_ref[hh, grp, lanes][ii:ii + 1], (nk, LANES)).astype(BF16)
            gate = gate + jnp.where(l1 > rank2_ref[hh, :, lanes], pb * p2_ref[hh, :, lanes],
                                    jnp.zeros((), BF16))
        z_new[rows, lanes] = gate * _gelu_x2(a_cur[rows, lanes]).astype(BF16)

    def step(a_new, a_cur, z_new, z_old):
        n_gate = n_grp * n_tc
        c = 0
        for k in range(2 * MXU_CHUNKS):
            if k % 2 == 0:
                act_chunk(a_new, k // 2)
            else:
                val_chunk(z_old, k // 2)
            while c < (k + 1) * n_gate // (2 * MXU_CHUNKS):
                gate_chunk(a_cur, z_new, c)
                c += 1

    @pl.when(s % 2 == 0)
    def _():
        step(a0_ref, a1_ref, z1_ref, z0_ref)

    @pl.when(s % 2 == 1)
    def _():
        step(a1_ref, a0_ref, z0_ref, z1_ref)

    @pl.when((s >= 2) & ((s - 2) % n_tiles == n_tiles - 1))
    def _():
        xo_ref[...] = x_ref[...] + acc_ref[...].T
        acc_ref[...] = jnp.zeros_like(acc_ref)


def _peer_dense(x, hb, u_all, vt_all, layer, lam1, rank2, p1, p2):
    n, d = x.shape
    n_exp = u_all.shape[1]
    n_heads, nk, _ = lam1.shape
    t = min(DENSE_TOKENS, n)
    te = DENSE_EXPERTS
    assert n % t == 0 and n_exp % te == 0 and te % (SUBLANES * nk) == 0 and n_exp == nk * nk and t % LANES == 0
    ne = n_exp // te
    n_items = (n // t) * ne
    assert ne >= 2
    item = lambda s, lag: jnp.clip(s - lag, 0, n_items - 1)
    tok = lambda lag: pl.BlockSpec((t, d), lambda s: (item(s, lag) // ne, 0))
    sel = pl.BlockSpec((n_heads, nk, t), lambda s: (0, 0, item(s, 1) // ne))
    return pl.pallas_call(
        functools.partial(_peer_dense_kernel, n_heads=n_heads, n_tiles=ne),
        grid=(n_items + 2,),
        in_specs=[tok(2), tok(0),
                  pl.BlockSpec((None, te, d), lambda s: (layer, item(s, 0) % ne, 0)),
                  pl.BlockSpec((None, d, te), lambda s: (layer, 0, item(s, 2) % ne)),
                  sel, sel, sel, sel],
        out_specs=tok(2),
        out_shape=jax.ShapeDtypeStruct((n, d), F32),
        scratch_shapes=[pltpu.VMEM((d, t), F32), pltpu.VMEM((te, t), F32), pltpu.VMEM((te, t), F32),
                        pltpu.VMEM((te, t), BF16), pltpu.VMEM((te, t), BF16)],
        compiler_params=_cparams(("arbitrary",)),
        name="peer_dense",
    )(x, hb, u_all, vt_all, lam1, rank2, p1, p2)


def _peer_tables_kernel(u_ref, v_ref, ub_ref, vt_ref):
    ub_ref[...] = u_ref[...].astype(BF16)
    vt_ref[...] = v_ref[...].T.astype(BF16)


def _peer_tables(peer_u, peer_v):
    n_l, n_exp, d = peer_u.shape
    te = ROW_BLOCK
    assert n_exp % te == 0
    src = pl.BlockSpec((None, te, d), lambda l, j: (l, j, 0))
    return pl.pallas_call(
        _peer_tables_kernel,
        grid=(n_l, n_exp // te),
        in_specs=[src, src],
        out_specs=[src, pl.BlockSpec((None, d, te), lambda l, j: (l, 0, j))],
        out_shape=[jax.ShapeDtypeStruct((n_l, n_exp, d), BF16), jax.ShapeDtypeStruct((n_l, d, n_exp), BF16)],
        compiler_params=_cparams(("arbitrary", "arbitrary")),
        name="peer_tables",
    )(peer_u, peer_v)


def _peer(x, g, wqt_bf, sk, u_all, vt_all, layer):
    n = x.shape[0]
    pad = (-n) % LANES
    xp = jnp.pad(x, ((0, pad), (0, 0))) if pad else x
    hb, lam1, rank2, p1, p2 = _peer_select(xp, g, wqt_bf, sk)
    out = _peer_dense(xp, hb, u_all, vt_all, layer, lam1, rank2, p1, p2)
    return out[:n] if pad else out


def _final_norm_kernel(x_ref, g_ref, o_ref):
    o_ref[...] = _rms(x_ref[...], g_ref[...])


def _final_norm(x, g):
    n, d = x.shape
    tm = min(ROW_BLOCK, n)
    assert n % tm == 0
    return pl.pallas_call(
        _final_norm_kernel,
        grid=(n // tm,),
        in_specs=[pl.BlockSpec((tm, d), lambda i: (i, 0)), pl.BlockSpec((1, d), lambda i: (0, 0))],
        out_specs=pl.BlockSpec((tm, d), lambda i: (i, 0)),
        out_shape=jax.ShapeDtypeStruct((n, d), F32),
        compiler_params=_cparams(("arbitrary",)),
        name="final_norm",
    )(x, g)


def kernel(x_prompt, x_sample, state_conv, state_win_k, state_win_v, state_pool, norm_mix, w_in_ab, conv_w,
           w_out_ab, pool_w, pool_b, pool_scale, norm_ffn, peer_wq, peer_subkeys, peer_u, peer_v, norm_final):
    b, s, d = x_prompt.shape
    bs, ts, _ = x_sample.shape
    assert ts == 1
    depth = norm_mix.shape[0]
    d_conv = conv_w.shape[2]
    n_heads_b, hd = state_win_k.shape[3], state_win_k.shape[4]
    d_att = n_heads_b * hd
    wb_s = state_win_k.shape[2]
    wb_p = min(max(w for w, _ in DILATED_PAIRS), s)
    pool_state = state_pool.shape[2]
    q_scale = float(hd) ** -0.5
    n_peer_heads = peer_subkeys.shape[1]

    xp = x_prompt.reshape(b * s, d)
    xs = x_sample.reshape(bs, d)
    u_all, vt_all = _peer_tables(peer_u, peer_v)
    conv_p, conv_s, wk_p, wk_s, wv_p, wv_s, pool_p, pool_s = [], [], [], [], [], [], [], []
    for l in range(depth):
        g_mix = norm_mix[l][None]
        if l % 2 == 0:
            e = l // 2
            w_in = w_in_ab[e].astype(BF16)
            wa = w_out_ab[e, :d_conv].astype(BF16)
            wbm = w_out_ab[e, d_conv:].astype(BF16)
            ya, k, v, ul, qkv = _even_in_prompt(xp.reshape(b, s, d), g_mix, w_in, conv_w[e], d_conv, d_att, q_scale)
            branches = [_attn_branch(*qkv[dl], w, dl, n_heads_b) for w, dl in DILATED_PAIRS]
            o_list = [br[0].reshape(b * s // dl, dl * d_att) for br, (_, dl) in zip(branches, DILATED_PAIRS)]
            l_list = [br[1].reshape(b * s // dl, dl * d_att) for br, (_, dl) in zip(branches, DILATED_PAIRS)]
            xp = _out_proj(xp, ya.reshape(b * s, d_conv), o_list + l_list, wa, wbm)
            conv_p.append(ul[:, SUBLANES - 2:])
            wk_p.append(k[:, s - wb_p:].reshape(b, wb_p, n_heads_b, hd))
            wv_p.append(v[:, s - wb_p:].reshape(b, wb_p, n_heads_b, hd))
            cst = state_conv[e]
            ya_s, q_s, k_s, v_s, u_s = _even_in_sample(xs, g_mix, w_in, conv_w[e], cst[:, 1], cst[:, 0],
                                                       d_conv, d_att, q_scale)
            yb_s = _attn_sample(q_s, k_s, v_s, state_win_k, state_win_v, e)
            xs = _out_proj(xs, ya_s, [yb_s], wa, wbm)
            conv_s.append(jnp.stack([cst[:, 1], u_s], axis=1))
            wk_s.append(k_s.reshape(bs, 1, n_heads_b, hd))
            wv_s.append(v_s.reshape(bs, 1, n_heads_b, hd))
        else:
            o = l // 2
            pw = pool_w[o].astype(BF16)
            xp3, hl = _pool_prompt(xp.reshape(b, s, d), g_mix, pw, pool_b[o], pool_scale[o][None])
            xp = xp3.reshape(b * s, d)
            pool_p.append(hl[:, hl.shape[1] - pool_state:])
            st = state_pool[o]
            xs, h_s = _pool_sample(xs, jnp.swapaxes(st, 0, 1), g_mix, pw, pool_b[o], pool_scale[o][None])
            pool_s.append(jnp.concatenate([st[:, 1:], h_s[:, None]], axis=1))
        g_ffn = norm_ffn[l][None]
        wqt = peer_wq[l].T.astype(BF16)
        sk = peer_subkeys[l].reshape(2 * n_peer_heads, N_KEYS, -1)
        sk_hi = sk.astype(BF16)
        sk_lo = (sk - sk_hi.astype(F32)).astype(BF16)
        sk = jnp.concatenate([sk_hi, sk_lo, sk_hi], axis=-1)
        xp = _peer(xp, g_ffn, wqt, sk, u_all, vt_all, l)
        xs = _peer(xs, g_ffn, wqt, sk, u_all, vt_all, l)
    gf = norm_final[None]
    y_prompt = _final_norm(xp, gf).reshape(b, s, d)
    y_sample = _final_norm(xs, gf).reshape(bs, ts, d)
    win_k_s = jnp.concatenate([state_win_k[:, :, 1:], jnp.stack(wk_s)], axis=2)
    win_v_s = jnp.concatenate([state_win_v[:, :, 1:], jnp.stack(wv_s)], axis=2)
    return (y_prompt, y_sample, jnp.stack(conv_p), jnp.stack(conv_s), jnp.stack(wk_p), win_k_s,
            jnp.stack(wv_p), win_v_s, jnp.stack(pool_p), jnp.stack(pool_s))
```

```python
import functools
import math

import jax
import jax.numpy as jnp
from jax import lax
from jax.experimental import pallas as pl
from jax.experimental.pallas import tpu as pltpu

F32 = jnp.float32
BF16 = jnp.bfloat16

NORM_EPS = 1e-6
NEG_INF = -1e30
DILATED_PAIRS = ((128, 1), (512, 4), (2048, 16))
POOL_WINDOWS = (2, 4, 8, 16)
PEER_TOPK = 16
N_KEYS = 128

LANES = 128
SUBLANES = 8
VMEM_LIMIT = 56 * 1024 * 1024

ATT_BLOCK = 128
ROW_BLOCK = 512
PEER_TOKENS = 256
MXU_CHUNKS = 4
DENSE_KEY_GROUP = 16
DENSE_EXPERTS = DENSE_KEY_GROUP * N_KEYS


def _cparams(sem, flags=None):
    return pltpu.CompilerParams(dimension_semantics=sem, vmem_limit_bytes=VMEM_LIMIT, flags=flags)


def _rms(x, g):
    r = lax.rsqrt(jnp.mean(x * x, axis=-1, keepdims=True) + NORM_EPS)
    return (x * r) * g


def _bdot(a, b):
    return jnp.dot(a.astype(BF16), b.astype(BF16), preferred_element_type=F32)


def _bdot_nt(a, b):
    return lax.dot_general(a.astype(BF16), b.astype(BF16), (((1,), (1,)), ((), ())),
                           preferred_element_type=F32)


def _even_in_body(x, g, w, cw, u1_fn, d_conv, d_att, q_scale):
    h = _rms(x, g)
    p = _bdot(h, w)
    gate_b = p[:, 0:d_conv]
    gate_c = p[:, d_conv:2 * d_conv]
    xv = p[:, 2 * d_conv:3 * d_conv]
    o = 3 * d_conv
    q = p[:, o:o + d_att] * q_scale
    k = p[:, o + d_att:o + 2 * d_att]
    v = p[:, o + 2 * d_att:o + 3 * d_att]
    u = gate_c * xv
    u1, u2 = u1_fn(u)
    y = cw[0:1] * u2 + cw[1:2] * u1 + cw[2:3] * u
    return gate_b * y, q, k, v, u


def _even_in_seq_kernel(x_ref, g_ref, w_ref, cw_ref, ya_ref, q_ref, k_ref, v_ref, ul_ref, *rest,
                        d_conv, d_att, q_scale):
    dil_refs, carry_ref, stage_ref = rest[:-2], rest[-2], rest[-1]
    s = pl.program_id(1)

    @pl.when(s == 0)
    def _():
        carry_ref[...] = jnp.zeros_like(carry_ref)

    prev = carry_ref[...]
    tm = x_ref.shape[0]

    def shifted(u):
        rows = lax.broadcasted_iota(jnp.int32, u.shape, 0)
        u1 = jnp.where(rows == 0, prev[7:8], pltpu.roll(u, 1, 0))
        u2 = pltpu.roll(u, 2, 0)
        u2 = jnp.where(rows == 0, prev[6:7], jnp.where(rows == 1, prev[7:8], u2))
        return u1, u2

    ya, q, k, v, u = _even_in_body(x_ref[...], g_ref[...], w_ref[...], cw_ref[...], shifted,
                                   d_conv, d_att, q_scale)
    ya_ref[...] = ya
    q_ref[...] = q
    k_ref[...] = k
    v_ref[...] = v
    last = u[tm - SUBLANES:tm]
    carry_ref[...] = last
    ul_ref[...] = last
    n_cb = d_att // LANES
    for j, val in enumerate((q, k, v)):
        for c in range(n_cb):
            stage_ref[j, c] = val[:, c * LANES:(c + 1) * LANES]
    for i, ref in enumerate(dil_refs):
        rows = ref.shape[0]
        dil = tm // rows
        for r in range(dil):
            for c in range(n_cb):
                col = r * d_att + c * LANES
                ref[:, col:col + LANES] = stage_ref[i % 3, c, pl.ds(r, rows, stride=dil), :]


def _even_in_rows_kernel(x_ref, g_ref, w_ref, cw_ref, u1_ref, u2_ref, ya_ref, q_ref, k_ref, v_ref, u_ref,
                         *, d_conv, d_att, q_scale):
    ya, q, k, v, u = _even_in_body(x_ref[...], g_ref[...], w_ref[...], cw_ref[...],
                                   lambda _: (u1_ref[...], u2_ref[...]), d_conv, d_att, q_scale)
    ya_ref[...] = ya
    q_ref[...] = q
    k_ref[...] = k
    v_ref[...] = v
    u_ref[...] = u


def _even_in_prompt(x, g, w_bf, cw, d_conv, d_att, q_scale):
    b, s, d = x.shape
    tm = min(ROW_BLOCK, s)
    assert s % tm == 0 and tm % SUBLANES == 0
    ncol = w_bf.shape[1]
    row = lambda c: pl.BlockSpec((None, tm, c), lambda i, j: (i, j, 0))
    full = lambda shp: pl.BlockSpec(shp, lambda i, j: (0,) * len(shp))
    dils = [dl for _, dl in DILATED_PAIRS if dl > 1]
    assert all(tm % (dl * SUBLANES) == 0 for dl in dils)
    dil_specs = [pl.BlockSpec((None, tm // dl, dl * d_att), lambda i, j: (i, j, 0)) for dl in dils for _ in range(3)]
    dil_shapes = [jax.ShapeDtypeStruct((b, s // dl, dl * d_att), F32) for dl in dils for _ in range(3)]
    outs = pl.pallas_call(
        functools.partial(_even_in_seq_kernel, d_conv=d_conv, d_att=d_att, q_scale=q_scale),
        grid=(b, s // tm),
        in_specs=[row(d), full((1, d)), full((d, ncol)), full((cw.shape[0], d_conv))],
        out_specs=[row(d_conv), row(d_att), row(d_att), row(d_att),
                   pl.BlockSpec((None, SUBLANES, d_conv), lambda i, j: (i, 0, 0))] + dil_specs,
        out_shape=[jax.ShapeDtypeStruct((b, s, d_conv), F32)] + [jax.ShapeDtypeStruct((b, s, d_att), F32)] * 3
                  + [jax.ShapeDtypeStruct((b, SUBLANES, d_conv), F32)] + dil_shapes,
        scratch_shapes=[pltpu.VMEM((SUBLANES, d_conv), F32), pltpu.VMEM((3, d_att // LANES, tm, LANES), F32)],
        compiler_params=_cparams(("arbitrary", "arbitrary")),
        name="even_in_prompt",
    )(x, g, w_bf, cw)
    ya, q, k, v, ul = outs[:5]
    qkv = {1: (q, k, v)}
    for i, dl in enumerate(dils):
        qkv[dl] = tuple(outs[5 + 3 * i:8 + 3 * i])
    return ya, k, v, ul, qkv


def _even_in_sample(x, g, w_bf, cw, u1, u2, d_conv, d_att, q_scale):
    n, d = x.shape
    ncol = w_bf.shape[1]
    full = lambda shp: pl.BlockSpec(shp, lambda i: (0,) * len(shp))
    return pl.pallas_call(
        functools.partial(_even_in_rows_kernel, d_conv=d_conv, d_att=d_att, q_scale=q_scale),
        grid=(1,),
        in_specs=[full((n, d)), full((1, d)), full((d, ncol)), full((cw.shape[0], d_conv)),
                  full((n, d_conv)), full((n, d_conv))],
        out_specs=[full((n, d_conv)), full((n, d_att)), full((n, d_att)), full((n, d_att)), full((n, d_conv))],
        out_shape=[jax.ShapeDtypeStruct((n, d_conv), F32)] + [jax.ShapeDtypeStruct((n, d_att), F32)] * 3
                  + [jax.ShapeDtypeStruct((n, d_conv), F32)],
        compiler_params=_cparams(("arbitrary",)),
        name="even_in_sample",
    )(x, g, w_bf, cw, u1, u2)


def _attn_branch_kernel(q_ref, kp_ref, kc_ref, vp_ref, vc_ref, o_ref, l_ref, *, n_heads, hd, n_back):
    n = pl.program_id(2)
    blk = q_ref.shape[0]
    q = q_ref[...]
    k = jnp.concatenate([kp_ref[...], kc_ref[...]], axis=0)
    v = jnp.concatenate([vp_ref[...], vc_ref[...]], axis=0)
    qi = lax.broadcasted_iota(jnp.int32, (blk, 2 * blk), 0)
    ki = lax.broadcasted_iota(jnp.int32, (blk, 2 * blk), 1)
    dist = qi + blk - ki
    has_prev = jnp.where(n > 0, 0, blk)
    mask = (dist >= 0) & (dist <= n_back) & (ki >= has_prev)
    o_parts, l_parts = [], []
    for h in range(n_heads):
        sl = slice(h * hd, (h + 1) * hd)
        s = _bdot_nt(q[:, sl], k[:, sl])
        s = jnp.where(mask, s, NEG_INF)
        m = jnp.max(s, axis=-1, keepdims=True)
        p = jnp.exp(s - m)
        den = jnp.sum(p, axis=-1, keepdims=True)
        o_parts.append(_bdot(p, v[:, sl]) / den)
        l_parts.append(jnp.broadcast_to(m + jnp.log(den), (blk, hd)))
    o_ref[...] = jnp.concatenate(o_parts, axis=1)
    l_ref[...] = jnp.concatenate(l_parts, axis=1)


def _attn_branch(q, k, v, window, dil, n_heads):
    b, L, dda = q.shape
    da = dda // dil
    hd = da // n_heads
    n_back = window // dil
    assert L % ATT_BLOCK == 0 and n_back <= ATT_BLOCK
    nb = L // ATT_BLOCK
    cur = pl.BlockSpec((None, ATT_BLOCK, da), lambda i, r, n: (i, n, r))
    prev = pl.BlockSpec((None, ATT_BLOCK, da), lambda i, r, n: (i, jnp.maximum(n - 1, 0), r))
    o, l = pl.pallas_call(
        functools.partial(_attn_branch_kernel, n_heads=n_heads, hd=hd, n_back=n_back),
        grid=(b, dil, nb),
        in_specs=[cur, prev, cur, prev, cur],
        out_specs=[cur, cur],
        out_shape=[jax.ShapeDtypeStruct((b, L, dil * da), F32)] * 2,
        compiler_params=_cparams(("arbitrary", "arbitrary", "arbitrary")),
        name=f"attn_branch_d{dil}",
    )(q, k, k, v, v)
    return o, l


def _attn_sample_kernel(q_ref, kn_ref, vn_ref, *refs):
    n_br = (len(refs) - 1) // 2
    k_refs, v_refs, y_ref = refs[:n_br], refs[n_br:2 * n_br], refs[-1]
    q = q_ref[...][None]
    kn = kn_ref[...][None]
    vn = vn_ref[...][None]
    s0 = jnp.sum(kn * q, axis=-1, keepdims=True)
    o_list, l_list = [], []
    for g in range(n_br):
        s = jnp.sum(k_refs[g][...] * q, axis=-1, keepdims=True)
        m = jnp.maximum(jnp.max(s, axis=0, keepdims=True), s0)
        p = jnp.exp(s - m)
        p0 = jnp.exp(s0 - m)
        den = jnp.sum(p, axis=0, keepdims=True) + p0
        o_list.append((jnp.sum(p * v_refs[g][...], axis=0, keepdims=True) + p0 * vn) / den)
        l_list.append(m + jnp.log(den))
    y_ref[...] = _merge_branches(o_list, l_list)[0]


def _merge_branches(o_list, l_list):
    m = functools.reduce(jnp.maximum, l_list)
    e = [jnp.exp(l - m) for l in l_list]
    num = functools.reduce(lambda a, b: a + b, [ei * oi for ei, oi in zip(e, o_list)])
    return num / functools.reduce(lambda a, b: a + b, e)


def _attn_sample(q, k_new, v_new, k_state_all, v_state_all, layer):
    b, da = q.shape
    n_l, _, wb, n_heads, hd = k_state_all.shape
    row = pl.BlockSpec((None, n_heads, hd), lambda i: (i, 0, 0))
    ins, specs = [], []
    for st in (k_state_all, v_state_all):
        for window, dil in DILATED_PAIRS:
            n_back = window // dil
            assert n_back * dil <= wb and wb % dil == 0 and (wb // dil) % n_back == 0
            L = wb // dil
            ins.append(st.reshape(n_l, b, L, dil, n_heads, hd))
            specs.append(pl.BlockSpec((None, None, n_back, None, n_heads, hd),
                                      lambda i, L=L, nbk=n_back: (layer, i, L // nbk - 1, 0, 0, 0)))
    as_heads = lambda t: t.reshape(b, n_heads, hd)
    y = pl.pallas_call(
        _attn_sample_kernel,
        grid=(b,),
        in_specs=[row, row, row] + specs,
        out_specs=row,
        out_shape=jax.ShapeDtypeStruct((b, n_heads, hd), F32),
        compiler_params=_cparams(("arbitrary",)),
        name="attn_sample",
    )(as_heads(q), as_heads(k_new), as_heads(v_new), *ins)
    return y.reshape(b, da)


def _out_proj_kernel(x_ref, ya_ref, *refs, n_in):
    br = refs[:n_in]
    wa_ref, wb_ref, xo_ref = refs[n_in:n_in + 3]
    scratch = list(refs[n_in + 3:])
    tm, da = ya_ref.shape[0], wb_ref.shape[0]

    def token_order(ref):
        rows = ref.shape[0]
        if rows == tm:
            return ref[...]
        dil = tm // rows
        sc = scratch.pop(0)
        n_cb = da // LANES
        for r in range(dil):
            for c in range(n_cb):
                col = r * da + c * LANES
                sc[c, pl.ds(r, rows, stride=dil), :] = ref[:, col:col + LANES]
        return jnp.concatenate([sc[c] for c in range(n_cb)], axis=1)

    vals = [token_order(r) for r in br]
    yb = vals[0] if n_in == 1 else _merge_branches(vals[:n_in // 2], vals[n_in // 2:])
    xo_ref[...] = x_ref[...] + _bdot(ya_ref[...], wa_ref[...]) + _bdot(yb, wb_ref[...])


def _out_proj(x, ya, branch_arrays, wa_bf, wb_bf):
    n, d = x.shape
    da = wb_bf.shape[0]
    tm = min(ROW_BLOCK, n)
    assert n % tm == 0
    row = lambda c: pl.BlockSpec((tm, c), lambda i: (i, 0))
    full = lambda shp: pl.BlockSpec(shp, lambda i: (0,) * len(shp))
    dils = [a.shape[1] // da for a in branch_arrays]
    assert all(a.shape == (n // dl, dl * da) and (dl == 1 or tm % (dl * SUBLANES) == 0)
               for a, dl in zip(branch_arrays, dils))
    return pl.pallas_call(
        functools.partial(_out_proj_kernel, n_in=len(branch_arrays)),
        grid=(n // tm,),
        in_specs=[row(d), row(ya.shape[1])]
                 + [pl.BlockSpec((tm // dl, dl * da), lambda i: (i, 0)) for dl in dils]
                 + [full(wa_bf.shape), full(wb_bf.shape)],
        out_specs=row(d),
        out_shape=jax.ShapeDtypeStruct((n, d), F32),
        scratch_shapes=[pltpu.VMEM((da // LANES, tm, LANES), F32) for dl in dils if dl > 1],
        compiler_params=_cparams(("arbitrary",)),
        name="out_proj",
    )(x, ya, *branch_arrays, wa_bf, wb_bf)


def _pool_groups(dmat, w_ref, b_ref, sc):
    n_g = w_ref.shape[0]
    gw = w_ref.shape[1]
    ys = [_bdot(dmat[g], w_ref[g]) + b_ref[g:g + 1] for g in range(n_g)]
    return jnp.concatenate(ys, axis=1) * sc


def _pool_prompt_kernel(x_ref, g_ref, w_ref, b_ref, sc_ref, xo_ref, hl_ref, carry_ref):
    s = pl.program_id(1)
    hist = carry_ref.shape[0]

    @pl.when(s == 0)
    def _():
        carry_ref[...] = jnp.zeros_like(carry_ref)

    x = x_ref[...]
    tm = x.shape[0]
    h = _rms(x, g_ref[...])
    ext = jnp.concatenate([carry_ref[...], h], axis=0)
    pos = s * tm + lax.broadcasted_iota(jnp.int32, (tm, 1), 0) + 1
    gw = w_ref.shape[1]
    acc = ext
    width = 1
    diffs = []
    for g, w in enumerate(POOL_WINDOWS):
        while width < w:
            acc = acc + pltpu.roll(acc, width, 0)
            width *= 2
        cols = slice(g * gw, (g + 1) * gw)
        win = acc[hist:, cols]
        div = jnp.minimum(pos, w).astype(F32)
        diffs.append(win / div - h[:, cols])
    xo_ref[...] = x + _pool_groups(diffs, w_ref, b_ref, sc_ref[...])
    last = ext[tm:tm + hist]
    carry_ref[...] = last
    hl_ref[...] = last


def _pool_prompt(x, g, w_bf, bias, scale):
    b, s, d = x.shape
    tm = min(ROW_BLOCK, s)
    hist = 16
    assert s % tm == 0 and tm >= hist and max(POOL_WINDOWS) <= hist
    assert all(w == 2 ** (i + 1) for i, w in enumerate(POOL_WINDOWS))
    row = pl.BlockSpec((None, tm, d), lambda i, j: (i, j, 0))
    full = lambda shp: pl.BlockSpec(shp, lambda i, j: (0,) * len(shp))
    return pl.pallas_call(
        _pool_prompt_kernel,
        grid=(b, s // tm),
        in_specs=[row, full((1, d)), full(w_bf.shape), full(bias.shape), full((1, d))],
        out_specs=[row, pl.BlockSpec((None, hist, d), lambda i, j: (i, 0, 0))],
        out_shape=[jax.ShapeDtypeStruct((b, s, d), F32), jax.ShapeDtypeStruct((b, hist, d), F32)],
        scratch_shapes=[pltpu.VMEM((hist, d), F32)],
        compiler_params=_cparams(("arbitrary", "arbitrary")),
        name="pool_prompt",
    )(x, g, w_bf, bias, scale)


def _pool_sample_kernel(x_ref, st_ref, g_ref, w_ref, b_ref, sc_ref, xo_ref, h_ref):
    x = x_ref[...]
    h = _rms(x, g_ref[...])
    n_st = st_ref.shape[0]
    gw = w_ref.shape[1]
    diffs = []
    for g, w in enumerate(POOL_WINDOWS):
        cols = slice(g * gw, (g + 1) * gw)
        tot = h[:, cols]
        for j in range(1, w):
            tot = tot + st_ref[n_st - j][:, cols]
        diffs.append(tot / float(w) - h[:, cols])
    xo_ref[...] = x + _pool_groups(diffs, w_ref, b_ref, sc_ref[...])
    h_ref[...] = h


def _pool_sample(x, state_t, g, w_bf, bias, scale):
    n, d = x.shape
    assert state_t.shape[0] + 1 >= max(POOL_WINDOWS)
    full = lambda shp: pl.BlockSpec(shp, lambda i: (0,) * len(shp))
    return pl.pallas_call(
        _pool_sample_kernel,
        grid=(1,),
        in_specs=[full(x.shape), full(state_t.shape), full((1, d)), full(w_bf.shape), full(bias.shape),
                  full((1, d))],
        out_specs=[full(x.shape), full(x.shape)],
        out_shape=[jax.ShapeDtypeStruct((n, d), F32)] * 2,
        compiler_params=_cparams(("arbitrary",)),
        name="pool_sample",
    )(x, state_t, g, w_bf, bias, scale)


def _top16_ranked(s):
    n, t = s.shape
    rows = lax.broadcasted_iota(jnp.int32, (n, t), 0).astype(F32)
    rows16 = lax.broadcasted_iota(jnp.int32, (PEER_TOPK, t), 0)

    def step(it, carry):
        s, rank, sv = carry
        m = jnp.max(s, axis=0, keepdims=True)
        first = jnp.min(jnp.where(s == m, rows, float(n)), axis=0, keepdims=True)
        sel = rows == first
        itf = jnp.asarray(it, jnp.int32).astype(F32)
        return (jnp.where(sel, -jnp.inf, s), jnp.where(sel, itf, rank), jnp.where(rows16 == it, m, sv))

    init = (s, jnp.full((n, t), float(PEER_TOPK), F32), jnp.zeros((PEER_TOPK, t), F32))
    _, rank, sv = lax.fori_loop(0, PEER_TOPK, step, init)
    return rank, sv


def _sort_network(n_in):
    n, pairs, p = PEER_TOPK, [], 1
    assert n_in <= n
    while p < n:
        k = p
        while k >= 1:
            for j in range(k % p, n - k, 2 * k):
                for i in range(min(k, n - j - k)):
                    if (i + j) // (2 * p) == (i + j + k) // (2 * p):
                        pairs.append((i + j, i + j + k))
            k //= 2
        p *= 2
    return [(i, j) for i, j in pairs if j < n_in]


def _largest_distinct(groups, k):
    v = list(groups)
    n_g = len(v)
    for i, j in _sort_network(n_g):
        v[i], v[j] = jnp.maximum(v[i], v[j]), jnp.minimum(v[i], v[j])
    out = []
    for it in range(k):
        m = jnp.max(v[0], axis=0, keepdims=True)
        out.append(m)
        keep = min(n_g, k - it)
        pop = v[0] == m
        for j in range(keep - 1):
            v[j] = jnp.where(pop, v[j + 1], v[j])
        if keep == n_g and keep > 1:
            v[n_g - 1] = jnp.where(pop, -jnp.inf, v[n_g - 1])
    return out


def _top16_values(s):
    n, t = s.shape
    tops = _largest_distinct([s[SUBLANES * j:SUBLANES * (j + 1)] for j in range(n // SUBLANES)], PEER_TOPK)
    rows16 = lax.broadcasted_iota(jnp.int32, (PEER_TOPK, t), 0)
    sv = jnp.zeros((PEER_TOPK, t), F32)
    for it, m in enumerate(tops):
        sv = jnp.where(rows16 == it, m, sv)
    cover = jnp.sum(jnp.where(s >= tops[-1], 1.0, 0.0), axis=0, keepdims=True)
    return sv, cover


def _pair_cells(sv1, sv2):
    t = sv1.shape[1]
    half = SUBLANES
    tiles = [sv1[0:1] + sv2[0:half], sv1[0:1] + sv2[half:2 * half]]
    tiles += [sv1[r:r + 1] + sv2[0:half] for r in range(1, half)]
    tiles += [sv1[half:2 * half] + sv2[0:1]]
    cand0 = jnp.concatenate(tiles, axis=0)
    nrow = cand0.shape[0]
    i = lax.broadcasted_iota(jnp.int32, (nrow, t), 0)
    j = i - 2 * half
    mid = (lax.shift_right_arithmetic(j, 3) + 1) * PEER_TOPK + lax.bitwise_and(j, half - 1)
    idx = jnp.where(i < 2 * half, i,
                    jnp.where(i < nrow - half, mid, (i - (nrow - half) + half) * PEER_TOPK)).astype(F32)
    return cand0, idx


def _pairs_exact(cand0, idx):
    def step(_, carry):
        cand, chosen = carry
        m = jnp.max(cand, axis=0, keepdims=True)
        first = jnp.min(jnp.where(cand == m, idx, 1e9), axis=0, keepdims=True)
        sel = idx == first
        return jnp.where(sel, -jnp.inf, cand), jnp.where(sel, 1.0, chosen)

    _, chosen = lax.fori_loop(0, PEER_TOPK, step, (cand0, jnp.zeros_like(cand0)))
    return chosen


def _pairs_fast(cand0):
    tiles = [cand0[SUBLANES * j:SUBLANES * (j + 1)] for j in range(cand0.shape[0] // SUBLANES)]
    return jnp.where(cand0 >= _largest_distinct(tiles, PEER_TOPK)[-1], 1.0, 0.0)


def _lam_den(chosen, cand0):
    half = SUBLANES
    nrow = cand0.shape[0]
    den = jnp.sum(jnp.where(chosen > 0, jnp.exp(cand0 - cand0[0:1]), 0.0), axis=0, keepdims=True)
    lam = [jnp.sum(chosen[0:2 * half], axis=0, keepdims=True)]
    lam += [jnp.sum(chosen[2 * half + half * (r - 1):2 * half + half * r], axis=0, keepdims=True)
            for r in range(1, half)]
    lam += [chosen[nrow - half + r:nrow - half + r + 1] for r in range(half)]
    return lam, den


def _peer_select_kernel(x_ref, g_ref, wqt_ref, sk_ref, hb_ref, lam1_ref, rank2_ref, p1_ref, p2_ref, s_ref,
                        *, n_heads):
    h = _rms(x_ref[...], g_ref[...])
    hb = h.astype(BF16)
    hb_ref[...] = hb
    nk = sk_ref.shape[1]
    qt = lax.dot_general(wqt_ref[...], hb, (((1,), (1,)), ((), ())), preferred_element_type=F32)
    q_hi = qt.astype(BF16)
    q_lo = (qt - q_hi.astype(F32)).astype(BF16)
    for c in range(2 * n_heads):
        rows = slice(c * nk, (c + 1) * nk)
        q3 = jnp.concatenate([q_hi[rows], q_hi[rows], q_lo[rows]], axis=0)
        s_ref[c] = jnp.dot(sk_ref[c], q3, preferred_element_type=F32)

    def emit(hh, lanes, lam1, rank2, s1, s2, top1, top2, den):
        lam1_ref[hh, :, lanes] = lam1
        rank2_ref[hh, :, lanes] = rank2.astype(rank2_ref.dtype)
        p1_ref[hh, :, lanes] = jnp.exp(s1 - top1) / (2.0 * den)
        p2_ref[hh, :, lanes] = jnp.exp(s2 - top2).astype(p2_ref.dtype)

    for hh, lc in ((hh, lc) for hh in range(n_heads) for lc in range(x_ref.shape[0] // LANES)):
        lanes = slice(lc * LANES, (lc + 1) * LANES)
        s1 = s_ref[2 * hh, :, lanes]
        s2 = s_ref[2 * hh + 1, :, lanes]
        sv1, cov1 = _top16_values(s1)
        sv2, cov2 = _top16_values(s2)
        cand0, _ = _pair_cells(sv1, sv2)
        chosen = _pairs_fast(cand0)
        lam, den = _lam_den(chosen, cand0)
        lam1 = jnp.broadcast_to(lam[0], s1.shape)
        rank2 = jnp.zeros_like(s2)
        for r in range(PEER_TOPK):
            lam1 = jnp.where(sv1[r:r + 1] > s1, lam[r + 1] if r + 1 < PEER_TOPK else 0.0, lam1)
            rank2 = jnp.where(sv2[r:r + 1] > s2, float(r + 1), rank2)
        emit(hh, lanes, lam1, rank2, s1, s2, sv1[0:1], sv2[0:1], den)
        cov3 = jnp.sum(chosen, axis=0, keepdims=True)
        k = float(PEER_TOPK)
        tied = jnp.max(jnp.abs(cov1 - k) + jnp.abs(cov2 - k) + jnp.abs(cov3 - k)) > 0.0

        @pl.when(tied)
        def _(hh=hh, lanes=lanes):
            s1 = s_ref[2 * hh, :, lanes]
            s2 = s_ref[2 * hh + 1, :, lanes]
            rank1, sv1 = _top16_ranked(s1)
            rank2, sv2 = _top16_ranked(s2)
            cand0, idx = _pair_cells(sv1, sv2)
            lam, den = _lam_den(_pairs_exact(cand0, idx), cand0)
            lam1 = jnp.zeros_like(rank1)
            for r in range(PEER_TOPK):
                lam1 = jnp.where(rank1 == float(r), lam[r], lam1)
            emit(hh, lanes, lam1, rank2, s1, s2, sv1[0:1], sv2[0:1], den)


def _peer_select(x, g, wqt_bf, sk):
    n, d = x.shape
    n_heads = sk.shape[0] // 2
    nk = sk.shape[1]
    t = min(PEER_TOKENS, n)
    assert n % t == 0 and nk == N_KEYS
    full = lambda shp: pl.BlockSpec(shp, lambda i: (0,) * len(shp))
    sel = pl.BlockSpec((None, n_heads, nk, t), lambda i: (i, 0, 0, 0))
    sel_f32 = jax.ShapeDtypeStruct((n // t, n_heads, nk, t), F32)
    sel_bf16 = jax.ShapeDtypeStruct((n // t, n_heads, nk, t), BF16)
    return pl.pallas_call(
        functools.partial(_peer_select_kernel, n_heads=n_heads),
        grid=(n // t,),
        in_specs=[pl.BlockSpec((t, d), lambda i: (i, 0)), full((1, d)), full(wqt_bf.shape), full(sk.shape)],
        out_specs=[pl.BlockSpec((t, d), lambda i: (i, 0)), sel, sel, sel, sel],
        out_shape=[jax.ShapeDtypeStruct((n, d), BF16), sel_f32, sel_bf16, sel_f32, sel_bf16],
        scratch_shapes=[pltpu.VMEM((2 * n_heads, nk, t), F32)],
        compiler_params=_cparams(("arbitrary",)),
        name="peer_select",
    )(x, g, wqt_bf, sk)


def _gelu_x2(a):
    return a * (1.0 + lax.erf(a * (1.0 / math.sqrt(2.0))))


def _peer_dense_kernel(x_ref, hb_ref, u_ref, vt_ref, lam1_ref, rank2_ref, p1_ref, p2_ref, xo_ref,
                       acc_ref, a0_ref, a1_ref, z0_ref, z1_ref, *, n_heads, n_tiles):
    s = pl.program_id(0)
    n_items = pl.num_programs(0) - 2
    t = hb_ref.shape[0]
    nk = rank2_ref.shape[1]

    @pl.when(s == 0)
    def _():
        acc_ref[...] = jnp.zeros_like(acc_ref)
        a1_ref[...] = jnp.zeros_like(a1_ref)
        z0_ref[...] = jnp.zeros_like(z0_ref)

    n_grp = u_ref.shape[0] // nk
    eb = jnp.clip(s - 1, 0, n_items - 1) % n_tiles
    grp = pl.ds(pl.multiple_of(eb * n_grp, SUBLANES), n_grp)

    te = u_ref.shape[0]
    d = vt_ref.shape[0]
    n_tc = t // LANES

    def act_chunk(a_new, j):
        rows = slice(j * (te // MXU_CHUNKS), (j + 1) * (te // MXU_CHUNKS))
        a_new[rows, :] = lax.dot_general(u_ref[rows, :], hb_ref[...], (((1,), (1,)), ((), ())),
                                         preferred_element_type=F32)

    def val_chunk(z_old, j):
        rows = slice(j * (d // MXU_CHUNKS), (j + 1) * (d // MXU_CHUNKS))
        acc_ref[rows, :] += jnp.dot(vt_ref[rows, :], z_old[...], preferred_element_type=F32)

    def gate_chunk(a_cur, z_new, c):
        ii, tc = divmod(c, n_tc)
        lanes = slice(tc * LANES, (tc + 1) * LANES)
        rows = slice(ii * nk, (ii + 1) * nk)
        gate = jnp.zeros((nk, LANES), BF16)
        for hh in range(n_heads):
            l1 = jnp.broadcast_to(lam1_ref[hh, grp, lanes][ii:ii + 1], (nk, LANES)).astype(BF16)
            pb = jnp.broadcast_to(p1_ref[hh, grp, lanes][ii:ii + 1], (nk, LANES)).astype(BF16)
            gate = gate + jnp.where(l1 > rank2_ref[hh, :, lanes], pb * p2_ref[hh, :, lanes],
                                    jnp.zeros((), BF16))
        z_new[rows, lanes] = gate * _gelu_x2(a_cur[rows, lanes]).astype(BF16)

    def step(a_new, a_cur, z_new, z_old):
        n_gate = n_grp * n_tc
        c = 0
        for k in range(2 * MXU_CHUNKS):
            if k % 2 == 0:
                act_chunk(a_new, k // 2)
            else:
                val_chunk(z_old, k // 2)
            while c < (k + 1) * n_gate // (2 * MXU_CHUNKS):
                gate_chunk(a_cur, z_new, c)
                c += 1

    @pl.when(s % 2 == 0)
    def _():
        step(a0_ref, a1_ref, z1_ref, z0_ref)

    @pl.when(s % 2 == 1)
    def _():
        step(a1_ref, a0_ref, z0_ref, z1_ref)

    @pl.when((s >= 2) & ((s - 2) % n_tiles == n_tiles - 1))
    def _():
        xo_ref[...] = x_ref[...] + acc_ref[...].T
        acc_ref[...] = jnp.zeros_like(acc_ref)


def _peer_dense(x, hb, u_all, vt_all, layer, lam1, rank2, p1, p2):
    n, d = x.shape
    n_exp = u_all.shape[1]
    _, n_heads, nk, t = lam1.shape
    te = DENSE_EXPERTS
    assert n % t == 0 and n_exp % te == 0 and te % (SUBLANES * nk) == 0 and n_exp == nk * nk and t % LANES == 0
    ne = n_exp // te
    assert vt_all.shape[1:] == (ne, d, te)
    n_items = (n // t) * ne
    assert ne >= 2
    item = lambda s, lag: jnp.clip(s - lag, 0, n_items - 1)
    tok = lambda lag: pl.BlockSpec((t, d), lambda s: (item(s, lag) // ne, 0))
    sel = pl.BlockSpec((None, n_heads, nk, t), lambda s: (item(s, 1) // ne, 0, 0, 0))
    return pl.pallas_call(
        functools.partial(_peer_dense_kernel, n_heads=n_heads, n_tiles=ne),
        grid=(n_items + 2,),
        in_specs=[tok(2), tok(0),
                  pl.BlockSpec((None, te, d), lambda s: (layer, item(s, 0) % ne, 0)),
                  pl.BlockSpec((None, None, d, te), lambda s: (layer, item(s, 2) % ne, 0, 0)),
                  sel, sel, sel, sel],
        out_specs=tok(2),
        out_shape=jax.ShapeDtypeStruct((n, d), F32),
        scratch_shapes=[pltpu.VMEM((d, t), F32), pltpu.VMEM((te, t), F32), pltpu.VMEM((te, t), F32),
                        pltpu.VMEM((te, t), BF16), pltpu.VMEM((te, t), BF16)],
        compiler_params=_cparams(("arbitrary",)),
        name="peer_dense",
    )(x, hb, u_all, vt_all, lam1, rank2, p1, p2)


def _peer_tables_kernel(u_ref, v_ref, ub_ref, vt_ref):
    ub_ref[...] = u_ref[...].astype(BF16)
    vt_ref[...] = v_ref[...].T.astype(BF16)


def _peer_tables(peer_u, peer_v):
    n_l, n_exp, d = peer_u.shape
    te = ROW_BLOCK
    per = DENSE_EXPERTS // te
    assert n_exp % DENSE_EXPERTS == 0 and DENSE_EXPERTS % te == 0
    src = pl.BlockSpec((None, te, d), lambda l, j: (l, j, 0))
    return pl.pallas_call(
        _peer_tables_kernel,
        grid=(n_l, n_exp // te),
        in_specs=[src, src],
        out_specs=[src, pl.BlockSpec((None, None, d, te), lambda l, j: (l, j // per, 0, j % per))],
        out_shape=[jax.ShapeDtypeStruct((n_l, n_exp, d), BF16),
                   jax.ShapeDtypeStruct((n_l, n_exp // DENSE_EXPERTS, d, DENSE_EXPERTS), BF16)],
        compiler_params=_cparams(("arbitrary", "arbitrary")),
        name="peer_tables",
    )(peer_u, peer_v)


def _peer(x, g, wqt_bf, sk, u_all, vt_all, layer):
    n = x.shape[0]
    pad = (-n) % LANES
    xp = jnp.pad(x, ((0, pad), (0, 0))) if pad else x
    hb, lam1, rank2, p1, p2 = _peer_select(xp, g, wqt_bf, sk)
    out = _peer_dense(xp, hb, u_all, vt_all, layer, lam1, rank2, p1, p2)
    return out[:n] if pad else out


def _final_norm_kernel(x_ref, g_ref, o_ref):
    o_ref[...] = _rms(x_ref[...], g_ref[...])


def _final_norm(x, g):
    n, d = x.shape
    tm = min(ROW_BLOCK, n)
    assert n % tm == 0
    return pl.pallas_call(
        _final_norm_kernel,
        grid=(n // tm,),
        in_specs=[pl.BlockSpec((tm, d), lambda i: (i, 0)), pl.BlockSpec((1, d), lambda i: (0, 0))],
        out_specs=pl.BlockSpec((tm, d), lambda i: (i, 0)),
        out_shape=jax.ShapeDtypeStruct((n, d), F32),
        compiler_params=_cparams(("arbitrary",)),
        name="final_norm",
    )(x, g)


def kernel(x_prompt, x_sample, state_conv, state_win_k, state_win_v, state_pool, norm_mix, w_in_ab, conv_w,
           w_out_ab, pool_w, pool_b, pool_scale, norm_ffn, peer_wq, peer_subkeys, peer_u, peer_v, norm_final):
    b, s, d = x_prompt.shape
    bs, ts, _ = x_sample.shape
    assert ts == 1
    depth = norm_mix.shape[0]
    d_conv = conv_w.shape[2]
    n_heads_b, hd = state_win_k.shape[3], state_win_k.shape[4]
    d_att = n_heads_b * hd
    wb_s = state_win_k.shape[2]
    wb_p = min(max(w for w, _ in DILATED_PAIRS), s)
    pool_state = state_pool.shape[2]
    q_scale = float(hd) ** -0.5
    n_peer_heads = peer_subkeys.shape[1]

    xp = x_prompt.reshape(b * s, d)
    xs = x_sample.reshape(bs, d)
    u_all, vt_all = _peer_tables(peer_u, peer_v)
    conv_p, conv_s, wk_p, wk_s, wv_p, wv_s, pool_p, pool_s = [], [], [], [], [], [], [], []
    for l in range(depth):
        g_mix = norm_mix[l][None]
        if l % 2 == 0:
            e = l // 2
            w_in = w_in_ab[e].astype(BF16)
            wa = w_out_ab[e, :d_conv].astype(BF16)
            wbm = w_out_ab[e, d_conv:].astype(BF16)
            ya, k, v, ul, qkv = _even_in_prompt(xp.reshape(b, s, d), g_mix, w_in, conv_w[e], d_conv, d_att, q_scale)
            branches = [_attn_branch(*qkv[dl], w, dl, n_heads_b) for w, dl in DILATED_PAIRS]
            o_list = [br[0].reshape(b * s // dl, dl * d_att) for br, (_, dl) in zip(branches, DILATED_PAIRS)]
            l_list = [br[1].reshape(b * s // dl, dl * d_att) for br, (_, dl) in zip(branches, DILATED_PAIRS)]
            xp = _out_proj(xp, ya.reshape(b * s, d_conv), o_list + l_list, wa, wbm)
            conv_p.append(ul[:, SUBLANES - 2:])
            wk_p.append(k[:, s - wb_p:].reshape(b, wb_p, n_heads_b, hd))
            wv_p.append(v[:, s - wb_p:].reshape(b, wb_p, n_heads_b, hd))
            cst = state_conv[e]
            ya_s, q_s, k_s, v_s, u_s = _even_in_sample(xs, g_mix, w_in, conv_w[e], cst[:, 1], cst[:, 0],
                                                       d_conv, d_att, q_scale)
            yb_s = _attn_sample(q_s, k_s, v_s, state_win_k, state_win_v, e)
            xs = _out_proj(xs, ya_s, [yb_s], wa, wbm)
            conv_s.append(jnp.stack([cst[:, 1], u_s], axis=1))
            wk_s.append(k_s.reshape(bs, 1, n_heads_b, hd))
            wv_s.append(v_s.reshape(bs, 1, n_heads_b, hd))
        else:
            o = l // 2
            pw = pool_w[o].astype(BF16)
            xp3, hl = _pool_prompt(xp.reshape(b, s, d), g_mix, pw, pool_b[o], pool_scale[o][None])
            xp = xp3.reshape(b * s, d)
            pool_p.append(hl[:, hl.shape[1] - pool_state:])
            st = state_pool[o]
            xs, h_s = _pool_sample(xs, jnp.swapaxes(st, 0, 1), g_mix, pw, pool_b[o], pool_scale[o][None])
            pool_s.append(jnp.concatenate([st[:, 1:], h_s[:, None]], axis=1))
        g_ffn = norm_ffn[l][None]
        wqt = peer_wq[l].T.astype(BF16)
        sk = peer_subkeys[l].reshape(2 * n_peer_heads, N_KEYS, -1)
        sk_hi = sk.astype(BF16)
        sk_lo = (sk - sk_hi.astype(F32)).astype(BF16)
        sk = jnp.concatenate([sk_hi, sk_lo, sk_hi], axis=-1)
        xp = _peer(xp, g_ffn, wqt, sk, u_all, vt_all, l)
        xs = _peer(xs, g_ffn, wqt, sk, u_all, vt_all, l)
    gf = norm_final[None]
    y_prompt = _final_norm(xp, gf).reshape(b, s, d)
    y_sample = _final_norm(xs, gf).reshape(bs, ts, d)
    win_k_s = jnp.concatenate([state_win_k[:, :, 1:], jnp.stack(wk_s)], axis=2)
    win_v_s = jnp.concatenate([state_win_v[:, :, 1:], jnp.stack(wv_s)], axis=2)
    return (y_prompt, y_sample, jnp.stack(conv_p), jnp.stack(conv_s), jnp.stack(wk_p), win_k_s,
            jnp.stack(wv_p), win_v_s, jnp.stack(pool_p), jnp.stack(pool_s))
```

```python
import functools
import math

import jax
import jax.numpy as jnp
from jax import lax
from jax.experimental import pallas as pl
from jax.experimental.pallas import tpu as pltpu

F32 = jnp.float32
BF16 = jnp.bfloat16

NORM_EPS = 1e-6
NEG_INF = -1e30
DILATED_PAIRS = ((128, 1), (512, 4), (2048, 16))
POOL_WINDOWS = (2, 4, 8, 16)
PEER_TOPK = 16
N_KEYS = 128

LANES = 128
SUBLANES = 8
VMEM_LIMIT = 56 * 1024 * 1024

ATT_BLOCK = 128
ROW_BLOCK = 512
PEER_TOKENS = 256
MXU_CHUNKS = 4
DENSE_KEY_GROUP = 16
DENSE_EXPERTS = DENSE_KEY_GROUP * N_KEYS


def _cparams(sem, flags=None):
    return pltpu.CompilerParams(dimension_semantics=sem, vmem_limit_bytes=VMEM_LIMIT, flags=flags)


def _rms(x, g):
    r = lax.rsqrt(jnp.mean(x * x, axis=-1, keepdims=True) + NORM_EPS)
    return (x * r) * g


def _bdot(a, b):
    return jnp.dot(a.astype(BF16), b.astype(BF16), preferred_element_type=F32)


def _bdot_nt(a, b):
    return lax.dot_general(a.astype(BF16), b.astype(BF16), (((1,), (1,)), ((), ())),
                           preferred_element_type=F32)


def _even_in_body(x, g, w, cw, u1_fn, d_conv, d_att, q_scale):
    h = _rms(x, g)
    p = _bdot(h, w)
    gate_b = p[:, 0:d_conv]
    gate_c = p[:, d_conv:2 * d_conv]
    xv = p[:, 2 * d_conv:3 * d_conv]
    o = 3 * d_conv
    q = p[:, o:o + d_att] * q_scale
    k = p[:, o + d_att:o + 2 * d_att]
    v = p[:, o + 2 * d_att:o + 3 * d_att]
    u = gate_c * xv
    u1, u2 = u1_fn(u)
    y = cw[0:1] * u2 + cw[1:2] * u1 + cw[2:3] * u
    return gate_b * y, q, k, v, u


def _even_in_seq_kernel(x_ref, g_ref, w_ref, cw_ref, ya_ref, q_ref, k_ref, v_ref, ul_ref, *rest,
                        d_conv, d_att, q_scale):
    dil_refs, carry_ref, stage_ref = rest[:-2], rest[-2], rest[-1]
    s = pl.program_id(1)

    @pl.when(s == 0)
    def _():
        carry_ref[...] = jnp.zeros_like(carry_ref)

    prev = carry_ref[...]
    tm = x_ref.shape[0]

    def shifted(u):
        rows = lax.broadcasted_iota(jnp.int32, u.shape, 0)
        u1 = jnp.where(rows == 0, prev[7:8], pltpu.roll(u, 1, 0))
        u2 = pltpu.roll(u, 2, 0)
        u2 = jnp.where(rows == 0, prev[6:7], jnp.where(rows == 1, prev[7:8], u2))
        return u1, u2

    ya, q, k, v, u = _even_in_body(x_ref[...], g_ref[...], w_ref[...], cw_ref[...], shifted,
                                   d_conv, d_att, q_scale)
    ya_ref[...] = ya
    q_ref[...] = q
    k_ref[...] = k
    v_ref[...] = v
    last = u[tm - SUBLANES:tm]
    carry_ref[...] = last
    ul_ref[...] = last
    n_cb = d_att // LANES
    for j, val in enumerate((q, k, v)):
        for c in range(n_cb):
            stage_ref[j, c] = val[:, c * LANES:(c + 1) * LANES]
    for i, ref in enumerate(dil_refs):
        rows = ref.shape[0]
        dil = tm // rows
        for r in range(dil):
            for c in range(n_cb):
                col = r * d_att + c * LANES
                ref[:, col:col + LANES] = stage_ref[i % 3, c, pl.ds(r, rows, stride=dil), :]


def _even_in_rows_kernel(x_ref, g_ref, w_ref, cw_ref, u1_ref, u2_ref, ya_ref, q_ref, k_ref, v_ref, u_ref,
                         *, d_conv, d_att, q_scale):
    ya, q, k, v, u = _even_in_body(x_ref[...], g_ref[...], w_ref[...], cw_ref[...],
                                   lambda _: (u1_ref[...], u2_ref[...]), d_conv, d_att, q_scale)
    ya_ref[...] = ya
    q_ref[...] = q
    k_ref[...] = k
    v_ref[...] = v
    u_ref[...] = u


def _even_in_prompt(x, g, w_bf, cw, d_conv, d_att, q_scale):
    b, s, d = x.shape
    tm = min(ROW_BLOCK, s)
    assert s % tm == 0 and tm % SUBLANES == 0
    ncol = w_bf.shape[1]
    row = lambda c: pl.BlockSpec((None, tm, c), lambda i, j: (i, j, 0))
    full = lambda shp: pl.BlockSpec(shp, lambda i, j: (0,) * len(shp))
    dils = [dl for _, dl in DILATED_PAIRS if dl > 1]
    assert all(tm % (dl * SUBLANES) == 0 for dl in dils)
    dil_specs = [pl.BlockSpec((None, tm // dl, dl * d_att), lambda i, j: (i, j, 0)) for dl in dils for _ in range(3)]
    dil_shapes = [jax.ShapeDtypeStruct((b, s // dl, dl * d_att), F32) for dl in dils for _ in range(3)]
    outs = pl.pallas_call(
        functools.partial(_even_in_seq_kernel, d_conv=d_conv, d_att=d_att, q_scale=q_scale),
        grid=(b, s // tm),
        in_specs=[row(d), full((1, d)), full((d, ncol)), full((cw.shape[0], d_conv))],
        out_specs=[row(d_conv), row(d_att), row(d_att), row(d_att),
                   pl.BlockSpec((None, SUBLANES, d_conv), lambda i, j: (i, 0, 0))] + dil_specs,
        out_shape=[jax.ShapeDtypeStruct((b, s, d_conv), F32)] + [jax.ShapeDtypeStruct((b, s, d_att), F32)] * 3
                  + [jax.ShapeDtypeStruct((b, SUBLANES, d_conv), F32)] + dil_shapes,
        scratch_shapes=[pltpu.VMEM((SUBLANES, d_conv), F32), pltpu.VMEM((3, d_att // LANES, tm, LANES), F32)],
        compiler_params=_cparams(("arbitrary", "arbitrary")),
        name="even_in_prompt",
    )(x, g, w_bf, cw)
    ya, q, k, v, ul = outs[:5]
    qkv = {1: (q, k, v)}
    for i, dl in enumerate(dils):
        qkv[dl] = tuple(outs[5 + 3 * i:8 + 3 * i])
    return ya, k, v, ul, qkv


def _even_in_sample(x, g, w_bf, cw, u1, u2, d_conv, d_att, q_scale):
    n, d = x.shape
    ncol = w_bf.shape[1]
    full = lambda shp: pl.BlockSpec(shp, lambda i: (0,) * len(shp))
    return pl.pallas_call(
        functools.partial(_even_in_rows_kernel, d_conv=d_conv, d_att=d_att, q_scale=q_scale),
        grid=(1,),
        in_specs=[full((n, d)), full((1, d)), full((d, ncol)), full((cw.shape[0], d_conv)),
                  full((n, d_conv)), full((n, d_conv))],
        out_specs=[full((n, d_conv)), full((n, d_att)), full((n, d_att)), full((n, d_att)), full((n, d_conv))],
        out_shape=[jax.ShapeDtypeStruct((n, d_conv), F32)] + [jax.ShapeDtypeStruct((n, d_att), F32)] * 3
                  + [jax.ShapeDtypeStruct((n, d_conv), F32)],
        compiler_params=_cparams(("arbitrary",)),
        name="even_in_sample",
    )(x, g, w_bf, cw, u1, u2)


def _attn_branch_kernel(q_ref, kp_ref, kc_ref, vp_ref, vc_ref, o_ref, l_ref, *, n_heads, hd, n_back):
    n = pl.program_id(2)
    blk = q_ref.shape[0]
    q = q_ref[...]
    k = jnp.concatenate([kp_ref[...], kc_ref[...]], axis=0)
    v = jnp.concatenate([vp_ref[...], vc_ref[...]], axis=0)
    qi = lax.broadcasted_iota(jnp.int32, (blk, 2 * blk), 0)
    ki = lax.broadcasted_iota(jnp.int32, (blk, 2 * blk), 1)
    dist = qi + blk - ki
    has_prev = jnp.where(n > 0, 0, blk)
    mask = (dist >= 0) & (dist <= n_back) & (ki >= has_prev)
    o_parts, l_parts = [], []
    for h in range(n_heads):
        sl = slice(h * hd, (h + 1) * hd)
        s = _bdot_nt(q[:, sl], k[:, sl])
        s = jnp.where(mask, s, NEG_INF)
        m = jnp.max(s, axis=-1, keepdims=True)
        p = jnp.exp(s - m)
        den = jnp.sum(p, axis=-1, keepdims=True)
        o_parts.append(_bdot(p, v[:, sl]) / den)
        l_parts.append(jnp.broadcast_to(m + jnp.log(den), (blk, hd)))
    o_ref[...] = jnp.concatenate(o_parts, axis=1)
    l_ref[...] = jnp.concatenate(l_parts, axis=1)


def _attn_branch(q, k, v, window, dil, n_heads):
    b, L, dda = q.shape
    da = dda // dil
    hd = da // n_heads
    n_back = window // dil
    assert L % ATT_BLOCK == 0 and n_back <= ATT_BLOCK
    nb = L // ATT_BLOCK
    cur = pl.BlockSpec((None, ATT_BLOCK, da), lambda i, r, n: (i, n, r))
    prev = pl.BlockSpec((None, ATT_BLOCK, da), lambda i, r, n: (i, jnp.maximum(n - 1, 0), r))
    o, l = pl.pallas_call(
        functools.partial(_attn_branch_kernel, n_heads=n_heads, hd=hd, n_back=n_back),
        grid=(b, dil, nb),
        in_specs=[cur, prev, cur, prev, cur],
        out_specs=[cur, cur],
        out_shape=[jax.ShapeDtypeStruct((b, L, dil * da), F32)] * 2,
        compiler_params=_cparams(("arbitrary", "arbitrary", "arbitrary")),
        name=f"attn_branch_d{dil}",
    )(q, k, k, v, v)
    return o, l


def _attn_sample_kernel(q_ref, kn_ref, vn_ref, *refs):
    n_br = (len(refs) - 1) // 2
    k_refs, v_refs, y_ref = refs[:n_br], refs[n_br:2 * n_br], refs[-1]
    q = q_ref[...][None]
    kn = kn_ref[...][None]
    vn = vn_ref[...][None]
    s0 = jnp.sum(kn * q, axis=-1, keepdims=True)
    o_list, l_list = [], []
    for g in range(n_br):
        s = jnp.sum(k_refs[g][...] * q, axis=-1, keepdims=True)
        m = jnp.maximum(jnp.max(s, axis=0, keepdims=True), s0)
        p = jnp.exp(s - m)
        p0 = jnp.exp(s0 - m)
        den = jnp.sum(p, axis=0, keepdims=True) + p0
        o_list.append((jnp.sum(p * v_refs[g][...], axis=0, keepdims=True) + p0 * vn) / den)
        l_list.append(m + jnp.log(den))
    y_ref[...] = _merge_branches(o_list, l_list)[0]


def _merge_branches(o_list, l_list):
    m = functools.reduce(jnp.maximum, l_list)
    e = [jnp.exp(l - m) for l in l_list]
    num = functools.reduce(lambda a, b: a + b, [ei * oi for ei, oi in zip(e, o_list)])
    return num / functools.reduce(lambda a, b: a + b, e)


def _attn_sample(q, k_new, v_new, k_state_all, v_state_all, layer):
    b, da = q.shape
    n_l, _, wb, n_heads, hd = k_state_all.shape
    row = pl.BlockSpec((None, n_heads, hd), lambda i: (i, 0, 0))
    ins, specs = [], []
    for st in (k_state_all, v_state_all):
        for window, dil in DILATED_PAIRS:
            n_back = window // dil
            assert n_back * dil <= wb and wb % dil == 0 and (wb // dil) % n_back == 0
            L = wb // dil
            ins.append(st.reshape(n_l, b, L, dil, n_heads, hd))
            specs.append(pl.BlockSpec((None, None, n_back, None, n_heads, hd),
                                      lambda i, L=L, nbk=n_back: (layer, i, L // nbk - 1, 0, 0, 0)))
    as_heads = lambda t: t.reshape(b, n_heads, hd)
    y = pl.pallas_call(
        _attn_sample_kernel,
        grid=(b,),
        in_specs=[row, row, row] + specs,
        out_specs=row,
        out_shape=jax.ShapeDtypeStruct((b, n_heads, hd), F32),
        compiler_params=_cparams(("arbitrary",)),
        name="attn_sample",
    )(as_heads(q), as_heads(k_new), as_heads(v_new), *ins)
    return y.reshape(b, da)


def _out_proj_kernel(x_ref, ya_ref, *refs, n_in):
    br = refs[:n_in]
    wa_ref, wb_ref, xo_ref = refs[n_in:n_in + 3]
    scratch = list(refs[n_in + 3:])
    tm, da = ya_ref.shape[0], wb_ref.shape[0]

    def token_order(ref):
        rows = ref.shape[0]
        if rows == tm:
            return ref[...]
        dil = tm // rows
        sc = scratch.pop(0)
        n_cb = da // LANES
        for r in range(dil):
            for c in range(n_cb):
                col = r * da + c * LANES
                sc[c, pl.ds(r, rows, stride=dil), :] = ref[:, col:col + LANES]
        return jnp.concatenate([sc[c] for c in range(n_cb)], axis=1)

    vals = [token_order(r) for r in br]
    yb = vals[0] if n_in == 1 else _merge_branches(vals[:n_in // 2], vals[n_in // 2:])
    xo_ref[...] = x_ref[...] + _bdot(ya_ref[...], wa_ref[...]) + _bdot(yb, wb_ref[...])


def _out_proj(x, ya, branch_arrays, wa_bf, wb_bf):
    n, d = x.shape
    da = wb_bf.shape[0]
    tm = min(ROW_BLOCK, n)
    assert n % tm == 0
    row = lambda c: pl.BlockSpec((tm, c), lambda i: (i, 0))
    full = lambda shp: pl.BlockSpec(shp, lambda i: (0,) * len(shp))
    dils = [a.shape[1] // da for a in branch_arrays]
    assert all(a.shape == (n // dl, dl * da) and (dl == 1 or tm % (dl * SUBLANES) == 0)
               for a, dl in zip(branch_arrays, dils))
    return pl.pallas_call(
        functools.partial(_out_proj_kernel, n_in=len(branch_arrays)),
        grid=(n // tm,),
        in_specs=[row(d), row(ya.shape[1])]
                 + [pl.BlockSpec((tm // dl, dl * da), lambda i: (i, 0)) for dl in dils]
                 + [full(wa_bf.shape), full(wb_bf.shape)],
        out_specs=row(d),
        out_shape=jax.ShapeDtypeStruct((n, d), F32),
        scratch_shapes=[pltpu.VMEM((da // LANES, tm, LANES), F32) for dl in dils if dl > 1],
        compiler_params=_cparams(("arbitrary",)),
        name="out_proj",
    )(x, ya, *branch_arrays, wa_bf, wb_bf)


def _pool_groups(dmat, w_ref, b_ref, sc):
    n_g = w_ref.shape[0]
    gw = w_ref.shape[1]
    ys = [_bdot(dmat[g], w_ref[g]) + b_ref[g:g + 1] for g in range(n_g)]
    return jnp.concatenate(ys, axis=1) * sc


def _pool_prompt_kernel(x_ref, g_ref, w_ref, b_ref, sc_ref, xo_ref, hl_ref, carry_ref):
    s = pl.program_id(1)
    hist = carry_ref.shape[0]

    @pl.when(s == 0)
    def _():
        carry_ref[...] = jnp.zeros_like(carry_ref)

    x = x_ref[...]
    tm = x.shape[0]
    h = _rms(x, g_ref[...])
    ext = jnp.concatenate([carry_ref[...], h], axis=0)
    pos = s * tm + lax.broadcasted_iota(jnp.int32, (tm, 1), 0) + 1
    gw = w_ref.shape[1]
    acc = ext
    width = 1
    diffs = []
    for g, w in enumerate(POOL_WINDOWS):
        while width < w:
            acc = acc + pltpu.roll(acc, width, 0)
            width *= 2
        cols = slice(g * gw, (g + 1) * gw)
        win = acc[hist:, cols]
        div = jnp.minimum(pos, w).astype(F32)
        diffs.append(win / div - h[:, cols])
    xo_ref[...] = x + _pool_groups(diffs, w_ref, b_ref, sc_ref[...])
    last = ext[tm:tm + hist]
    carry_ref[...] = last
    hl_ref[...] = last


def _pool_prompt(x, g, w_bf, bias, scale):
    b, s, d = x.shape
    tm = min(ROW_BLOCK, s)
    hist = 16
    assert s % tm == 0 and tm >= hist and max(POOL_WINDOWS) <= hist
    assert all(w == 2 ** (i + 1) for i, w in enumerate(POOL_WINDOWS))
    row = pl.BlockSpec((None, tm, d), lambda i, j: (i, j, 0))
    full = lambda shp: pl.BlockSpec(shp, lambda i, j: (0,) * len(shp))
    return pl.pallas_call(
        _pool_prompt_kernel,
        grid=(b, s // tm),
        in_specs=[row, full((1, d)), full(w_bf.shape), full(bias.shape), full((1, d))],
        out_specs=[row, pl.BlockSpec((None, hist, d), lambda i, j: (i, 0, 0))],
        out_shape=[jax.ShapeDtypeStruct((b, s, d), F32), jax.ShapeDtypeStruct((b, hist, d), F32)],
        scratch_shapes=[pltpu.VMEM((hist, d), F32)],
        compiler_params=_cparams(("arbitrary", "arbitrary")),
        name="pool_prompt",
    )(x, g, w_bf, bias, scale)


def _pool_sample_kernel(x_ref, st_ref, g_ref, w_ref, b_ref, sc_ref, xo_ref, h_ref):
    x = x_ref[...]
    h = _rms(x, g_ref[...])
    n_st = st_ref.shape[0]
    gw = w_ref.shape[1]
    diffs = []
    for g, w in enumerate(POOL_WINDOWS):
        cols = slice(g * gw, (g + 1) * gw)
        tot = h[:, cols]
        for j in range(1, w):
            tot = tot + st_ref[n_st - j][:, cols]
        diffs.append(tot / float(w) - h[:, cols])
    xo_ref[...] = x + _pool_groups(diffs, w_ref, b_ref, sc_ref[...])
    h_ref[...] = h


def _pool_sample(x, state_t, g, w_bf, bias, scale):
    n, d = x.shape
    assert state_t.shape[0] + 1 >= max(POOL_WINDOWS)
    full = lambda shp: pl.BlockSpec(shp, lambda i: (0,) * len(shp))
    return pl.pallas_call(
        _pool_sample_kernel,
        grid=(1,),
        in_specs=[full(x.shape), full(state_t.shape), full((1, d)), full(w_bf.shape), full(bias.shape),
                  full((1, d))],
        out_specs=[full(x.shape), full(x.shape)],
        out_shape=[jax.ShapeDtypeStruct((n, d), F32)] * 2,
        compiler_params=_cparams(("arbitrary",)),
        name="pool_sample",
    )(x, state_t, g, w_bf, bias, scale)


def _top16_ranked(s):
    n, t = s.shape
    rows = lax.broadcasted_iota(jnp.int32, (n, t), 0).astype(F32)
    rows16 = lax.broadcasted_iota(jnp.int32, (PEER_TOPK, t), 0)

    def step(it, carry):
        s, rank, sv = carry
        m = jnp.max(s, axis=0, keepdims=True)
        first = jnp.min(jnp.where(s == m, rows, float(n)), axis=0, keepdims=True)
        sel = rows == first
        itf = jnp.asarray(it, jnp.int32).astype(F32)
        return (jnp.where(sel, -jnp.inf, s), jnp.where(sel, itf, rank), jnp.where(rows16 == it, m, sv))

    init = (s, jnp.full((n, t), float(PEER_TOPK), F32), jnp.zeros((PEER_TOPK, t), F32))
    _, rank, sv = lax.fori_loop(0, PEER_TOPK, step, init)
    return rank, sv


def _sort_network(n_in):
    n, pairs, p = PEER_TOPK, [], 1
    assert n_in <= n
    while p < n:
        k = p
        while k >= 1:
            for j in range(k % p, n - k, 2 * k):
                for i in range(min(k, n - j - k)):
                    if (i + j) // (2 * p) == (i + j + k) // (2 * p):
                        pairs.append((i + j, i + j + k))
            k //= 2
        p *= 2
    return [(i, j) for i, j in pairs if j < n_in]


def _largest_distinct(groups, k):
    v = list(groups)
    n_g = len(v)
    for i, j in _sort_network(n_g):
        v[i], v[j] = jnp.maximum(v[i], v[j]), jnp.minimum(v[i], v[j])
    out = []
    for it in range(k):
        m = jnp.max(v[0], axis=0, keepdims=True)
        out.append(m)
        keep = min(n_g, k - it)
        pop = v[0] == m
        for j in range(keep - 1):
            v[j] = jnp.where(pop, v[j + 1], v[j])
        if keep == n_g and keep > 1:
            v[n_g - 1] = jnp.where(pop, -jnp.inf, v[n_g - 1])
    return out


def _top16_values(s):
    n, t = s.shape
    tops = _largest_distinct([s[SUBLANES * j:SUBLANES * (j + 1)] for j in range(n // SUBLANES)], PEER_TOPK)
    rows16 = lax.broadcasted_iota(jnp.int32, (PEER_TOPK, t), 0)
    sv = jnp.zeros((PEER_TOPK, t), F32)
    for it, m in enumerate(tops):
        sv = jnp.where(rows16 == it, m, sv)
    cover = jnp.sum(jnp.where(s >= tops[-1], 1.0, 0.0), axis=0, keepdims=True)
    return sv, cover


def _pair_cells(sv1, sv2):
    t = sv1.shape[1]
    half = SUBLANES
    tiles = [sv1[0:1] + sv2[0:half], sv1[0:1] + sv2[half:2 * half]]
    tiles += [sv1[r:r + 1] + sv2[0:half] for r in range(1, half)]
    tiles += [sv1[half:2 * half] + sv2[0:1]]
    cand0 = jnp.concatenate(tiles, axis=0)
    nrow = cand0.shape[0]
    i = lax.broadcasted_iota(jnp.int32, (nrow, t), 0)
    j = i - 2 * half
    mid = (lax.shift_right_arithmetic(j, 3) + 1) * PEER_TOPK + lax.bitwise_and(j, half - 1)
    idx = jnp.where(i < 2 * half, i,
                    jnp.where(i < nrow - half, mid, (i - (nrow - half) + half) * PEER_TOPK)).astype(F32)
    return cand0, idx


def _pairs_exact(cand0, idx):
    def step(_, carry):
        cand, chosen = carry
        m = jnp.max(cand, axis=0, keepdims=True)
        first = jnp.min(jnp.where(cand == m, idx, 1e9), axis=0, keepdims=True)
        sel = idx == first
        return jnp.where(sel, -jnp.inf, cand), jnp.where(sel, 1.0, chosen)

    _, chosen = lax.fori_loop(0, PEER_TOPK, step, (cand0, jnp.zeros_like(cand0)))
    return chosen


def _pairs_fast(cand0):
    tiles = [cand0[SUBLANES * j:SUBLANES * (j + 1)] for j in range(cand0.shape[0] // SUBLANES)]
    return jnp.where(cand0 >= _largest_distinct(tiles, PEER_TOPK)[-1], 1.0, 0.0)


def _lam_den(chosen, cand0):
    half = SUBLANES
    nrow = cand0.shape[0]
    den = jnp.sum(jnp.where(chosen > 0, jnp.exp(cand0 - cand0[0:1]), 0.0), axis=0, keepdims=True)
    lam = [jnp.sum(chosen[0:2 * half], axis=0, keepdims=True)]
    lam += [jnp.sum(chosen[2 * half + half * (r - 1):2 * half + half * r], axis=0, keepdims=True)
            for r in range(1, half)]
    lam += [chosen[nrow - half + r:nrow - half + r + 1] for r in range(half)]
    return lam, den


def _peer_select_kernel(x_ref, g_ref, wqt_ref, sk_ref, hb_ref, lam1_ref, rank2_ref, p1_ref, p2_ref, s_ref,
                        *, n_heads):
    h = _rms(x_ref[...], g_ref[...])
    hb = h.astype(BF16)
    hb_ref[...] = hb
    nk = sk_ref.shape[1]
    qt = lax.dot_general(wqt_ref[...], hb, (((1,), (1,)), ((), ())), preferred_element_type=F32)
    q_hi = qt.astype(BF16)
    q_lo = (qt - q_hi.astype(F32)).astype(BF16)
    for c in range(2 * n_heads):
        rows = slice(c * nk, (c + 1) * nk)
        q3 = jnp.concatenate([q_hi[rows], q_hi[rows], q_lo[rows]], axis=0)
        s_ref[c] = jnp.dot(sk_ref[c], q3, preferred_element_type=F32)

    def emit(hh, lanes, lam1, rank2, s1, s2, top1, top2, den):
        lam1_ref[hh, :, lanes] = lam1
        rank2_ref[hh, :, lanes] = rank2.astype(rank2_ref.dtype)
        p1_ref[hh, :, lanes] = jnp.exp(s1 - top1) / (2.0 * den)
        p2_ref[hh, :, lanes] = jnp.exp(s2 - top2).astype(p2_ref.dtype)

    for hh, lc in ((hh, lc) for hh in range(n_heads) for lc in range(x_ref.shape[0] // LANES)):
        lanes = slice(lc * LANES, (lc + 1) * LANES)
        s1 = s_ref[2 * hh, :, lanes]
        s2 = s_ref[2 * hh + 1, :, lanes]
        sv1, cov1 = _top16_values(s1)
        sv2, cov2 = _top16_values(s2)
        cand0, _ = _pair_cells(sv1, sv2)
        chosen = _pairs_fast(cand0)
        lam, den = _lam_den(chosen, cand0)
        lam1 = jnp.broadcast_to(lam[0], s1.shape)
        rank2 = jnp.zeros_like(s2)
        for r in range(PEER_TOPK):
            lam1 = jnp.where(sv1[r:r + 1] > s1, lam[r + 1] if r + 1 < PEER_TOPK else 0.0, lam1)
            rank2 = jnp.where(sv2[r:r + 1] > s2, float(r + 1), rank2)
        emit(hh, lanes, lam1, rank2, s1, s2, sv1[0:1], sv2[0:1], den)
        cov3 = jnp.sum(chosen, axis=0, keepdims=True)
        k = float(PEER_TOPK)
        tied = jnp.max(jnp.abs(cov1 - k) + jnp.abs(cov2 - k) + jnp.abs(cov3 - k)) > 0.0

        @pl.when(tied)
        def _(hh=hh, lanes=lanes):
            s1 = s_ref[2 * hh, :, lanes]
            s2 = s_ref[2 * hh + 1, :, lanes]
            rank1, sv1 = _top16_ranked(s1)
            rank2, sv2 = _top16_ranked(s2)
            cand0, idx = _pair_cells(sv1, sv2)
            lam, den = _lam_den(_pairs_exact(cand0, idx), cand0)
            lam1 = jnp.zeros_like(rank1)
            for r in range(PEER_TOPK):
                lam1 = jnp.where(rank1 == float(r), lam[r], lam1)
            emit(hh, lanes, lam1, rank2, s1, s2, sv1[0:1], sv2[0:1], den)


def _peer_select(x, g, wqt_bf, sk):
    n, d = x.shape
    n_heads = sk.shape[0] // 2
    nk = sk.shape[1]
    t = min(PEER_TOKENS, n)
    assert n % t == 0 and nk == N_KEYS
    full = lambda shp: pl.BlockSpec(shp, lambda i: (0,) * len(shp))
    sel = pl.BlockSpec((None, n_heads, nk, t), lambda i: (i, 0, 0, 0))
    sel_f32 = jax.ShapeDtypeStruct((n // t, n_heads, nk, t), F32)
    sel_bf16 = jax.ShapeDtypeStruct((n // t, n_heads, nk, t), BF16)
    return pl.pallas_call(
        functools.partial(_peer_select_kernel, n_heads=n_heads),
        grid=(n // t,),
        in_specs=[pl.BlockSpec((t, d), lambda i: (i, 0)), full((1, d)), full(wqt_bf.shape), full(sk.shape)],
        out_specs=[pl.BlockSpec((t, d), lambda i: (i, 0)), sel, sel, sel, sel],
        out_shape=[jax.ShapeDtypeStruct((n, d), BF16), sel_f32, sel_bf16, sel_f32, sel_bf16],
        scratch_shapes=[pltpu.VMEM((2 * n_heads, nk, t), F32)],
        compiler_params=_cparams(("arbitrary",)),
        name="peer_select",
    )(x, g, wqt_bf, sk)


def _gelu_x2(a):
    return a * (1.0 + lax.erf(a * (1.0 / math.sqrt(2.0))))


def _peer_dense_kernel(x_ref, hb_ref, u_ref, vt_ref, lam1_ref, rank2_ref, p1_ref, p2_ref, xo_ref,
                       acc_ref, a0_ref, a1_ref, z0_ref, z1_ref, *, n_heads, n_tiles):
    s = pl.program_id(0)
    n_items = pl.num_programs(0) - 2
    t = hb_ref.shape[0]
    nk = rank2_ref.shape[1]

    @pl.when(s == 0)
    def _():
        acc_ref[...] = jnp.zeros_like(acc_ref)
        a1_ref[...] = jnp.zeros_like(a1_ref)
        z0_ref[...] = jnp.zeros_like(z0_ref)

    n_grp = u_ref.shape[0] // nk
    eb = jnp.clip(s - 1, 0, n_items - 1) % n_tiles
    grp = pl.ds(pl.multiple_of(eb * n_grp, SUBLANES), n_grp)

    te = u_ref.shape[0]
    d = vt_ref.shape[0]
    n_tc = t // LANES

    def act_chunk(a_new, j):
        rows = slice(j * (te // MXU_CHUNKS), (j + 1) * (te // MXU_CHUNKS))
        a_new[rows, :] = lax.dot_general(u_ref[rows, :], hb_ref[...], (((1,), (1,)), ((), ())),
                                         preferred_element_type=F32)

    def val_chunk(z_old, j):
        rows = slice(j * (d // MXU_CHUNKS), (j + 1) * (d // MXU_CHUNKS))
        acc_ref[rows, :] += jnp.dot(vt_ref[rows, :], z_old[...], preferred_element_type=F32)

    def gate_chunk(a_cur, z_new, c):
        ii, tc = divmod(c, n_tc)
        lanes = slice(tc * LANES, (tc + 1) * LANES)
        rows = slice(ii * nk, (ii + 1) * nk)
        gate = jnp.zeros((nk, LANES), BF16)
        for hh in range(n_heads):
            l1 = jnp.broadcast_to(lam1_ref[hh, grp, lanes][ii:ii + 1], (nk, LANES)).astype(BF16)
            pb = jnp.broadcast_to(p1_ref[hh, grp, lanes][ii:ii + 1], (nk, LANES)).astype(BF16)
            gate = gate + jnp.where(l1 > rank2_ref[hh, :, lanes], pb * p2_ref[hh, :, lanes],
                                    jnp.zeros((), BF16))
        z_new[rows, lanes] = gate * _gelu_x2(a_cur[rows, lanes]).astype(BF16)

    def step(a_new, a_cur, z_new, z_old):
        n_gate = n_grp * n_tc
        c = 0
        for k in range(2 * MXU_CHUNKS):
            if k % 2 == 0:
                act_chunk(a_new, k // 2)
            else:
                val_chunk(z_old, k // 2)
            while c < (k + 1) * n_gate // (2 * MXU_CHUNKS):
                gate_chunk(a_cur, z_new, c)
                c += 1

    @pl.when(s % 2 == 0)
    def _():
        step(a0_ref, a1_ref, z1_ref, z0_ref)

    @pl.when(s % 2 == 1)
    def _():
        step(a1_ref, a0_ref, z0_ref, z1_ref)

    @pl.when((s >= 2) & ((s - 2) % n_tiles == n_tiles - 1))
    def _():
        xo_ref[...] = x_ref[...] + acc_ref[...].T
        acc_ref[...] = jnp.zeros_like(acc_ref)


def _peer_dense(x, hb, u_all, vt_all, layer, lam1, rank2, p1, p2):
    n, d = x.shape
    n_exp = u_all.shape[1]
    _, n_heads, nk, t = lam1.shape
    te = DENSE_EXPERTS
    assert n % t == 0 and n_exp % te == 0 and te % (SUBLANES * nk) == 0 and n_exp == nk * nk and t % LANES == 0
    ne = n_exp // te
    assert vt_all.shape[1:] == (ne, d, te)
    n_items = (n // t) * ne
    assert ne >= 2
    item = lambda s, lag: jnp.clip(s - lag, 0, n_items - 1)
    tok = lambda lag: pl.BlockSpec((t, d), lambda s: (item(s, lag) // ne, 0))
    sel = pl.BlockSpec((None, n_heads, nk, t), lambda s: (item(s, 1) // ne, 0, 0, 0))
    return pl.pallas_call(
        functools.partial(_peer_dense_kernel, n_heads=n_heads, n_tiles=ne),
        grid=(n_items + 2,),
        in_specs=[tok(2), tok(0),
                  pl.BlockSpec((None, te, d), lambda s: (layer, item(s, 0) % ne, 0)),
                  pl.BlockSpec((None, None, d, te), lambda s: (layer, item(s, 2) % ne, 0, 0)),
                  sel, sel, sel, sel],
        out_specs=tok(2),
        out_shape=jax.ShapeDtypeStruct((n, d), F32),
        scratch_shapes=[pltpu.VMEM((d, t), F32), pltpu.VMEM((te, t), F32), pltpu.VMEM((te, t), F32),
                        pltpu.VMEM((te, t), BF16), pltpu.VMEM((te, t), BF16)],
        compiler_params=_cparams(("arbitrary",)),
        name="peer_dense",
    )(x, hb, u_all, vt_all, lam1, rank2, p1, p2)


def _peer_tables_kernel(u_ref, v_ref, ub_ref, vt_ref):
    ub_ref[...] = u_ref[...].astype(BF16)
    vt_ref[...] = v_ref[...].T.astype(BF16)


def _peer_tables(peer_u, peer_v):
    n_l, n_exp, d = peer_u.shape
    te = ROW_BLOCK
    per = DENSE_EXPERTS // te
    assert n_exp % DENSE_EXPERTS == 0 and DENSE_EXPERTS % te == 0
    src = pl.BlockSpec((None, te, d), lambda l, j: (l, j, 0))
    return pl.pallas_call(
        _peer_tables_kernel,
        grid=(n_l, n_exp // te),
        in_specs=[src, src],
        out_specs=[src, pl.BlockSpec((None, None, d, te), lambda l, j: (l, j // per, 0, j % per))],
        out_shape=[jax.ShapeDtypeStruct((n_l, n_exp, d), BF16),
                   jax.ShapeDtypeStruct((n_l, n_exp // DENSE_EXPERTS, d, DENSE_EXPERTS), BF16)],
        compiler_params=_cparams(("arbitrary", "arbitrary")),
        name="peer_tables",
    )(peer_u, peer_v)


def _peer(x, g, wqt_bf, sk, u_all, vt_all, layer):
    n = x.shape[0]
    pad = (-n) % LANES
    xp = jnp.pad(x, ((0, pad), (0, 0))) if pad else x
    hb, lam1, rank2, p1, p2 = _peer_select(xp, g, wqt_bf, sk)
    out = _peer_dense(xp, hb, u_all, vt_all, layer, lam1, rank2, p1, p2)
    return out[:n] if pad else out


def _final_norm_kernel(x_ref, g_ref, o_ref):
    o_ref[...] = _rms(x_ref[...], g_ref[...])


def _final_norm(x, g):
    n, d = x.shape
    tm = min(ROW_BLOCK, n)
    assert n % tm == 0
    return pl.pallas_call(
        _final_norm_kernel,
        grid=(n // tm,),
        in_specs=[pl.BlockSpec((tm, d), lambda i: (i, 0)), pl.BlockSpec((1, d), lambda i: (0, 0))],
        out_specs=pl.BlockSpec((tm, d), lambda i: (i, 0)),
        out_shape=jax.ShapeDtypeStruct((n, d), F32),
        compiler_params=_cparams(("arbitrary",)),
        name="final_norm",
    )(x, g)


def kernel(x_prompt, x_sample, state_conv, state_win_k, state_win_v, state_pool, norm_mix, w_in_ab, conv_w,
           w_out_ab, pool_w, pool_b, pool_scale, norm_ffn, peer_wq, peer_subkeys, peer_u, peer_v, norm_final):
    b, s, d = x_prompt.shape
    bs, ts, _ = x_sample.shape
    assert ts == 1
    depth = norm_mix.shape[0]
    d_conv = conv_w.shape[2]
    n_heads_b, hd = state_win_k.shape[3], state_win_k.shape[4]
    d_att = n_heads_b * hd
    wb_s = state_win_k.shape[2]
    wb_p = min(max(w for w, _ in DILATED_PAIRS), s)
    pool_state = state_pool.shape[2]
    q_scale = float(hd) ** -0.5
    n_peer_heads = peer_subkeys.shape[1]

    xp = x_prompt.reshape(b * s, d)
    xs = x_sample.reshape(bs, d)
    u_all, vt_all = _peer_tables(peer_u, peer_v)
    conv_p, conv_s, wk_p, wk_s, wv_p, wv_s, pool_p, pool_s = [], [], [], [], [], [], [], []
    for l in range(depth):
        g_mix = norm_mix[l][None]
        if l % 2 == 0:
            e = l // 2
            w_in = w_in_ab[e].astype(BF16)
            wa = w_out_ab[e, :d_conv].astype(BF16)
            wbm = w_out_ab[e, d_conv:].astype(BF16)
            ya, k, v, ul, qkv = _even_in_prompt(xp.reshape(b, s, d), g_mix, w_in, conv_w[e], d_conv, d_att, q_scale)
            branches = [_attn_branch(*qkv[dl], w, dl, n_heads_b) for w, dl in DILATED_PAIRS]
            o_list = [br[0].reshape(b * s // dl, dl * d_att) for br, (_, dl) in zip(branches, DILATED_PAIRS)]
            l_list = [br[1].reshape(b * s // dl, dl * d_att) for br, (_, dl) in zip(branches, DILATED_PAIRS)]
            xp = _out_proj(xp, ya.reshape(b * s, d_conv), o_list + l_list, wa, wbm)
            conv_p.append(ul[:, SUBLANES - 2:])
            wk_p.append(k[:, s - wb_p:].reshape(b, wb_p, n_heads_b, hd))
            wv_p.append(v[:, s - wb_p:].reshape(b, wb_p, n_heads_b, hd))
            cst = state_conv[e]
            ya_s, q_s, k_s, v_s, u_s = _even_in_sample(xs, g_mix, w_in, conv_w[e], cst[:, 1], cst[:, 0],
                                                       d_conv, d_att, q_scale)
            yb_s = _attn_sample(q_s, k_s, v_s, state_win_k, state_win_v, e)
            xs = _out_proj(xs, ya_s, [yb_s], wa, wbm)
            conv_s.append(jnp.stack([cst[:, 1], u_s], axis=1))
            wk_s.append(k_s.reshape(bs, 1, n_heads_b, hd))
            wv_s.append(v_s.reshape(bs, 1, n_heads_b, hd))
        else:
            o = l // 2
            pw = pool_w[o].astype(BF16)
            xp3, hl = _pool_prompt(xp.reshape(b, s, d), g_mix, pw, pool_b[o], pool_scale[o][None])
            xp = xp3.reshape(b * s, d)
            pool_p.append(hl[:, hl.shape[1] - pool_state:])
            st = state_pool[o]
            xs, h_s = _pool_sample(xs, jnp.swapaxes(st, 0, 1), g_mix, pw, pool_b[o], pool_scale[o][None])
            pool_s.append(jnp.concatenate([st[:, 1:], h_s[:, None]], axis=1))
        g_ffn = norm_ffn[l][None]
        wqt = peer_wq[l].T.astype(BF16)
        sk = peer_subkeys[l].reshape(2 * n_peer_heads, N_KEYS, -1)
        sk_hi = sk.astype(BF16)
        sk_lo = (sk - sk_hi.astype(F32)).astype(BF16)
        sk = jnp.concatenate([sk_hi, sk_lo, sk_hi], axis=-1)
        xp = _peer(xp, g_ffn, wqt, sk, u_all, vt_all, l)
        xs = _peer(xs, g_ffn, wqt, sk, u_all, vt_all, l)
    gf = norm_final[None]
    y_prompt = _final_norm(xp, gf).reshape(b, s, d)
    y_sample = _final_norm(xs, gf).reshape(bs, ts, d)
    last = (0, 0, wb_s - 1, 0, 0)
    shifted_k = lax.pad(state_win_k[:, :, 1:], jnp.zeros((), state_win_k.dtype),
                        [(0, 0, 0), (0, 0, 0), (0, 1, 0), (0, 0, 0), (0, 0, 0)])
    win_k_s = lax.dynamic_update_slice(shifted_k, jnp.stack(wk_s), last)
    win_v_s = lax.dynamic_update_slice(jnp.roll(state_win_v, -1, axis=2), jnp.stack(wv_s), last)
    return (y_prompt, y_sample, jnp.stack(conv_p), jnp.stack(conv_s), jnp.stack(wk_p), win_k_s,
            jnp.stack(wv_p), win_v_s, jnp.stack(pool_p), jnp.stack(pool_s))
```

```python
import functools
import math

import jax
import jax.numpy as jnp
from jax import lax
from jax.experimental import pallas as pl
from jax.experimental.pallas import tpu as pltpu

F32 = jnp.float32
BF16 = jnp.bfloat16

NORM_EPS = 1e-6
NEG_INF = -1e30
DILATED_PAIRS = ((128, 1), (512, 4), (2048, 16))
POOL_WINDOWS = (2, 4, 8, 16)
PEER_TOPK = 16
N_KEYS = 128

LANES = 128
SUBLANES = 8
VMEM_LIMIT = 56 * 1024 * 1024

ATT_BLOCK = 128
ROW_BLOCK = 512
PEER_TOKENS = 256
MXU_CHUNKS = 4
DENSE_KEY_GROUP = 16
DENSE_EXPERTS = DENSE_KEY_GROUP * N_KEYS


def _cparams(sem, flags=None):
    return pltpu.CompilerParams(dimension_semantics=sem, vmem_limit_bytes=VMEM_LIMIT, flags=flags)


def _rms(x, g):
    r = lax.rsqrt(jnp.mean(x * x, axis=-1, keepdims=True) + NORM_EPS)
    return (x * r) * g


def _bdot(a, b):
    return jnp.dot(a.astype(BF16), b.astype(BF16), preferred_element_type=F32)


def _bdot_nt(a, b):
    return lax.dot_general(a.astype(BF16), b.astype(BF16), (((1,), (1,)), ((), ())),
                           preferred_element_type=F32)


def _even_in_body(x, g, w, cw, u1_fn, d_conv, d_att, q_scale):
    h = _rms(x, g)
    p = _bdot(h, w)
    gate_b = p[:, 0:d_conv]
    gate_c = p[:, d_conv:2 * d_conv]
    xv = p[:, 2 * d_conv:3 * d_conv]
    o = 3 * d_conv
    q = p[:, o:o + d_att] * q_scale
    k = p[:, o + d_att:o + 2 * d_att]
    v = p[:, o + 2 * d_att:o + 3 * d_att]
    u = gate_c * xv
    u1, u2 = u1_fn(u)
    y = cw[0:1] * u2 + cw[1:2] * u1 + cw[2:3] * u
    return gate_b * y, q, k, v, u


def _even_in_seq_kernel(x_ref, g_ref, w_ref, cw_ref, ya_ref, q_ref, k_ref, v_ref, ul_ref, *rest,
                        d_conv, d_att, q_scale):
    dil_refs, carry_ref, stage_ref = rest[:-2], rest[-2], rest[-1]
    s = pl.program_id(1)

    @pl.when(s == 0)
    def _():
        carry_ref[...] = jnp.zeros_like(carry_ref)

    prev = carry_ref[...]
    tm = x_ref.shape[0]

    def shifted(u):
        rows = lax.broadcasted_iota(jnp.int32, u.shape, 0)
        u1 = jnp.where(rows == 0, prev[7:8], pltpu.roll(u, 1, 0))
        u2 = pltpu.roll(u, 2, 0)
        u2 = jnp.where(rows == 0, prev[6:7], jnp.where(rows == 1, prev[7:8], u2))
        return u1, u2

    ya, q, k, v, u = _even_in_body(x_ref[...], g_ref[...], w_ref[...], cw_ref[...], shifted,
                                   d_conv, d_att, q_scale)
    ya_ref[...] = ya
    q_ref[...] = q
    k_ref[...] = k
    v_ref[...] = v
    last = u[tm - SUBLANES:tm]
    carry_ref[...] = last
    ul_ref[...] = last
    n_cb = d_att // LANES
    for j, val in enumerate((q, k, v)):
        for c in range(n_cb):
            stage_ref[j, c] = val[:, c * LANES:(c + 1) * LANES]
    for i, ref in enumerate(dil_refs):
        rows = ref.shape[0]
        dil = tm // rows
        for r in range(dil):
            for c in range(n_cb):
                col = r * d_att + c * LANES
                ref[:, col:col + LANES] = stage_ref[i % 3, c, pl.ds(r, rows, stride=dil), :]


def _even_in_rows_kernel(x_ref, g_ref, w_ref, cw_ref, u1_ref, u2_ref, ya_ref, q_ref, k_ref, v_ref, u_ref,
                         *, d_conv, d_att, q_scale):
    ya, q, k, v, u = _even_in_body(x_ref[...], g_ref[...], w_ref[...], cw_ref[...],
                                   lambda _: (u1_ref[...], u2_ref[...]), d_conv, d_att, q_scale)
    ya_ref[...] = ya
    q_ref[...] = q
    k_ref[...] = k
    v_ref[...] = v
    u_ref[...] = u


def _even_in_prompt(x, g, w_bf, cw, d_conv, d_att, q_scale):
    b, s, d = x.shape
    tm = min(ROW_BLOCK, s)
    assert s % tm == 0 and tm % SUBLANES == 0
    ncol = w_bf.shape[1]
    row = lambda c: pl.BlockSpec((None, tm, c), lambda i, j: (i, j, 0))
    full = lambda shp: pl.BlockSpec(shp, lambda i, j: (0,) * len(shp))
    dils = [dl for _, dl in DILATED_PAIRS if dl > 1]
    assert all(tm % (dl * SUBLANES) == 0 for dl in dils)
    dil_specs = [pl.BlockSpec((None, tm // dl, dl * d_att), lambda i, j: (i, j, 0)) for dl in dils for _ in range(3)]
    dil_shapes = [jax.ShapeDtypeStruct((b, s // dl, dl * d_att), F32) for dl in dils for _ in range(3)]
    outs = pl.pallas_call(
        functools.partial(_even_in_seq_kernel, d_conv=d_conv, d_att=d_att, q_scale=q_scale),
        grid=(b, s // tm),
        in_specs=[row(d), full((1, d)), full((d, ncol)), full((cw.shape[0], d_conv))],
        out_specs=[row(d_conv), row(d_att), row(d_att), row(d_att),
                   pl.BlockSpec((None, SUBLANES, d_conv), lambda i, j: (i, 0, 0))] + dil_specs,
        out_shape=[jax.ShapeDtypeStruct((b, s, d_conv), F32)] + [jax.ShapeDtypeStruct((b, s, d_att), F32)] * 3
                  + [jax.ShapeDtypeStruct((b, SUBLANES, d_conv), F32)] + dil_shapes,
        scratch_shapes=[pltpu.VMEM((SUBLANES, d_conv), F32), pltpu.VMEM((3, d_att // LANES, tm, LANES), F32)],
        compiler_params=_cparams(("arbitrary", "arbitrary")),
        name="even_in_prompt",
    )(x, g, w_bf, cw)
    ya, q, k, v, ul = outs[:5]
    qkv = {1: (q, k, v)}
    for i, dl in enumerate(dils):
        qkv[dl] = tuple(outs[5 + 3 * i:8 + 3 * i])
    return ya, k, v, ul, qkv


def _even_in_sample(x, g, w_bf, cw, u1, u2, d_conv, d_att, q_scale):
    n, d = x.shape
    ncol = w_bf.shape[1]
    full = lambda shp: pl.BlockSpec(shp, lambda i: (0,) * len(shp))
    return pl.pallas_call(
        functools.partial(_even_in_rows_kernel, d_conv=d_conv, d_att=d_att, q_scale=q_scale),
        grid=(1,),
        in_specs=[full((n, d)), full((1, d)), full((d, ncol)), full((cw.shape[0], d_conv)),
                  full((n, d_conv)), full((n, d_conv))],
        out_specs=[full((n, d_conv)), full((n, d_att)), full((n, d_att)), full((n, d_att)), full((n, d_conv))],
        out_shape=[jax.ShapeDtypeStruct((n, d_conv), F32)] + [jax.ShapeDtypeStruct((n, d_att), F32)] * 3
                  + [jax.ShapeDtypeStruct((n, d_conv), F32)],
        compiler_params=_cparams(("arbitrary",)),
        name="even_in_sample",
    )(x, g, w_bf, cw, u1, u2)


def _attn_branch_kernel(q_ref, kp_ref, kc_ref, vp_ref, vc_ref, o_ref, l_ref, *, n_heads, hd, n_back):
    n = pl.program_id(2)
    blk = q_ref.shape[0]
    q = q_ref[...]
    k = jnp.concatenate([kp_ref[...], kc_ref[...]], axis=0)
    v = jnp.concatenate([vp_ref[...], vc_ref[...]], axis=0)
    qi = lax.broadcasted_iota(jnp.int32, (blk, 2 * blk), 0)
    ki = lax.broadcasted_iota(jnp.int32, (blk, 2 * blk), 1)
    dist = qi + blk - ki
    has_prev = jnp.where(n > 0, 0, blk)
    mask = (dist >= 0) & (dist <= n_back) & (ki >= has_prev)
    o_parts, l_parts = [], []
    for h in range(n_heads):
        sl = slice(h * hd, (h + 1) * hd)
        s = _bdot_nt(q[:, sl], k[:, sl])
        s = jnp.where(mask, s, NEG_INF)
        m = jnp.max(s, axis=-1, keepdims=True)
        p = jnp.exp(s - m)
        den = jnp.sum(p, axis=-1, keepdims=True)
        o_parts.append(_bdot(p, v[:, sl]) / den)
        l_parts.append(jnp.broadcast_to(m + jnp.log(den), (blk, hd)))
    o_ref[...] = jnp.concatenate(o_parts, axis=1)
    l_ref[...] = jnp.concatenate(l_parts, axis=1)


def _attn_branch(q, k, v, window, dil, n_heads):
    b, L, dda = q.shape
    da = dda // dil
    hd = da // n_heads
    n_back = window // dil
    assert L % ATT_BLOCK == 0 and n_back <= ATT_BLOCK
    nb = L // ATT_BLOCK
    cur = pl.BlockSpec((None, ATT_BLOCK, da), lambda i, r, n: (i, n, r))
    prev = pl.BlockSpec((None, ATT_BLOCK, da), lambda i, r, n: (i, jnp.maximum(n - 1, 0), r))
    o, l = pl.pallas_call(
        functools.partial(_attn_branch_kernel, n_heads=n_heads, hd=hd, n_back=n_back),
        grid=(b, dil, nb),
        in_specs=[cur, prev, cur, prev, cur],
        out_specs=[cur, cur],
        out_shape=[jax.ShapeDtypeStruct((b, L, dil * da), F32)] * 2,
        compiler_params=_cparams(("arbitrary", "arbitrary", "arbitrary")),
        name=f"attn_branch_d{dil}",
    )(q, k, k, v, v)
    return o, l


def _attn_sample_kernel(q_ref, kn_ref, vn_ref, *refs):
    n_br = (len(refs) - 1) // 2
    k_refs, v_refs, y_ref = refs[:n_br], refs[n_br:2 * n_br], refs[-1]
    q = q_ref[...][None]
    kn = kn_ref[...][None]
    vn = vn_ref[...][None]
    s0 = jnp.sum(kn * q, axis=-1, keepdims=True)
    o_list, l_list = [], []
    for g in range(n_br):
        s = jnp.sum(k_refs[g][...] * q, axis=-1, keepdims=True)
        m = jnp.maximum(jnp.max(s, axis=0, keepdims=True), s0)
        p = jnp.exp(s - m)
        p0 = jnp.exp(s0 - m)
        den = jnp.sum(p, axis=0, keepdims=True) + p0
        o_list.append((jnp.sum(p * v_refs[g][...], axis=0, keepdims=True) + p0 * vn) / den)
        l_list.append(m + jnp.log(den))
    y_ref[...] = _merge_branches(o_list, l_list)[0]


def _merge_branches(o_list, l_list):
    m = functools.reduce(jnp.maximum, l_list)
    e = [jnp.exp(l - m) for l in l_list]
    num = functools.reduce(lambda a, b: a + b, [ei * oi for ei, oi in zip(e, o_list)])
    return num / functools.reduce(lambda a, b: a + b, e)


def _attn_sample(q, k_new, v_new, k_state_all, v_state_all, layer):
    b, da = q.shape
    n_l, _, wb, n_heads, hd = k_state_all.shape
    row = pl.BlockSpec((None, n_heads, hd), lambda i: (i, 0, 0))
    ins, specs = [], []
    for st in (k_state_all, v_state_all):
        for window, dil in DILATED_PAIRS:
            n_back = window // dil
            assert n_back * dil <= wb and wb % dil == 0 and (wb // dil) % n_back == 0
            L = wb // dil
            ins.append(st.reshape(n_l, b, L, dil, n_heads, hd))
            specs.append(pl.BlockSpec((None, None, n_back, None, n_heads, hd),
                                      lambda i, L=L, nbk=n_back: (layer, i, L // nbk - 1, 0, 0, 0)))
    as_heads = lambda t: t.reshape(b, n_heads, hd)
    y = pl.pallas_call(
        _attn_sample_kernel,
        grid=(b,),
        in_specs=[row, row, row] + specs,
        out_specs=row,
        out_shape=jax.ShapeDtypeStruct((b, n_heads, hd), F32),
        compiler_params=_cparams(("arbitrary",)),
        name="attn_sample",
    )(as_heads(q), as_heads(k_new), as_heads(v_new), *ins)
    return y.reshape(b, da)


def _out_proj_kernel(x_ref, ya_ref, *refs, n_in):
    br = refs[:n_in]
    wa_ref, wb_ref, xo_ref = refs[n_in:n_in + 3]
    scratch = list(refs[n_in + 3:])
    tm, da = ya_ref.shape[0], wb_ref.shape[0]

    def token_order(ref):
        rows = ref.shape[0]
        if rows == tm:
            return ref[...]
        dil = tm // rows
        sc = scratch.pop(0)
        n_cb = da // LANES
        for r in range(dil):
            for c in range(n_cb):
                col = r * da + c * LANES
                sc[c, pl.ds(r, rows, stride=dil), :] = ref[:, col:col + LANES]
        return jnp.concatenate([sc[c] for c in range(n_cb)], axis=1)

    vals = [token_order(r) for r in br]
    yb = vals[0] if n_in == 1 else _merge_branches(vals[:n_in // 2], vals[n_in // 2:])
    xo_ref[...] = x_ref[...] + _bdot(ya_ref[...], wa_ref[...]) + _bdot(yb, wb_ref[...])


def _out_proj(x, ya, branch_arrays, wa_bf, wb_bf):
    n, d = x.shape
    da = wb_bf.shape[0]
    tm = min(ROW_BLOCK, n)
    assert n % tm == 0
    row = lambda c: pl.BlockSpec((tm, c), lambda i: (i, 0))
    full = lambda shp: pl.BlockSpec(shp, lambda i: (0,) * len(shp))
    dils = [a.shape[1] // da for a in branch_arrays]
    assert all(a.shape == (n // dl, dl * da) and (dl == 1 or tm % (dl * SUBLANES) == 0)
               for a, dl in zip(branch_arrays, dils))
    return pl.pallas_call(
        functools.partial(_out_proj_kernel, n_in=len(branch_arrays)),
        grid=(n // tm,),
        in_specs=[row(d), row(ya.shape[1])]
                 + [pl.BlockSpec((tm // dl, dl * da), lambda i: (i, 0)) for dl in dils]
                 + [full(wa_bf.shape), full(wb_bf.shape)],
        out_specs=row(d),
        out_shape=jax.ShapeDtypeStruct((n, d), F32),
        scratch_shapes=[pltpu.VMEM((da // LANES, tm, LANES), F32) for dl in dils if dl > 1],
        compiler_params=_cparams(("arbitrary",)),
        name="out_proj",
    )(x, ya, *branch_arrays, wa_bf, wb_bf)


def _pool_groups(dmat, w_ref, b_ref, sc):
    n_g = w_ref.shape[0]
    ys = [_bdot(dmat[g], w_ref[g]) + b_ref[g:g + 1] for g in range(n_g)]
    return jnp.concatenate(ys, axis=1) * sc


def _pool_prompt_kernel(x_ref, g_ref, w_ref, b_ref, sc_ref, xo_ref, hl_ref, carry_ref):
    s = pl.program_id(1)
    hist = carry_ref.shape[0]

    @pl.when(s == 0)
    def _():
        carry_ref[...] = jnp.zeros_like(carry_ref)

    x = x_ref[...]
    tm = x.shape[0]
    h = _rms(x, g_ref[...])
    ext = jnp.concatenate([carry_ref[...], h], axis=0)
    pos = s * tm + lax.broadcasted_iota(jnp.int32, (tm, 1), 0) + 1
    gw = w_ref.shape[1]
    acc = ext
    width = 1
    diffs = []
    for g, w in enumerate(POOL_WINDOWS):
        while width < w:
            acc = acc + pltpu.roll(acc, width, 0)
            width *= 2
        cols = slice(g * gw, (g + 1) * gw)
        win = acc[hist:, cols]
        div = jnp.minimum(pos, w).astype(F32)
        diffs.append(win / div - h[:, cols])
    xo_ref[...] = x + _pool_groups(diffs, w_ref, b_ref, sc_ref[...])
    last = ext[tm:tm + hist]
    carry_ref[...] = last
    hl_ref[...] = last


def _pool_prompt(x, g, w_bf, bias, scale):
    b, s, d = x.shape
    tm = min(ROW_BLOCK, s)
    hist = 16
    assert s % tm == 0 and tm >= hist and max(POOL_WINDOWS) <= hist
    assert all(w == 2 ** (i + 1) for i, w in enumerate(POOL_WINDOWS))
    row = pl.BlockSpec((None, tm, d), lambda i, j: (i, j, 0))
    full = lambda shp: pl.BlockSpec(shp, lambda i, j: (0,) * len(shp))
    return pl.pallas_call(
        _pool_prompt_kernel,
        grid=(b, s // tm),
        in_specs=[row, full((1, d)), full(w_bf.shape), full(bias.shape), full((1, d))],
        out_specs=[row, pl.BlockSpec((None, hist, d), lambda i, j: (i, 0, 0))],
        out_shape=[jax.ShapeDtypeStruct((b, s, d), F32), jax.ShapeDtypeStruct((b, hist, d), F32)],
        scratch_shapes=[pltpu.VMEM((hist, d), F32)],
        compiler_params=_cparams(("arbitrary", "arbitrary")),
        name="pool_prompt",
    )(x, g, w_bf, bias, scale)


def _pool_sample_kernel(x_ref, st_ref, g_ref, w_ref, b_ref, sc_ref, xo_ref, h_ref):
    x = x_ref[...]
    h = _rms(x, g_ref[...])
    n_st = st_ref.shape[0]
    gw = w_ref.shape[1]
    diffs = []
    for g, w in enumerate(POOL_WINDOWS):
        cols = slice(g * gw, (g + 1) * gw)
        tot = h[:, cols]
        for j in range(1, w):
            tot = tot + st_ref[n_st - j][:, cols]
        diffs.append(tot / float(w) - h[:, cols])
    xo_ref[...] = x + _pool_groups(diffs, w_ref, b_ref, sc_ref[...])
    h_ref[...] = h


def _pool_sample(x, state_t, g, w_bf, bias, scale):
    n, d = x.shape
    assert state_t.shape[0] + 1 >= max(POOL_WINDOWS)
    full = lambda shp: pl.BlockSpec(shp, lambda i: (0,) * len(shp))
    return pl.pallas_call(
        _pool_sample_kernel,
        grid=(1,),
        in_specs=[full(x.shape), full(state_t.shape), full((1, d)), full(w_bf.shape), full(bias.shape),
                  full((1, d))],
        out_specs=[full(x.shape), full(x.shape)],
        out_shape=[jax.ShapeDtypeStruct((n, d), F32)] * 2,
        compiler_params=_cparams(("arbitrary",)),
        name="pool_sample",
    )(x, state_t, g, w_bf, bias, scale)


def _top16_ranked(s):
    n, t = s.shape
    rows = lax.broadcasted_iota(jnp.int32, (n, t), 0).astype(F32)
    rows16 = lax.broadcasted_iota(jnp.int32, (PEER_TOPK, t), 0)

    def step(it, carry):
        s, rank, sv = carry
        m = jnp.max(s, axis=0, keepdims=True)
        first = jnp.min(jnp.where(s == m, rows, float(n)), axis=0, keepdims=True)
        sel = rows == first
        itf = jnp.asarray(it, jnp.int32).astype(F32)
        return (jnp.where(sel, -jnp.inf, s), jnp.where(sel, itf, rank), jnp.where(rows16 == it, m, sv))

    init = (s, jnp.full((n, t), float(PEER_TOPK), F32), jnp.zeros((PEER_TOPK, t), F32))
    _, rank, sv = lax.fori_loop(0, PEER_TOPK, step, init)
    return rank, sv


def _sort_network(n_in):
    n, pairs, p = PEER_TOPK, [], 1
    assert n_in <= n
    while p < n:
        k = p
        while k >= 1:
            for j in range(k % p, n - k, 2 * k):
                for i in range(min(k, n - j - k)):
                    if (i + j) // (2 * p) == (i + j + k) // (2 * p):
                        pairs.append((i + j, i + j + k))
            k //= 2
        p *= 2
    return [(i, j) for i, j in pairs if j < n_in]


def _largest_distinct(groups, k):
    v = list(groups)
    n_g = len(v)
    for i, j in _sort_network(n_g):
        v[i], v[j] = jnp.maximum(v[i], v[j]), jnp.minimum(v[i], v[j])
    out = []
    for it in range(k):
        m = jnp.max(v[0], axis=0, keepdims=True)
        out.append(m)
        keep = min(n_g, k - it)
        pop = v[0] == m
        for j in range(keep - 1):
            v[j] = jnp.where(pop, v[j + 1], v[j])
        if keep == n_g and keep > 1:
            v[n_g - 1] = jnp.where(pop, -jnp.inf, v[n_g - 1])
    return out


def _top16_values(s):
    n, t = s.shape
    tops = _largest_distinct([s[SUBLANES * j:SUBLANES * (j + 1)] for j in range(n // SUBLANES)], PEER_TOPK)
    rows16 = lax.broadcasted_iota(jnp.int32, (PEER_TOPK, t), 0)
    sv = jnp.zeros((PEER_TOPK, t), F32)
    for it, m in enumerate(tops):
        sv = jnp.where(rows16 == it, m, sv)
    cover = jnp.sum(jnp.where(s >= tops[-1], 1.0, 0.0), axis=0, keepdims=True)
    return sv, cover


def _pair_cells(sv1, sv2):
    t = sv1.shape[1]
    half = SUBLANES
    tiles = [sv1[0:1] + sv2[0:half], sv1[0:1] + sv2[half:2 * half]]
    tiles += [sv1[r:r + 1] + sv2[0:half] for r in range(1, half)]
    tiles += [sv1[half:2 * half] + sv2[0:1]]
    cand0 = jnp.concatenate(tiles, axis=0)
    nrow = cand0.shape[0]
    i = lax.broadcasted_iota(jnp.int32, (nrow, t), 0)
    j = i - 2 * half
    mid = (lax.shift_right_arithmetic(j, 3) + 1) * PEER_TOPK + lax.bitwise_and(j, half - 1)
    idx = jnp.where(i < 2 * half, i,
                    jnp.where(i < nrow - half, mid, (i - (nrow - half) + half) * PEER_TOPK)).astype(F32)
    return cand0, idx


def _pairs_exact(cand0, idx):
    def step(_, carry):
        cand, chosen = carry
        m = jnp.max(cand, axis=0, keepdims=True)
        first = jnp.min(jnp.where(cand == m, idx, 1e9), axis=0, keepdims=True)
        sel = idx == first
        return jnp.where(sel, -jnp.inf, cand), jnp.where(sel, 1.0, chosen)

    _, chosen = lax.fori_loop(0, PEER_TOPK, step, (cand0, jnp.zeros_like(cand0)))
    return chosen


def _pairs_fast(cand0):
    tiles = [cand0[SUBLANES * j:SUBLANES * (j + 1)] for j in range(cand0.shape[0] // SUBLANES)]
    return jnp.where(cand0 >= _largest_distinct(tiles, PEER_TOPK)[-1], 1.0, 0.0)


def _lam_den(chosen, cand0):
    half = SUBLANES
    nrow = cand0.shape[0]
    den = jnp.sum(jnp.where(chosen > 0, jnp.exp(cand0 - cand0[0:1]), 0.0), axis=0, keepdims=True)
    lam = [jnp.sum(chosen[0:2 * half], axis=0, keepdims=True)]
    lam += [jnp.sum(chosen[2 * half + half * (r - 1):2 * half + half * r], axis=0, keepdims=True)
            for r in range(1, half)]
    lam += [chosen[nrow - half + r:nrow - half + r + 1] for r in range(half)]
    return lam, den


def _peer_select_kernel(x_ref, g_ref, wqt_ref, sk_ref, hb_ref, lam1_ref, rank2_ref, p1_ref, p2_ref, s_ref,
                        *, n_heads):
    h = _rms(x_ref[...], g_ref[...])
    hb = h.astype(BF16)
    hb_ref[...] = hb
    nk = sk_ref.shape[1]
    qt = lax.dot_general(wqt_ref[...], hb, (((1,), (1,)), ((), ())), preferred_element_type=F32)
    q_hi = qt.astype(BF16)
    q_lo = (qt - q_hi.astype(F32)).astype(BF16)
    for c in range(2 * n_heads):
        rows = slice(c * nk, (c + 1) * nk)
        q3 = jnp.concatenate([q_hi[rows], q_hi[rows], q_lo[rows]], axis=0)
        s_ref[c] = jnp.dot(sk_ref[c], q3, preferred_element_type=F32)

    def emit(hh, lanes, lam1, rank2, s1, s2, top1, top2, den):
        lam1_ref[hh, :, lanes] = lam1
        rank2_ref[hh, :, lanes] = rank2.astype(rank2_ref.dtype)
        p1_ref[hh, :, lanes] = jnp.exp(s1 - top1) / (2.0 * den)
        p2_ref[hh, :, lanes] = jnp.exp(s2 - top2).astype(p2_ref.dtype)

    for hh, lc in ((hh, lc) for hh in range(n_heads) for lc in range(x_ref.shape[0] // LANES)):
        lanes = slice(lc * LANES, (lc + 1) * LANES)
        s1 = s_ref[2 * hh, :, lanes]
        s2 = s_ref[2 * hh + 1, :, lanes]
        sv1, cov1 = _top16_values(s1)
        sv2, cov2 = _top16_values(s2)
        cand0, _ = _pair_cells(sv1, sv2)
        chosen = _pairs_fast(cand0)
        lam, den = _lam_den(chosen, cand0)
        lam1 = jnp.broadcast_to(lam[0], s1.shape)
        rank2 = jnp.zeros_like(s2)
        for r in range(PEER_TOPK):
            lam1 = jnp.where(sv1[r:r + 1] > s1, lam[r + 1] if r + 1 < PEER_TOPK else 0.0, lam1)
            rank2 = jnp.where(sv2[r:r + 1] > s2, float(r + 1), rank2)
        emit(hh, lanes, lam1, rank2, s1, s2, sv1[0:1], sv2[0:1], den)
        cov3 = jnp.sum(chosen, axis=0, keepdims=True)
        k = float(PEER_TOPK)
        tied = jnp.max(jnp.abs(cov1 - k) + jnp.abs(cov2 - k) + jnp.abs(cov3 - k)) > 0.0

        @pl.when(tied)
        def _(hh=hh, lanes=lanes):
            s1 = s_ref[2 * hh, :, lanes]
            s2 = s_ref[2 * hh + 1, :, lanes]
            rank1, sv1 = _top16_ranked(s1)
            rank2, sv2 = _top16_ranked(s2)
            cand0, idx = _pair_cells(sv1, sv2)
            lam, den = _lam_den(_pairs_exact(cand0, idx), cand0)
            lam1 = jnp.zeros_like(rank1)
            for r in range(PEER_TOPK):
                lam1 = jnp.where(rank1 == float(r), lam[r], lam1)
            emit(hh, lanes, lam1, rank2, s1, s2, sv1[0:1], sv2[0:1], den)


def _peer_select(x, g, wqt_bf, sk):
    n, d = x.shape
    n_heads = sk.shape[0] // 2
    nk = sk.shape[1]
    t = min(PEER_TOKENS, n)
    assert n % t == 0 and nk == N_KEYS
    full = lambda shp: pl.BlockSpec(shp, lambda i: (0,) * len(shp))
    sel = pl.BlockSpec((None, n_heads, nk, t), lambda i: (i, 0, 0, 0))
    sel_f32 = jax.ShapeDtypeStruct((n // t, n_heads, nk, t), F32)
    sel_bf16 = jax.ShapeDtypeStruct((n // t, n_heads, nk, t), BF16)
    return pl.pallas_call(
        functools.partial(_peer_select_kernel, n_heads=n_heads),
        grid=(n // t,),
        in_specs=[pl.BlockSpec((t, d), lambda i: (i, 0)), full((1, d)), full(wqt_bf.shape), full(sk.shape)],
        out_specs=[pl.BlockSpec((t, d), lambda i: (i, 0)), sel, sel, sel, sel],
        out_shape=[jax.ShapeDtypeStruct((n, d), BF16), sel_f32, sel_bf16, sel_f32, sel_bf16],
        scratch_shapes=[pltpu.VMEM((2 * n_heads, nk, t), F32)],
        compiler_params=_cparams(("arbitrary",)),
        name="peer_select",
    )(x, g, wqt_bf, sk)


def _gelu_x2(a):
    return a * (1.0 + lax.erf(a * (1.0 / math.sqrt(2.0))))


def _peer_dense_kernel(x_ref, hb_ref, u_ref, vt_ref, lam1_ref, rank2_ref, p1_ref, p2_ref, xo_ref,
                       acc_ref, a0_ref, a1_ref, z0_ref, z1_ref, *, n_heads, n_tiles):
    s = pl.program_id(0)
    n_items = pl.num_programs(0) - 2
    t = hb_ref.shape[0]
    nk = rank2_ref.shape[1]

    @pl.when(s == 0)
    def _():
        acc_ref[...] = jnp.zeros_like(acc_ref)
        a1_ref[...] = jnp.zeros_like(a1_ref)
        z0_ref[...] = jnp.zeros_like(z0_ref)

    n_grp = u_ref.shape[0] // nk
    eb = jnp.clip(s - 1, 0, n_items - 1) % n_tiles
    grp = pl.ds(pl.multiple_of(eb * n_grp, SUBLANES), n_grp)

    te = u_ref.shape[0]
    d = vt_ref.shape[0]
    n_tc = t // LANES

    def act_chunk(a_new, j):
        rows = slice(j * (te // MXU_CHUNKS), (j + 1) * (te // MXU_CHUNKS))
        a_new[rows, :] = lax.dot_general(u_ref[rows, :], hb_ref[...], (((1,), (1,)), ((), ())),
                                         preferred_element_type=F32)

    def val_chunk(z_old, j):
        rows = slice(j * (d // MXU_CHUNKS), (j + 1) * (d // MXU_CHUNKS))
        acc_ref[rows, :] += jnp.dot(vt_ref[rows, :], z_old[...], preferred_element_type=F32)

    def gate_chunk(a_cur, z_new, c):
        ii, tc = divmod(c, n_tc)
        lanes = slice(tc * LANES, (tc + 1) * LANES)
        rows = slice(ii * nk, (ii + 1) * nk)
        gate = jnp.zeros((nk, LANES), BF16)
        for hh in range(n_heads):
            l1 = jnp.broadcast_to(lam1_ref[hh, grp, lanes][ii:ii + 1], (nk, LANES)).astype(BF16)
            pb = jnp.broadcast_to(p1_ref[hh, grp, lanes][ii:ii + 1], (nk, LANES)).astype(BF16)
            gate = gate + jnp.where(l1 > rank2_ref[hh, :, lanes], pb * p2_ref[hh, :, lanes],
                                    jnp.zeros((), BF16))
        z_new[rows, lanes] = gate * _gelu_x2(a_cur[rows, lanes]).astype(BF16)

    def step(a_new, a_cur, z_new, z_old):
        n_gate = n_grp * n_tc
        c = 0
        for k in range(2 * MXU_CHUNKS):
            if k % 2 == 0:
                act_chunk(a_new, k // 2)
            else:
                val_chunk(z_old, k // 2)
            while c < (k + 1) * n_gate // (2 * MXU_CHUNKS):
                gate_chunk(a_cur, z_new, c)
                c += 1

    @pl.when(s % 2 == 0)
    def _():
        step(a0_ref, a1_ref, z1_ref, z0_ref)

    @pl.when(s % 2 == 1)
    def _():
        step(a1_ref, a0_ref, z0_ref, z1_ref)

    @pl.when((s >= 2) & ((s - 2) % n_tiles == n_tiles - 1))
    def _():
        xo_ref[...] = x_ref[...] + acc_ref[...].T
        acc_ref[...] = jnp.zeros_like(acc_ref)


def _peer_dense(x, hb, u_all, vt_all, layer, lam1, rank2, p1, p2):
    n, d = x.shape
    n_exp = u_all.shape[1]
    _, n_heads, nk, t = lam1.shape
    te = DENSE_EXPERTS
    assert n % t == 0 and n_exp % te == 0 and te % (SUBLANES * nk) == 0 and n_exp == nk * nk and t % LANES == 0
    ne = n_exp // te
    assert vt_all.shape[1:] == (ne, d, te)
    n_items = (n // t) * ne
    assert ne >= 2
    item = lambda s, lag: jnp.clip(s - lag, 0, n_items - 1)
    tok = lambda lag: pl.BlockSpec((t, d), lambda s: (item(s, lag) // ne, 0))
    sel = pl.BlockSpec((None, n_heads, nk, t), lambda s: (item(s, 1) // ne, 0, 0, 0))
    return pl.pallas_call(
        functools.partial(_peer_dense_kernel, n_heads=n_heads, n_tiles=ne),
        grid=(n_items + 2,),
        in_specs=[tok(2), tok(0),
                  pl.BlockSpec((None, te, d), lambda s: (layer, item(s, 0) % ne, 0)),
                  pl.BlockSpec((None, None, d, te), lambda s: (layer, item(s, 2) % ne, 0, 0)),
                  sel, sel, sel, sel],
        out_specs=tok(2),
        out_shape=jax.ShapeDtypeStruct((n, d), F32),
        scratch_shapes=[pltpu.VMEM((d, t), F32), pltpu.VMEM((te, t), F32), pltpu.VMEM((te, t), F32),
                        pltpu.VMEM((te, t), BF16), pltpu.VMEM((te, t), BF16)],
        compiler_params=_cparams(("arbitrary",)),
        name="peer_dense",
    )(x, hb, u_all, vt_all, lam1, rank2, p1, p2)


def _peer_tables_kernel(u_ref, v_ref, ub_ref, vt_ref):
    ub_ref[...] = u_ref[...].astype(BF16)
    vt_ref[...] = v_ref[...].T.astype(BF16)


def _peer_tables(peer_u, peer_v):
    n_l, n_exp, d = peer_u.shape
    te = ROW_BLOCK
    per = DENSE_EXPERTS // te
    assert n_exp % DENSE_EXPERTS == 0 and DENSE_EXPERTS % te == 0
    src = pl.BlockSpec((None, te, d), lambda l, j: (l, j, 0))
    return pl.pallas_call(
        _peer_tables_kernel,
        grid=(n_l, n_exp // te),
        in_specs=[src, src],
        out_specs=[src, pl.BlockSpec((None, None, d, te), lambda l, j: (l, j // per, 0, j % per))],
        out_shape=[jax.ShapeDtypeStruct((n_l, n_exp, d), BF16),
                   jax.ShapeDtypeStruct((n_l, n_exp // DENSE_EXPERTS, d, DENSE_EXPERTS), BF16)],
        compiler_params=_cparams(("arbitrary", "arbitrary")),
        name="peer_tables",
    )(peer_u, peer_v)


def _peer(x, g, wqt_bf, sk, u_all, vt_all, layer):
    n = x.shape[0]
    pad = (-n) % LANES
    xp = jnp.pad(x, ((0, pad), (0, 0))) if pad else x
    hb, lam1, rank2, p1, p2 = _peer_select(xp, g, wqt_bf, sk)
    out = _peer_dense(xp, hb, u_all, vt_all, layer, lam1, rank2, p1, p2)
    return out[:n] if pad else out


def _final_norm_kernel(x_ref, g_ref, o_ref):
    o_ref[...] = _rms(x_ref[...], g_ref[...])


def _final_norm(x, g):
    n, d = x.shape
    tm = min(ROW_BLOCK, n)
    assert n % tm == 0
    return pl.pallas_call(
        _final_norm_kernel,
        grid=(n // tm,),
        in_specs=[pl.BlockSpec((tm, d), lambda i: (i, 0)), pl.BlockSpec((1, d), lambda i: (0, 0))],
        out_specs=pl.BlockSpec((tm, d), lambda i: (i, 0)),
        out_shape=jax.ShapeDtypeStruct((n, d), F32),
        compiler_params=_cparams(("arbitrary",)),
        name="final_norm",
    )(x, g)


def kernel(x_prompt, x_sample, state_conv, state_win_k, state_win_v, state_pool, norm_mix, w_in_ab, conv_w,
           w_out_ab, pool_w, pool_b, pool_scale, norm_ffn, peer_wq, peer_subkeys, peer_u, peer_v, norm_final):
    b, s, d = x_prompt.shape
    bs, ts, _ = x_sample.shape
    assert ts == 1
    depth = norm_mix.shape[0]
    d_conv = conv_w.shape[2]
    n_heads_b, hd = state_win_k.shape[3], state_win_k.shape[4]
    d_att = n_heads_b * hd
    wb_p = min(max(w for w, _ in DILATED_PAIRS), s)
    pool_state = state_pool.shape[2]
    q_scale = float(hd) ** -0.5
    n_peer_heads = peer_subkeys.shape[1]

    xp = x_prompt.reshape(b * s, d)
    xs = x_sample.reshape(bs, d)
    u_all, vt_all = _peer_tables(peer_u, peer_v)
    conv_p, conv_s, wk_p, wk_s, wv_p, wv_s, pool_p, pool_s = [], [], [], [], [], [], [], []
    for l in range(depth):
        g_mix = norm_mix[l][None]
        if l % 2 == 0:
            e = l // 2
            w_in = w_in_ab[e].astype(BF16)
            wa = w_out_ab[e, :d_conv].astype(BF16)
            wbm = w_out_ab[e, d_conv:].astype(BF16)
            ya, k, v, ul, qkv = _even_in_prompt(xp.reshape(b, s, d), g_mix, w_in, conv_w[e], d_conv, d_att, q_scale)
            branches = [_attn_branch(*qkv[dl], w, dl, n_heads_b) for w, dl in DILATED_PAIRS]
            o_list = [br[0].reshape(b * s // dl, dl * d_att) for br, (_, dl) in zip(branches, DILATED_PAIRS)]
            l_list = [br[1].reshape(b * s // dl, dl * d_att) for br, (_, dl) in zip(branches, DILATED_PAIRS)]
            xp = _out_proj(xp, ya.reshape(b * s, d_conv), o_list + l_list, wa, wbm)
            conv_p.append(ul[:, SUBLANES - 2:])
            wk_p.append(k[:, s - wb_p:].reshape(b, wb_p, n_heads_b, hd))
            wv_p.append(v[:, s - wb_p:].reshape(b, wb_p, n_heads_b, hd))
            cst = state_conv[e]
            ya_s, q_s, k_s, v_s, u_s = _even_in_sample(xs, g_mix, w_in, conv_w[e], cst[:, 1], cst[:, 0],
                                                       d_conv, d_att, q_scale)
            yb_s = _attn_sample(q_s, k_s, v_s, state_win_k, state_win_v, e)
            xs = _out_proj(xs, ya_s, [yb_s], wa, wbm)
            conv_s.append(jnp.stack([cst[:, 1], u_s], axis=1))
            wk_s.append(k_s.reshape(bs, 1, n_heads_b, hd))
            wv_s.append(v_s.reshape(bs, 1, n_heads_b, hd))
        else:
            o = l // 2
            pw = pool_w[o].astype(BF16)
            xp3, hl = _pool_prompt(xp.reshape(b, s, d), g_mix, pw, pool_b[o], pool_scale[o][None])
            xp = xp3.reshape(b * s, d)
            pool_p.append(hl[:, hl.shape[1] - pool_state:])
            st = state_pool[o]
            xs, h_s = _pool_sample(xs, jnp.swapaxes(st, 0, 1), g_mix, pw, pool_b[o], pool_scale[o][None])
            pool_s.append(jnp.concatenate([st[:, 1:], h_s[:, None]], axis=1))
        g_ffn = norm_ffn[l][None]
        wqt = peer_wq[l].T.astype(BF16)
        sk = peer_subkeys[l].reshape(2 * n_peer_heads, N_KEYS, -1)
        sk_hi = sk.astype(BF16)
        sk_lo = (sk - sk_hi.astype(F32)).astype(BF16)
        sk = jnp.concatenate([sk_hi, sk_lo, sk_hi], axis=-1)
        xp = _peer(xp, g_ffn, wqt, sk, u_all, vt_all, l)
        xs = _peer(xs, g_ffn, wqt, sk, u_all, vt_all, l)
    gf = norm_final[None]
    y_prompt = _final_norm(xp, gf).reshape(b, s, d)
    y_sample = _final_norm(xs, gf).reshape(bs, ts, d)
    win_k_s = jnp.concatenate([state_win_k[:, :, 1:], jnp.stack(wk_s)], axis=2)
    win_v_s = jnp.concatenate([state_win_v[:, :, 1:], jnp.stack(wv_s)], axis=2)
    return (y_prompt, y_sample, jnp.stack(conv_p), jnp.stack(conv_s), jnp.stack(wk_p), win_k_s,
            jnp.stack(wv_p), win_v_s, jnp.stack(pool_p), jnp.stack(pool_s))
```

```python
import functools
import math

import jax
import jax.numpy as jnp
from jax import lax
from jax.experimental import pallas as pl
from jax.experimental.pallas import tpu as pltpu

F32 = jnp.float32
BF16 = jnp.bfloat16

NORM_EPS = 1e-6
NEG_INF = -1e30
DILATED_PAIRS = ((128, 1), (512, 4), (2048, 16))
POOL_WINDOWS = (2, 4, 8, 16)
PEER_TOPK = 16
N_KEYS = 128

LANES = 128
SUBLANES = 8
VMEM_LIMIT = 56 * 1024 * 1024

ATT_BLOCK = 128
ROW_BLOCK = 512
PEER_TOKENS = 256
MXU_CHUNKS = 4
DENSE_KEY_GROUP = 16
DENSE_EXPERTS = DENSE_KEY_GROUP * N_KEYS


def _cparams(sem, flags=None):
    return pltpu.CompilerParams(dimension_semantics=sem, vmem_limit_bytes=VMEM_LIMIT, flags=flags)


def _rms(x, g):
    r = lax.rsqrt(jnp.mean(x * x, axis=-1, keepdims=True) + NORM_EPS)
    return (x * r) * g


def _bdot(a, b):
    return jnp.dot(a.astype(BF16), b.astype(BF16), preferred_element_type=F32)


def _bdot_nt(a, b):
    return lax.dot_general(a.astype(BF16), b.astype(BF16), (((1,), (1,)), ((), ())),
                           preferred_element_type=F32)


def _even_in_body(x, g, w, cw, u1_fn, d_conv, d_att, q_scale):
    h = _rms(x, g)
    p = _bdot(h, w)
    gate_b = p[:, 0:d_conv]
    gate_c = p[:, d_conv:2 * d_conv]
    xv = p[:, 2 * d_conv:3 * d_conv]
    o = 3 * d_conv
    q = p[:, o:o + d_att] * q_scale
    k = p[:, o + d_att:o + 2 * d_att]
    v = p[:, o + 2 * d_att:o + 3 * d_att]
    u = gate_c * xv
    u1, u2 = u1_fn(u)
    y = cw[0:1] * u2 + cw[1:2] * u1 + cw[2:3] * u
    return gate_b * y, q, k, v, u


def _even_in_seq_kernel(x_ref, g_ref, w_ref, cw_ref, ya_ref, q_ref, k_ref, v_ref, ul_ref, *rest,
                        d_conv, d_att, q_scale):
    dil_refs, carry_ref, stage_ref = rest[:-2], rest[-2], rest[-1]
    s = pl.program_id(1)

    @pl.when(s == 0)
    def _():
        carry_ref[...] = jnp.zeros_like(carry_ref)

    prev = carry_ref[...]
    tm = x_ref.shape[0]

    def shifted(u):
        rows = lax.broadcasted_iota(jnp.int32, u.shape, 0)
        u1 = jnp.where(rows == 0, prev[7:8], pltpu.roll(u, 1, 0))
        u2 = pltpu.roll(u, 2, 0)
        u2 = jnp.where(rows == 0, prev[6:7], jnp.where(rows == 1, prev[7:8], u2))
        return u1, u2

    ya, q, k, v, u = _even_in_body(x_ref[...], g_ref[...], w_ref[...], cw_ref[...], shifted,
                                   d_conv, d_att, q_scale)
    ya_ref[...] = ya
    q_ref[...] = q
    k_ref[...] = k
    v_ref[...] = v
    last = u[tm - SUBLANES:tm]
    carry_ref[...] = last
    ul_ref[...] = last
    n_cb = d_att // LANES
    for j, val in enumerate((q, k, v)):
        for c in range(n_cb):
            stage_ref[j, c] = val[:, c * LANES:(c + 1) * LANES]
    for i, ref in enumerate(dil_refs):
        rows = ref.shape[0]
        dil = tm // rows
        for r in range(dil):
            for c in range(n_cb):
                col = r * d_att + c * LANES
                ref[:, col:col + LANES] = stage_ref[i % 3, c, pl.ds(r, rows, stride=dil), :]


def _even_in_rows_kernel(x_ref, g_ref, w_ref, cw_ref, u1_ref, u2_ref, ya_ref, q_ref, k_ref, v_ref, u_ref,
                         *, d_conv, d_att, q_scale):
    ya, q, k, v, u = _even_in_body(x_ref[...], g_ref[...], w_ref[...], cw_ref[...],
                                   lambda _: (u1_ref[...], u2_ref[...]), d_conv, d_att, q_scale)
    ya_ref[...] = ya
    q_ref[...] = q
    k_ref[...] = k
    v_ref[...] = v
    u_ref[...] = u


def _even_in_prompt(x, g, w_bf, cw, d_conv, d_att, q_scale):
    b, s, d = x.shape
    tm = min(ROW_BLOCK, s)
    assert s % tm == 0 and tm % SUBLANES == 0
    ncol = w_bf.shape[1]
    row = lambda c: pl.BlockSpec((None, tm, c), lambda i, j: (i, j, 0))
    full = lambda shp: pl.BlockSpec(shp, lambda i, j: (0,) * len(shp))
    dils = [dl for _, dl in DILATED_PAIRS if dl > 1]
    assert all(tm % (dl * SUBLANES) == 0 for dl in dils)
    dil_specs = [pl.BlockSpec((None, tm // dl, dl * d_att), lambda i, j: (i, j, 0)) for dl in dils for _ in range(3)]
    dil_shapes = [jax.ShapeDtypeStruct((b, s // dl, dl * d_att), F32) for dl in dils for _ in range(3)]
    outs = pl.pallas_call(
        functools.partial(_even_in_seq_kernel, d_conv=d_conv, d_att=d_att, q_scale=q_scale),
        grid=(b, s // tm),
        in_specs=[row(d), full((1, d)), full((d, ncol)), full((cw.shape[0], d_conv))],
        out_specs=[row(d_conv), row(d_att), row(d_att), row(d_att),
                   pl.BlockSpec((None, SUBLANES, d_conv), lambda i, j: (i, 0, 0))] + dil_specs,
        out_shape=[jax.ShapeDtypeStruct((b, s, d_conv), F32)] + [jax.ShapeDtypeStruct((b, s, d_att), F32)] * 3
                  + [jax.ShapeDtypeStruct((b, SUBLANES, d_conv), F32)] + dil_shapes,
        scratch_shapes=[pltpu.VMEM((SUBLANES, d_conv), F32), pltpu.VMEM((3, d_att // LANES, tm, LANES), F32)],
        compiler_params=_cparams(("arbitrary", "arbitrary")),
        name="even_in_prompt",
    )(x, g, w_bf, cw)
    ya, q, k, v, ul = outs[:5]
    qkv = {1: (q, k, v)}
    for i, dl in enumerate(dils):
        qkv[dl] = tuple(outs[5 + 3 * i:8 + 3 * i])
    return ya, k, v, ul, qkv


def _even_in_sample(x, g, w_bf, cw, u1, u2, d_conv, d_att, q_scale):
    n, d = x.shape
    ncol = w_bf.shape[1]
    full = lambda shp: pl.BlockSpec(shp, lambda i: (0,) * len(shp))
    return pl.pallas_call(
        functools.partial(_even_in_rows_kernel, d_conv=d_conv, d_att=d_att, q_scale=q_scale),
        grid=(1,),
        in_specs=[full((n, d)), full((1, d)), full((d, ncol)), full((cw.shape[0], d_conv)),
                  full((n, d_conv)), full((n, d_conv))],
        out_specs=[full((n, d_conv)), full((n, d_att)), full((n, d_att)), full((n, d_att)), full((n, d_conv))],
        out_shape=[jax.ShapeDtypeStruct((n, d_conv), F32)] + [jax.ShapeDtypeStruct((n, d_att), F32)] * 3
                  + [jax.ShapeDtypeStruct((n, d_conv), F32)],
        compiler_params=_cparams(("arbitrary",)),
        name="even_in_sample",
    )(x, g, w_bf, cw, u1, u2)


def _attn_branch_kernel(q_ref, kp_ref, kc_ref, vp_ref, vc_ref, o_ref, l_ref, *, n_heads, hd, n_back):
    n = pl.program_id(2)
    blk = q_ref.shape[0]
    q = q_ref[...]
    k = jnp.concatenate([kp_ref[...], kc_ref[...]], axis=0)
    v = jnp.concatenate([vp_ref[...], vc_ref[...]], axis=0)
    qi = lax.broadcasted_iota(jnp.int32, (blk, 2 * blk), 0)
    ki = lax.broadcasted_iota(jnp.int32, (blk, 2 * blk), 1)
    dist = qi + blk - ki
    has_prev = jnp.where(n > 0, 0, blk)
    mask = (dist >= 0) & (dist <= n_back) & (ki >= has_prev)
    o_parts, l_parts = [], []
    for h in range(n_heads):
        sl = slice(h * hd, (h + 1) * hd)
        s = _bdot_nt(q[:, sl], k[:, sl])
        s = jnp.where(mask, s, NEG_INF)
        m = jnp.max(s, axis=-1, keepdims=True)
        p = jnp.exp(s - m)
        den = jnp.sum(p, axis=-1, keepdims=True)
        o_parts.append(_bdot(p, v[:, sl]) / den)
        l_parts.append(jnp.broadcast_to(m + jnp.log(den), (blk, hd)))
    o_ref[...] = jnp.concatenate(o_parts, axis=1)
    l_ref[...] = jnp.concatenate(l_parts, axis=1)


def _attn_branch(q, k, v, window, dil, n_heads):
    b, L, dda = q.shape
    da = dda // dil
    hd = da // n_heads
    n_back = window // dil
    assert L % ATT_BLOCK == 0 and n_back <= ATT_BLOCK
    nb = L // ATT_BLOCK
    cur = pl.BlockSpec((None, ATT_BLOCK, da), lambda i, r, n: (i, n, r))
    prev = pl.BlockSpec((None, ATT_BLOCK, da), lambda i, r, n: (i, jnp.maximum(n - 1, 0), r))
    o, l = pl.pallas_call(
        functools.partial(_attn_branch_kernel, n_heads=n_heads, hd=hd, n_back=n_back),
        grid=(b, dil, nb),
        in_specs=[cur, prev, cur, prev, cur],
        out_specs=[cur, cur],
        out_shape=[jax.ShapeDtypeStruct((b, L, dil * da), F32)] * 2,
        compiler_params=_cparams(("arbitrary", "arbitrary", "arbitrary")),
        name=f"attn_branch_d{dil}",
    )(q, k, k, v, v)
    return o, l


def _attn_sample_kernel(q_ref, kn_ref, vn_ref, *refs):
    n_br = (len(refs) - 1) // 2
    k_refs, v_refs, y_ref = refs[:n_br], refs[n_br:2 * n_br], refs[-1]
    q = q_ref[...][None]
    kn = kn_ref[...][None]
    vn = vn_ref[...][None]
    s0 = jnp.sum(kn * q, axis=-1, keepdims=True)
    o_list, l_list = [], []
    for g in range(n_br):
        s = jnp.sum(k_refs[g][...] * q, axis=-1, keepdims=True)
        m = jnp.maximum(jnp.max(s, axis=0, keepdims=True), s0)
        p = jnp.exp(s - m)
        p0 = jnp.exp(s0 - m)
        den = jnp.sum(p, axis=0, keepdims=True) + p0
        o_list.append((jnp.sum(p * v_refs[g][...], axis=0, keepdims=True) + p0 * vn) / den)
        l_list.append(m + jnp.log(den))
    y_ref[...] = _merge_branches(o_list, l_list)[0]


def _merge_branches(o_list, l_list):
    m = functools.reduce(jnp.maximum, l_list)
    e = [jnp.exp(l - m) for l in l_list]
    num = functools.reduce(lambda a, b: a + b, [ei * oi for ei, oi in zip(e, o_list)])
    return num / functools.reduce(lambda a, b: a + b, e)


def _attn_sample(q, k_new, v_new, k_state_all, v_state_all, layer):
    b, da = q.shape
    n_l, _, wb, n_heads, hd = k_state_all.shape
    row = pl.BlockSpec((None, n_heads, hd), lambda i: (i, 0, 0))
    ins, specs = [], []
    for st in (k_state_all, v_state_all):
        for window, dil in DILATED_PAIRS:
            n_back = window // dil
            assert n_back * dil <= wb and wb % dil == 0 and (wb // dil) % n_back == 0
            L = wb // dil
            ins.append(st.reshape(n_l, b, L, dil, n_heads, hd))
            specs.append(pl.BlockSpec((None, None, n_back, None, n_heads, hd),
                                      lambda i, L=L, nbk=n_back: (layer, i, L // nbk - 1, 0, 0, 0)))
    as_heads = lambda t: t.reshape(b, n_heads, hd)
    y = pl.pallas_call(
        _attn_sample_kernel,
        grid=(b,),
        in_specs=[row, row, row] + specs,
        out_specs=row,
        out_shape=jax.ShapeDtypeStruct((b, n_heads, hd), F32),
        compiler_params=_cparams(("arbitrary",)),
        name="attn_sample",
    )(as_heads(q), as_heads(k_new), as_heads(v_new), *ins)
    return y.reshape(b, da)


def _out_proj_kernel(x_ref, ya_ref, *refs, n_in):
    br = refs[:n_in]
    wa_ref, wb_ref, xo_ref = refs[n_in:n_in + 3]
    scratch = list(refs[n_in + 3:])
    tm, da = ya_ref.shape[0], wb_ref.shape[0]

    def token_order(ref):
        rows = ref.shape[0]
        if rows == tm:
            return ref[...]
        dil = tm // rows
        sc = scratch.pop(0)
        n_cb = da // LANES
        for r in range(dil):
            for c in range(n_cb):
                col = r * da + c * LANES
                sc[c, pl.ds(r, rows, stride=dil), :] = ref[:, col:col + LANES]
        return jnp.concatenate([sc[c] for c in range(n_cb)], axis=1)

    vals = [token_order(r) for r in br]
    yb = vals[0] if n_in == 1 else _merge_branches(vals[:n_in // 2], vals[n_in // 2:])
    xo_ref[...] = x_ref[...] + _bdot(ya_ref[...], wa_ref[...]) + _bdot(yb, wb_ref[...])


def _out_proj(x, ya, branch_arrays, wa_bf, wb_bf):
    n, d = x.shape
    da = wb_bf.shape[0]
    tm = min(ROW_BLOCK, n)
    assert n % tm == 0
    row = lambda c: pl.BlockSpec((tm, c), lambda i: (i, 0))
    full = lambda shp: pl.BlockSpec(shp, lambda i: (0,) * len(shp))
    dils = [a.shape[1] // da for a in branch_arrays]
    assert all(a.shape == (n // dl, dl * da) and (dl == 1 or tm % (dl * SUBLANES) == 0)
               for a, dl in zip(branch_arrays, dils))
    return pl.pallas_call(
        functools.partial(_out_proj_kernel, n_in=len(branch_arrays)),
        grid=(n // tm,),
        in_specs=[row(d), row(ya.shape[1])]
                 + [pl.BlockSpec((tm // dl, dl * da), lambda i: (i, 0)) for dl in dils]
                 + [full(wa_bf.shape), full(wb_bf.shape)],
        out_specs=row(d),
        out_shape=jax.ShapeDtypeStruct((n, d), F32),
        scratch_shapes=[pltpu.VMEM((da // LANES, tm, LANES), F32) for dl in dils if dl > 1],
        compiler_params=_cparams(("arbitrary",)),
        name="out_proj",
    )(x, ya, *branch_arrays, wa_bf, wb_bf)


def _pool_groups(dmat, w_ref, b_ref, sc):
    n_g = w_ref.shape[0]
    ys = [_bdot(dmat[g], w_ref[g]) + b_ref[g:g + 1] for g in range(n_g)]
    return jnp.concatenate(ys, axis=1) * sc


def _pool_prompt_kernel(x_ref, g_ref, w_ref, b_ref, sc_ref, xo_ref, hl_ref, carry_ref):
    s = pl.program_id(1)
    hist = carry_ref.shape[0]

    @pl.when(s == 0)
    def _():
        carry_ref[...] = jnp.zeros_like(carry_ref)

    x = x_ref[...]
    tm = x.shape[0]
    h = _rms(x, g_ref[...])
    ext = jnp.concatenate([carry_ref[...], h], axis=0)
    pos = s * tm + lax.broadcasted_iota(jnp.int32, (tm, 1), 0) + 1
    gw = w_ref.shape[1]
    acc = ext
    width = 1
    diffs = []
    for g, w in enumerate(POOL_WINDOWS):
        while width < w:
            acc = acc + pltpu.roll(acc, width, 0)
            width *= 2
        cols = slice(g * gw, (g + 1) * gw)
        win = acc[hist:, cols]
        div = jnp.minimum(pos, w).astype(F32)
        diffs.append(win / div - h[:, cols])
    xo_ref[...] = x + _pool_groups(diffs, w_ref, b_ref, sc_ref[...])
    last = ext[tm:tm + hist]
    carry_ref[...] = last
    hl_ref[...] = last


def _pool_prompt(x, g, w_bf, bias, scale):
    b, s, d = x.shape
    tm = min(ROW_BLOCK, s)
    hist = 16
    assert s % tm == 0 and tm >= hist and max(POOL_WINDOWS) <= hist
    assert all(w == 2 ** (i + 1) for i, w in enumerate(POOL_WINDOWS))
    row = pl.BlockSpec((None, tm, d), lambda i, j: (i, j, 0))
    full = lambda shp: pl.BlockSpec(shp, lambda i, j: (0,) * len(shp))
    return pl.pallas_call(
        _pool_prompt_kernel,
        grid=(b, s // tm),
        in_specs=[row, full((1, d)), full(w_bf.shape), full(bias.shape), full((1, d))],
        out_specs=[row, pl.BlockSpec((None, hist, d), lambda i, j: (i, 0, 0))],
        out_shape=[jax.ShapeDtypeStruct((b, s, d), F32), jax.ShapeDtypeStruct((b, hist, d), F32)],
        scratch_shapes=[pltpu.VMEM((hist, d), F32)],
        compiler_params=_cparams(("arbitrary", "arbitrary")),
        name="pool_prompt",
    )(x, g, w_bf, bias, scale)


def _pool_sample_kernel(x_ref, st_ref, g_ref, w_ref, b_ref, sc_ref, xo_ref, h_ref):
    x = x_ref[...]
    h = _rms(x, g_ref[...])
    n_st = st_ref.shape[0]
    gw = w_ref.shape[1]
    diffs = []
    for g, w in enumerate(POOL_WINDOWS):
        cols = slice(g * gw, (g + 1) * gw)
        tot = h[:, cols]
        for j in range(1, w):
            tot = tot + st_ref[n_st - j][:, cols]
        diffs.append(tot / float(w) - h[:, cols])
    xo_ref[...] = x + _pool_groups(diffs, w_ref, b_ref, sc_ref[...])
    h_ref[...] = h


def _pool_sample(x, state_t, g, w_bf, bias, scale):
    n, d = x.shape
    assert state_t.shape[0] + 1 >= max(POOL_WINDOWS)
    full = lambda shp: pl.BlockSpec(shp, lambda i: (0,) * len(shp))
    return pl.pallas_call(
        _pool_sample_kernel,
        grid=(1,),
        in_specs=[full(x.shape), full(state_t.shape), full((1, d)), full(w_bf.shape), full(bias.shape),
                  full((1, d))],
        out_specs=[full(x.shape), full(x.shape)],
        out_shape=[jax.ShapeDtypeStruct((n, d), F32)] * 2,
        compiler_params=_cparams(("arbitrary",)),
        name="pool_sample",
    )(x, state_t, g, w_bf, bias, scale)


def _top16_ranked(s):
    n, t = s.shape
    rows = lax.broadcasted_iota(jnp.int32, (n, t), 0).astype(F32)
    rows16 = lax.broadcasted_iota(jnp.int32, (PEER_TOPK, t), 0)

    def step(it, carry):
        s, rank, sv = carry
        m = jnp.max(s, axis=0, keepdims=True)
        first = jnp.min(jnp.where(s == m, rows, float(n)), axis=0, keepdims=True)
        sel = rows == first
        itf = jnp.asarray(it, jnp.int32).astype(F32)
        return (jnp.where(sel, -jnp.inf, s), jnp.where(sel, itf, rank), jnp.where(rows16 == it, m, sv))

    init = (s, jnp.full((n, t), float(PEER_TOPK), F32), jnp.zeros((PEER_TOPK, t), F32))
    _, rank, sv = lax.fori_loop(0, PEER_TOPK, step, init)
    return rank, sv


def _sort_network(n_in):
    n, pairs, p = PEER_TOPK, [], 1
    assert n_in <= n
    while p < n:
        k = p
        while k >= 1:
            for j in range(k % p, n - k, 2 * k):
                for i in range(min(k, n - j - k)):
                    if (i + j) // (2 * p) == (i + j + k) // (2 * p):
                        pairs.append((i + j, i + j + k))
            k //= 2
        p *= 2
    return [(i, j) for i, j in pairs if j < n_in]


def _largest_distinct(groups, k):
    v = list(groups)
    n_g = len(v)
    for i, j in _sort_network(n_g):
        v[i], v[j] = jnp.maximum(v[i], v[j]), jnp.minimum(v[i], v[j])
    out = []
    for it in range(k):
        m = jnp.max(v[0], axis=0, keepdims=True)
        out.append(m)
        keep = min(n_g, k - it)
        pop = v[0] == m
        for j in range(keep - 1):
            v[j] = jnp.where(pop, v[j + 1], v[j])
        if keep == n_g and keep > 1:
            v[n_g - 1] = jnp.where(pop, -jnp.inf, v[n_g - 1])
    return out


def _top16_values(s):
    n, t = s.shape
    tops = _largest_distinct([s[SUBLANES * j:SUBLANES * (j + 1)] for j in range(n // SUBLANES)], PEER_TOPK)
    rows16 = lax.broadcasted_iota(jnp.int32, (PEER_TOPK, t), 0)
    sv = jnp.zeros((PEER_TOPK, t), F32)
    for it, m in enumerate(tops):
        sv = jnp.where(rows16 == it, m, sv)
    cover = jnp.sum(jnp.where(s >= tops[-1], 1.0, 0.0), axis=0, keepdims=True)
    return sv, cover


def _pair_cells(sv1, sv2):
    t = sv1.shape[1]
    half = SUBLANES
    tiles = [sv1[0:1] + sv2[0:half], sv1[0:1] + sv2[half:2 * half]]
    tiles += [sv1[r:r + 1] + sv2[0:half] for r in range(1, half)]
    tiles += [sv1[half:2 * half] + sv2[0:1]]
    cand0 = jnp.concatenate(tiles, axis=0)
    nrow = cand0.shape[0]
    i = lax.broadcasted_iota(jnp.int32, (nrow, t), 0)
    j = i - 2 * half
    mid = (lax.shift_right_arithmetic(j, 3) + 1) * PEER_TOPK + lax.bitwise_and(j, half - 1)
    idx = jnp.where(i < 2 * half, i,
                    jnp.where(i < nrow - half, mid, (i - (nrow - half) + half) * PEER_TOPK)).astype(F32)
    return cand0, idx


def _pairs_exact(cand0, idx):
    def step(_, carry):
        cand, chosen = carry
        m = jnp.max(cand, axis=0, keepdims=True)
        first = jnp.min(jnp.where(cand == m, idx, 1e9), axis=0, keepdims=True)
        sel = idx == first
        return jnp.where(sel, -jnp.inf, cand), jnp.where(sel, 1.0, chosen)

    _, chosen = lax.fori_loop(0, PEER_TOPK, step, (cand0, jnp.zeros_like(cand0)))
    return chosen


def _pairs_fast(cand0):
    tiles = [cand0[SUBLANES * j:SUBLANES * (j + 1)] for j in range(cand0.shape[0] // SUBLANES)]
    return jnp.where(cand0 >= _largest_distinct(tiles, PEER_TOPK)[-1], 1.0, 0.0)


def _lam_den(chosen, cand0):
    half = SUBLANES
    nrow = cand0.shape[0]
    den = jnp.sum(jnp.where(chosen > 0, jnp.exp(cand0 - cand0[0:1]), 0.0), axis=0, keepdims=True)
    lam = [jnp.sum(chosen[0:2 * half], axis=0, keepdims=True)]
    lam += [jnp.sum(chosen[2 * half + half * (r - 1):2 * half + half * r], axis=0, keepdims=True)
            for r in range(1, half)]
    lam += [chosen[nrow - half + r:nrow - half + r + 1] for r in range(half)]
    return lam, den


def _peer_select_kernel(x_ref, g_ref, wqt_ref, sk_ref, hb_ref, lam1_ref, rank2_ref, p1_ref, p2_ref, s_ref,
                        *, n_heads):
    h = _rms(x_ref[...], g_ref[...])
    hb = h.astype(BF16)
    hb_ref[...] = hb
    nk = sk_ref.shape[1]
    qt = lax.dot_general(wqt_ref[...], hb, (((1,), (1,)), ((), ())), preferred_element_type=F32)
    q_hi = qt.astype(BF16)
    q_lo = (qt - q_hi.astype(F32)).astype(BF16)
    for c in range(2 * n_heads):
        rows = slice(c * nk, (c + 1) * nk)
        q3 = jnp.concatenate([q_hi[rows], q_hi[rows], q_lo[rows]], axis=0)
        s_ref[c] = jnp.dot(sk_ref[c], q3, preferred_element_type=F32)

    def emit(hh, lanes, lam1, rank2, s1, s2, top1, top2, den):
        lam1_ref[hh, :, lanes] = lam1
        rank2_ref[hh, :, lanes] = rank2.astype(rank2_ref.dtype)
        p1_ref[hh, :, lanes] = jnp.exp(s1 - top1) / (2.0 * den)
        p2_ref[hh, :, lanes] = jnp.exp(s2 - top2).astype(p2_ref.dtype)

    for hh, lc in ((hh, lc) for hh in range(n_heads) for lc in range(x_ref.shape[0] // LANES)):
        lanes = slice(lc * LANES, (lc + 1) * LANES)
        s1 = s_ref[2 * hh, :, lanes]
        s2 = s_ref[2 * hh + 1, :, lanes]
        sv1, cov1 = _top16_values(s1)
        sv2, cov2 = _top16_values(s2)
        cand0, _ = _pair_cells(sv1, sv2)
        chosen = _pairs_fast(cand0)
        lam, den = _lam_den(chosen, cand0)
        lam1 = jnp.broadcast_to(lam[0], s1.shape)
        rank2 = jnp.zeros_like(s2)
        for r in range(PEER_TOPK):
            lam1 = jnp.where(sv1[r:r + 1] > s1, lam[r + 1] if r + 1 < PEER_TOPK else 0.0, lam1)
            rank2 = jnp.where(sv2[r:r + 1] > s2, float(r + 1), rank2)
        emit(hh, lanes, lam1, rank2, s1, s2, sv1[0:1], sv2[0:1], den)
        cov3 = jnp.sum(chosen, axis=0, keepdims=True)
        k = float(PEER_TOPK)
        tied = jnp.max(jnp.abs(cov1 - k) + jnp.abs(cov2 - k) + jnp.abs(cov3 - k)) > 0.0

        @pl.when(tied)
        def _(hh=hh, lanes=lanes):
            s1 = s_ref[2 * hh, :, lanes]
            s2 = s_ref[2 * hh + 1, :, lanes]
            rank1, sv1 = _top16_ranked(s1)
            rank2, sv2 = _top16_ranked(s2)
            cand0, idx = _pair_cells(sv1, sv2)
            lam, den = _lam_den(_pairs_exact(cand0, idx), cand0)
            lam1 = jnp.zeros_like(rank1)
            for r in range(PEER_TOPK):
                lam1 = jnp.where(rank1 == float(r), lam[r], lam1)
            emit(hh, lanes, lam1, rank2, s1, s2, sv1[0:1], sv2[0:1], den)


def _peer_select(x, g, wqt_bf, sk):
    n, d = x.shape
    n_heads = sk.shape[0] // 2
    nk = sk.shape[1]
    t = min(PEER_TOKENS, n)
    assert n % t == 0 and nk == N_KEYS
    full = lambda shp: pl.BlockSpec(shp, lambda i: (0,) * len(shp))
    sel = pl.BlockSpec((None, n_heads, nk, t), lambda i: (i, 0, 0, 0))
    sel_f32 = jax.ShapeDtypeStruct((n // t, n_heads, nk, t), F32)
    sel_bf16 = jax.ShapeDtypeStruct((n // t, n_heads, nk, t), BF16)
    return pl.pallas_call(
        functools.partial(_peer_select_kernel, n_heads=n_heads),
        grid=(n // t,),
        in_specs=[pl.BlockSpec((t, d), lambda i: (i, 0)), full((1, d)), full(wqt_bf.shape), full(sk.shape)],
        out_specs=[pl.BlockSpec((t, d), lambda i: (i, 0)), sel, sel, sel, sel],
        out_shape=[jax.ShapeDtypeStruct((n, d), BF16), sel_f32, sel_bf16, sel_f32, sel_bf16],
        scratch_shapes=[pltpu.VMEM((2 * n_heads, nk, t), F32)],
        compiler_params=_cparams(("arbitrary",)),
        name="peer_select",
    )(x, g, wqt_bf, sk)


def _gelu_x2(a):
    return a * (1.0 + lax.erf(a * (1.0 / math.sqrt(2.0))))


def _peer_dense_kernel(x_ref, hb_ref, u_ref, vt_ref, lam1_ref, rank2_ref, p1_ref, p2_ref, xo_ref,
                       acc_ref, a_ref, z0_ref, z1_ref, *, n_heads, n_tiles):
    s = pl.program_id(0)
    n_items = pl.num_programs(0) - 1
    t = hb_ref.shape[0]
    nk = rank2_ref.shape[1]

    @pl.when(s == 0)
    def _():
        acc_ref[...] = jnp.zeros_like(acc_ref)
        z1_ref[...] = jnp.zeros_like(z1_ref)

    n_grp = u_ref.shape[0] // nk
    eb = jnp.minimum(s, n_items - 1) % n_tiles
    grp = pl.ds(pl.multiple_of(eb * n_grp, SUBLANES), n_grp)

    te = u_ref.shape[0]
    d = vt_ref.shape[0]
    n_tc = t // LANES

    def act_chunk(j):
        rows = slice(j * (te // MXU_CHUNKS), (j + 1) * (te // MXU_CHUNKS))
        a_ref[rows, :] = lax.dot_general(u_ref[rows, :], hb_ref[...], (((1,), (1,)), ((), ())),
                                         preferred_element_type=F32)

    def val_chunk(z_old, j):
        rows = slice(j * (d // MXU_CHUNKS), (j + 1) * (d // MXU_CHUNKS))
        acc_ref[rows, :] += jnp.dot(vt_ref[rows, :], z_old[...], preferred_element_type=F32)

    def gate_chunk(z_new, c):
        ii, tc = divmod(c, n_tc)
        lanes = slice(tc * LANES, (tc + 1) * LANES)
        rows = slice(ii * nk, (ii + 1) * nk)
        gate = jnp.zeros((nk, LANES), BF16)
        for hh in range(n_heads):
            l1 = jnp.broadcast_to(lam1_ref[hh, grp, lanes][ii:ii + 1], (nk, LANES)).astype(BF16)
            pb = jnp.broadcast_to(p1_ref[hh, grp, lanes][ii:ii + 1], (nk, LANES)).astype(BF16)
            gate = gate + jnp.where(l1 > rank2_ref[hh, :, lanes], pb * p2_ref[hh, :, lanes],
                                    jnp.zeros((), BF16))
        z_new[rows, lanes] = gate * _gelu_x2(a_ref[rows, lanes]).astype(BF16)

    def step(z_new, z_old):
        per = n_grp * n_tc // MXU_CHUNKS
        act_chunk(0)
        for j in range(MXU_CHUNKS):
            if j + 1 < MXU_CHUNKS:
                act_chunk(j + 1)
            val_chunk(z_old, j)
            for c in range(j * per, (j + 1) * per):
                gate_chunk(z_new, c)

    @pl.when(s % 2 == 0)
    def _():
        step(z0_ref, z1_ref)

    @pl.when(s % 2 == 1)
    def _():
        step(z1_ref, z0_ref)

    @pl.when((s >= 1) & ((s - 1) % n_tiles == n_tiles - 1))
    def _():
        xo_ref[...] = x_ref[...] + acc_ref[...].T
        acc_ref[...] = jnp.zeros_like(acc_ref)


def _peer_dense(x, hb, u_all, vt_all, layer, lam1, rank2, p1, p2):
    n, d = x.shape
    n_exp = u_all.shape[1]
    _, n_heads, nk, t = lam1.shape
    te = DENSE_EXPERTS
    assert n % t == 0 and n_exp % te == 0 and te % (SUBLANES * nk) == 0 and n_exp == nk * nk and t % LANES == 0
    ne = n_exp // te
    assert vt_all.shape[1:] == (ne, d, te)
    n_items = (n // t) * ne
    assert ne >= 2
    item = lambda s, lag: jnp.clip(s - lag, 0, n_items - 1)
    tok = lambda lag: pl.BlockSpec((t, d), lambda s: (item(s, lag) // ne, 0))
    sel = pl.BlockSpec((None, n_heads, nk, t), lambda s: (item(s, 0) // ne, 0, 0, 0))
    return pl.pallas_call(
        functools.partial(_peer_dense_kernel, n_heads=n_heads, n_tiles=ne),
        grid=(n_items + 1,),
        in_specs=[tok(1), tok(0),
                  pl.BlockSpec((None, te, d), lambda s: (layer, item(s, 0) % ne, 0)),
                  pl.BlockSpec((None, None, d, te), lambda s: (layer, item(s, 1) % ne, 0, 0)),
                  sel, sel, sel, sel],
        out_specs=tok(1),
        out_shape=jax.ShapeDtypeStruct((n, d), F32),
        scratch_shapes=[pltpu.VMEM((d, t), F32), pltpu.VMEM((te, t), F32),
                        pltpu.VMEM((te, t), BF16), pltpu.VMEM((te, t), BF16)],
        compiler_params=_cparams(("arbitrary",)),
        name="peer_dense",
    )(x, hb, u_all, vt_all, lam1, rank2, p1, p2)


def _peer_tables_kernel(u_ref, v_ref, ub_ref, vt_ref):
    ub_ref[...] = u_ref[...].astype(BF16)
    vt_ref[...] = v_ref[...].T.astype(BF16)


def _peer_tables(peer_u, peer_v):
    n_l, n_exp, d = peer_u.shape
    te = ROW_BLOCK
    per = DENSE_EXPERTS // te
    assert n_exp % DENSE_EXPERTS == 0 and DENSE_EXPERTS % te == 0
    src = pl.BlockSpec((None, te, d), lambda l, j: (l, j, 0))
    return pl.pallas_call(
        _peer_tables_kernel,
        grid=(n_l, n_exp // te),
        in_specs=[src, src],
        out_specs=[src, pl.BlockSpec((None, None, d, te), lambda l, j: (l, j // per, 0, j % per))],
        out_shape=[jax.ShapeDtypeStruct((n_l, n_exp, d), BF16),
                   jax.ShapeDtypeStruct((n_l, n_exp // DENSE_EXPERTS, d, DENSE_EXPERTS), BF16)],
        compiler_params=_cparams(("arbitrary", "arbitrary")),
        name="peer_tables",
    )(peer_u, peer_v)


def _peer(x, g, wqt_bf, sk, u_all, vt_all, layer):
    n = x.shape[0]
    pad = (-n) % LANES
    xp = jnp.pad(x, ((0, pad), (0, 0))) if pad else x
    hb, lam1, rank2, p1, p2 = _peer_select(xp, g, wqt_bf, sk)
    out = _peer_dense(xp, hb, u_all, vt_all, layer, lam1, rank2, p1, p2)
    return out[:n] if pad else out


def _final_norm_kernel(x_ref, g_ref, o_ref):
    o_ref[...] = _rms(x_ref[...], g_ref[...])


def _final_norm(x, g):
    n, d = x.shape
    tm = min(ROW_BLOCK, n)
    assert n % tm == 0
    return pl.pallas_call(
        _final_norm_kernel,
        grid=(n // tm,),
        in_specs=[pl.BlockSpec((tm, d), lambda i: (i, 0)), pl.BlockSpec((1, d), lambda i: (0, 0))],
        out_specs=pl.BlockSpec((tm, d), lambda i: (i, 0)),
        out_shape=jax.ShapeDtypeStruct((n, d), F32),
        compiler_params=_cparams(("arbitrary",)),
        name="final_norm",
    )(x, g)


def kernel(x_prompt, x_sample, state_conv, state_win_k, state_win_v, state_pool, norm_mix, w_in_ab, conv_w,
           w_out_ab, pool_w, pool_b, pool_scale, norm_ffn, peer_wq, peer_subkeys, peer_u, peer_v, norm_final):
    b, s, d = x_prompt.shape
    bs, ts, _ = x_sample.shape
    assert ts == 1
    depth = norm_mix.shape[0]
    d_conv = conv_w.shape[2]
    n_heads_b, hd = state_win_k.shape[3], state_win_k.shape[4]
    d_att = n_heads_b * hd
    wb_p = min(max(w for w, _ in DILATED_PAIRS), s)
    pool_state = state_pool.shape[2]
    q_scale = float(hd) ** -0.5
    n_peer_heads = peer_subkeys.shape[1]

    xp = x_prompt.reshape(b * s, d)
    xs = x_sample.reshape(bs, d)
    u_all, vt_all = _peer_tables(peer_u, peer_v)
    conv_p, conv_s, wk_p, wk_s, wv_p, wv_s, pool_p, pool_s = [], [], [], [], [], [], [], []
    for l in range(depth):
        g_mix = norm_mix[l][None]
        if l % 2 == 0:
            e = l // 2
            w_in = w_in_ab[e].astype(BF16)
            wa = w_out_ab[e, :d_conv].astype(BF16)
            wbm = w_out_ab[e, d_conv:].astype(BF16)
            ya, k, v, ul, qkv = _even_in_prompt(xp.reshape(b, s, d), g_mix, w_in, conv_w[e], d_conv, d_att, q_scale)
            branches = [_attn_branch(*qkv[dl], w, dl, n_heads_b) for w, dl in DILATED_PAIRS]
            o_list = [br[0].reshape(b * s // dl, dl * d_att) for br, (_, dl) in zip(branches, DILATED_PAIRS)]
            l_list = [br[1].reshape(b * s // dl, dl * d_att) for br, (_, dl) in zip(branches, DILATED_PAIRS)]
            xp = _out_proj(xp, ya.reshape(b * s, d_conv), o_list + l_list, wa, wbm)
            conv_p.append(ul[:, SUBLANES - 2:])
            wk_p.append(k[:, s - wb_p:].reshape(b, wb_p, n_heads_b, hd))
            wv_p.append(v[:, s - wb_p:].reshape(b, wb_p, n_heads_b, hd))
            cst = state_conv[e]
            ya_s, q_s, k_s, v_s, u_s = _even_in_sample(xs, g_mix, w_in, conv_w[e], cst[:, 1], cst[:, 0],
                                                       d_conv, d_att, q_scale)
            yb_s = _attn_sample(q_s, k_s, v_s, state_win_k, state_win_v, e)
            xs = _out_proj(xs, ya_s, [yb_s], wa, wbm)
            conv_s.append(jnp.stack([cst[:, 1], u_s], axis=1))
            wk_s.append(k_s.reshape(bs, 1, n_heads_b, hd))
            wv_s.append(v_s.reshape(bs, 1, n_heads_b, hd))
        else:
            o = l // 2
            pw = pool_w[o].astype(BF16)
            xp3, hl = _pool_prompt(xp.reshape(b, s, d), g_mix, pw, pool_b[o], pool_scale[o][None])
            xp = xp3.reshape(b * s, d)
            pool_p.append(hl[:, hl.shape[1] - pool_state:])
            st = state_pool[o]
            xs, h_s = _pool_sample(xs, jnp.swapaxes(st, 0, 1), g_mix, pw, pool_b[o], pool_scale[o][None])
            pool_s.append(jnp.concatenate([st[:, 1:], h_s[:, None]], axis=1))
        g_ffn = norm_ffn[l][None]
        wqt = peer_wq[l].T.astype(BF16)
        sk = peer_subkeys[l].reshape(2 * n_peer_heads, N_KEYS, -1)
        sk_hi = sk.astype(BF16)
        sk_lo = (sk - sk_hi.astype(F32)).astype(BF16)
        sk = jnp.concatenate([sk_hi, sk_lo, sk_hi], axis=-1)
        xp = _peer(xp, g_ffn, wqt, sk, u_all, vt_all, l)
        xs = _peer(xs, g_ffn, wqt, sk, u_all, vt_all, l)
    gf = norm_final[None]
    y_prompt = _final_norm(xp, gf).reshape(b, s, d)
    y_sample = _final_norm(xs, gf).reshape(bs, ts, d)
    win_k_s = jnp.concatenate([state_win_k[:, :, 1:], jnp.stack(wk_s)], axis=2)
    win_v_s = jnp.concatenate([state_win_v[:, :, 1:], jnp.stack(wv_s)], axis=2)
    return (y_prompt, y_sample, jnp.stack(conv_p), jnp.stack(conv_s), jnp.stack(wk_p), win_k_s,
            jnp.stack(wv_p), win_v_s, jnp.stack(pool_p), jnp.stack(pool_s))
```
